```python
import math
import jax, jax.numpy as jnp
from jax import lax
import numpy as np

D_MODEL = 1024
BATCH = 8
SEQ = 4096
DEPTH = 1

DA_HEADS = 4
DA_V_DIM = D_MODEL // (2 * DA_HEADS)
DA_HEAD_DIM = DA_V_DIM // 2
A_QK = DA_HEADS * 2 * DA_HEAD_DIM
A_V = DA_HEADS * DA_V_DIM
Q_BLOCK = 128

GLA_HEADS = 4
GLA_DV = D_MODEL // (2 * GLA_HEADS)
GLA_DK = GLA_DV // 2
B_QK = GLA_HEADS * GLA_DK
B_V = GLA_HEADS * GLA_DV
GLA_GATE_RANK = 16
GLA_TAU = 16.0
GLA_CHUNK = 64

RMS_EPS = 1e-6

SPLITS = (A_QK, A_QK, A_V, A_V,
          B_QK, B_QK, B_V, B_V, GLA_GATE_RANK,
          D_MODEL, D_MODEL)
D_IN = sum(SPLITS)

kernel_name = 'hybrid_diffattn_gla_gated_merge'


def rms_norm(x, g):
    xf = x.astype(jnp.float32)
    y = xf * lax.rsqrt(jnp.mean(xf * xf, axis=-1, keepdims=True) + RMS_EPS)
    return (y * g.astype(jnp.float32)).astype(x.dtype)


def diff_attention(q, k, v, lam, slopes):
    B, H, _, S, DH = q.shape
    nb = S // Q_BLOCK
    scale = DH ** -0.5
    qb = q.reshape(B, H, 2, nb, Q_BLOCK, DH).transpose(3, 0, 1, 2, 4, 5)
    kpos = jnp.arange(S)

    def block(args):
        qi, i = args
        s = jnp.einsum('bhmqd,bhmkd->bhmqk', qi, k).astype(jnp.float32) * scale
        qpos = i * Q_BLOCK + jnp.arange(Q_BLOCK)
        dist = (qpos[:, None] - kpos[None, :]).astype(jnp.float32)
        bias = -slopes[:, None, None, None] * dist
        s = jnp.where(dist >= 0, s + bias, -jnp.inf)
        p = jax.nn.softmax(s, axis=-1)
        pd = p[:, :, 0] - lam * p[:, :, 1]
        return jnp.einsum('bhqk,bhkd->bhqd', pd.astype(v.dtype), v)

    o = lax.map(block, (qb, jnp.arange(nb)))
    return o.transpose(1, 0, 3, 2, 4).reshape(B, S, H, v.shape[-1])


def gla_chunked(q, k, v, glog):
    B, H, S, DK = q.shape
    DV = v.shape[-1]
    n = S // GLA_CHUNK

    def to_chunks(t):
        return t.reshape(B, H, n, GLA_CHUNK, t.shape[-1]).transpose(2, 0, 1, 3, 4)

    tri = jnp.tril(jnp.ones((GLA_CHUNK, GLA_CHUNK), dtype=bool))

    def step(state, inp):
        qc, kc, vc, gc = inp
        b = jnp.cumsum(gc, axis=-2)
        o_inter = jnp.einsum('bhtk,bhkv->bhtv', qc * jnp.exp(b), state)
        rel = jnp.where(tri[:, :, None], b[:, :, :, None, :] - b[:, :, None, :, :], -jnp.inf)
        a = jnp.einsum('bhtk,bhsk,bhtsk->bhts', qc, kc, jnp.exp(rel))
        o_intra = jnp.einsum('bhts,bhsv->bhtv', a, vc)
        b_last = b[:, :, -1:, :]
        state = (jnp.exp(b_last[:, :, 0, :, None]) * state
                 + jnp.einsum('bhsk,bhsv->bhkv', kc * jnp.exp(b_last - b), vc))
        return state, o_inter + o_intra

    s0 = jnp.zeros((B, H, DK, DV), jnp.float32)
    _, o = lax.scan(step, s0, (to_chunks(q), to_chunks(k), to_chunks(v), to_chunks(glog)))
    return o.transpose(1, 0, 3, 2, 4).reshape(B, S, H, DV)


def hybrid_layer(x, g_pre, w_in, lam_q1, lam_k1, lam_q2, lam_k2, g_sub_a,
                 w_alpha, b_alpha, g_sub_b, w_up_a, w_up_b, w_out, g_post, layer_idx):
    B, S, _ = x.shape
    f32 = jnp.float32
    h = rms_norm(x, g_pre)
    proj = h @ w_in
    idx = np.cumsum(np.array(SPLITS))[:-1].tolist()
    qa, ka, va, za, qb, kb, vb, zb, lr, gate_a, gate_b = jnp.split(proj, idx, axis=-1)

    lam_init = 0.8 - 0.6 * math.exp(-0.3 * layer_idx)
    lam = (jnp.exp(jnp.sum(lam_q1.astype(f32) * lam_k1.astype(f32)))
           - jnp.exp(jnp.sum(lam_q2.astype(f32) * lam_k2.astype(f32))) + lam_init)
    slopes = jnp.exp2(-8.0 * jnp.arange(1, DA_HEADS + 1, dtype=f32) / DA_HEADS)
    qa = qa.reshape(B, S, DA_HEADS, 2, DA_HEAD_DIM).transpose(0, 2, 3, 1, 4)
    ka = ka.reshape(B, S, DA_HEADS, 2, DA_HEAD_DIM).transpose(0, 2, 3, 1, 4)
    va = va.reshape(B, S, DA_HEADS, DA_V_DIM).transpose(0, 2, 1, 3)
    oa = diff_attention(qa, ka, va, lam, slopes)
    oa = rms_norm(oa, g_sub_a) * (1.0 - lam_init)
    ya = (oa.reshape(B, S, A_V) * jax.nn.silu(za)) @ w_up_a

    glog = jax.nn.log_sigmoid((lr @ w_alpha + b_alpha).astype(f32)) / GLA_TAU
    qb = qb.reshape(B, S, GLA_HEADS, GLA_DK).transpose(0, 2, 1, 3).astype(f32) * (GLA_DK ** -0.5)
    kb = kb.reshape(B, S, GLA_HEADS, GLA_DK).transpose(0, 2, 1, 3).astype(f32)
    vb = vb.reshape(B, S, GLA_HEADS, GLA_DV).transpose(0, 2, 1, 3).astype(f32)
    glog = glog.reshape(B, S, GLA_HEADS, GLA_DK).transpose(0, 2, 1, 3)
    ob = gla_chunked(qb, kb, vb, glog).astype(x.dtype)
    ob = rms_norm(ob, g_sub_b)
    yb = (ob.reshape(B, S, B_V) * jax.nn.silu(zb)) @ w_up_b

    y = jax.nn.sigmoid(gate_a) * ya + jax.nn.sigmoid(gate_b) * yb
    return x + rms_norm(y @ w_out, g_post)


def setup_inputs(seed: int = 0) -> dict:
    key = jax.random.key(seed)
    ks = jax.random.split(key, 16)
    L = DEPTH
    nrm = jax.random.normal
    f32 = jnp.float32
    return {
        'x': nrm(ks[0], (BATCH, SEQ, D_MODEL), f32),
        'g_pre': 1.0 + 0.05 * nrm(ks[1], (L, D_MODEL), f32),
        'w_in': nrm(ks[2], (L, D_MODEL, D_IN), f32) * D_MODEL ** -0.5,
        'lam_q1': 0.1 * nrm(ks[3], (L, DA_HEAD_DIM), f32),
        'lam_k1': 0.1 * nrm(ks[4], (L, DA_HEAD_DIM), f32),
        'lam_q2': 0.1 * nrm(ks[5], (L, DA_HEAD_DIM), f32),
        'lam_k2': 0.1 * nrm(ks[6], (L, DA_HEAD_DIM), f32),
        'g_sub_a': 1.0 + 0.05 * nrm(ks[7], (L, DA_V_DIM), f32),
        'w_alpha': nrm(ks[8], (L, GLA_GATE_RANK, B_QK), f32) * GLA_GATE_RANK ** -0.5,
        'b_alpha': 0.1 * nrm(ks[9], (L, B_QK), f32),
        'g_sub_b': 1.0 + 0.05 * nrm(ks[10], (L, GLA_DV), f32),
        'w_up_a': nrm(ks[11], (L, A_V, D_MODEL), f32) * A_V ** -0.5,
        'w_up_b': nrm(ks[12], (L, B_V, D_MODEL), f32) * B_V ** -0.5,
        'w_out': nrm(ks[13], (L, D_MODEL, D_MODEL), f32) * D_MODEL ** -0.5,
        'g_post': 1.0 + 0.05 * nrm(ks[14], (L, D_MODEL), f32),
    }


def reference(x, g_pre, w_in, lam_q1, lam_k1, lam_q2, lam_k2, g_sub_a, w_alpha, b_alpha,
              g_sub_b, w_up_a, w_up_b, w_out, g_post):
    for i in range(DEPTH):
        x = hybrid_layer(x, g_pre[i], w_in[i], lam_q1[i], lam_k1[i], lam_q2[i], lam_k2[i],
                         g_sub_a[i], w_alpha[i], b_alpha[i], g_sub_b[i], w_up_a[i], w_up_b[i],
                         w_out[i], g_post[i], i)
    return x
```

```python
import functools
import math

import numpy as np
import jax
import jax.numpy as jnp
from jax import lax
from jax.experimental import pallas as pl
from jax.experimental.pallas import tpu as pltpu

F32 = jnp.float32
BF16 = jnp.bfloat16

DA_HEADS = 4
DA_HEAD_DIM = 64
DA_V_DIM = 128
GLA_HEADS = 4
GLA_DK = 64
GLA_DV = 128
GLA_GATE_RANK = 16
GLA_TAU = 16.0
RMS_EPS = 1e-6
LAYER_IDX = 0
LAM_INIT = 0.8 - 0.6 * math.exp(-0.3 * LAYER_IDX)

LANE = 128
VMEM_LIMIT = 56 * 1024 * 1024

NT_DIMS = (((1,), (1,)), ((), ()))
TN_DIMS = (((0,), (0,)), ((), ()))


def _rms(x, g):
    return x * lax.rsqrt(jnp.mean(x * x, axis=-1, keepdims=True) + RMS_EPS) * g


def _silu(z):
    return z * jax.nn.sigmoid(z)


def _inproj_kernel(x_ref, gpre_ref, w_ref, walpha_ref, balpha_ref,
                   qa_ref, ka_ref, va_ref, za_ref, qb_ref, kb_ref, vb_ref, zb_ref,
                   ga_ref, gb_ref, glog_ref, h_scr, *, segments, q_scale_a, q_scale_b):
    xf = x_ref[0]
    h_scr[...] = _rms(xf, gpre_ref[...]).astype(BF16)
    outs = dict(qa=qa_ref, ka=ka_ref, va=va_ref, za=za_ref, qb=qb_ref, kb=kb_ref, vb=vb_ref,
                zb=zb_ref, ga=ga_ref, gb=gb_ref)
    scales = dict(qa=q_scale_a, qb=q_scale_b)
    for name, c0, width, o0 in segments:
        acc = jnp.dot(h_scr[...], w_ref[:, c0:c0 + width], preferred_element_type=F32)
        if name == "lr":
            z = jnp.dot(acc.astype(BF16), walpha_ref[...], preferred_element_type=F32) + balpha_ref[...]
            log_sig = jnp.minimum(z, 0.0) - jnp.log1p(jnp.exp(-jnp.abs(z)))
            glog_ref[0] = log_sig / GLA_TAU
        else:
            if name in scales:
                acc = acc * scales[name]
            outs[name][0, :, o0:o0 + width] = acc.astype(BF16)


def _input_projection(x, g_pre, w_perm, w_alpha_pad, b_alpha, *, tm):
    B, S, D = x.shape
    a_qk, a_v = DA_HEADS * 2 * DA_HEAD_DIM, DA_HEADS * DA_V_DIM
    b_qk, b_v = GLA_HEADS * GLA_DK, GLA_HEADS * GLA_DV
    widths = [("qa", a_qk), ("ka", a_qk), ("va", a_v), ("za", a_v), ("qb", b_qk), ("kb", b_qk),
              ("vb", b_v), ("zb", b_v), ("ga", D), ("gb", D), ("lr", LANE)]
    max_chunk = 512
    segments = []
    c0 = 0
    for name, width in widths:
        for o0 in range(0, width, max_chunk):
            wdt = min(max_chunk, width - o0)
            segments.append((name, c0 + o0, wdt, o0))
        c0 += width
    assert c0 == w_perm.shape[1]

    def row_spec(width):
        return pl.BlockSpec((1, tm, width), lambda b, i: (b, i, 0))

    def const_spec(shape):
        return pl.BlockSpec(shape, lambda b, i: (0,) * len(shape))

    out_widths = [w for n, w in widths if n != "lr"]
    out_shape = [jax.ShapeDtypeStruct((B, S, w), BF16) for w in out_widths]
    out_shape.append(jax.ShapeDtypeStruct((B, S, b_qk), F32))
    out_specs = [row_spec(w) for w in out_widths] + [row_spec(b_qk)]
    kern = functools.partial(_inproj_kernel, segments=tuple(segments),
                             q_scale_a=DA_HEAD_DIM ** -0.5, q_scale_b=GLA_DK ** -0.5)
    return pl.pallas_call(
        kern,
        grid=(B, S // tm),
        in_specs=[row_spec(D), const_spec((1, D)), const_spec(w_perm.shape),
                  const_spec(w_alpha_pad.shape), const_spec((1, b_qk))],
        out_specs=out_specs,
        out_shape=out_shape,
        scratch_shapes=[pltpu.VMEM((tm, D), BF16)],
        compiler_params=pltpu.CompilerParams(
            dimension_semantics=("parallel", "parallel"), vmem_limit_bytes=VMEM_LIMIT),
        name="inproj",
    )(x, g_pre.reshape(1, D), w_perm, w_alpha_pad, b_alpha.reshape(1, b_qk))


def _attn_kernel(slopes_ref, lq1_ref, lk1_ref, lq2_ref, lk2_ref, q_ref, k_ref, v_ref, za_ref, gsub_ref,
                 o_ref, qs_scr, m_scr, l_scr, acc_scr, *, tq, tk):
    h = pl.program_id(1)
    i = pl.program_id(2)
    slope = slopes_ref[h]

    q = q_ref[0]
    lane = lax.broadcasted_iota(jnp.int32, q.shape, 1)
    zero = jnp.zeros_like(q)
    qs_scr[0:tq, :] = jnp.where(lane < DA_HEAD_DIM, q, zero)
    qs_scr[tq:2 * tq, :] = jnp.where(lane >= DA_HEAD_DIM, q, zero)
    m_scr[...] = jnp.full(m_scr.shape, -jnp.inf, F32)
    l_scr[...] = jnp.zeros(l_scr.shape, F32)
    acc_scr[...] = jnp.zeros(acc_scr.shape, F32)

    col = lax.broadcasted_iota(jnp.int32, (1, tk), 1)

    def step(j, masked):
        k0 = pl.multiple_of(j * tk, tk)
        kt = k_ref[0, pl.ds(k0, tk), :]
        vt = v_ref[0, pl.ds(k0, tk), :]
        s = lax.dot_general(qs_scr[...], kt, NT_DIMS, preferred_element_type=F32)
        colb = slope * (col + (j * tk - i * tq)).astype(F32)
        s = s + colb
        if masked:
            row = lax.broadcasted_iota(jnp.int32, (2 * tq, tk), 0)
            row = jnp.where(row >= tq, row - tq, row)
            cc = lax.broadcasted_iota(jnp.int32, (2 * tq, tk), 1)
            s = jnp.where(cc + j * tk <= row + i * tq, s, -1e30)
        m_prev = m_scr[...]
        m_next = jnp.maximum(m_prev, jnp.max(s, axis=1, keepdims=True))
        alpha = jnp.exp(m_prev - m_next)
        p = jnp.exp(s - m_next)
        l_scr[...] = alpha * l_scr[...] + jnp.sum(p, axis=1, keepdims=True)
        acc_scr[...] = alpha * acc_scr[...] + jnp.dot(p.astype(BF16), vt, preferred_element_type=F32)
        m_scr[...] = m_next

    n_full = (i * tq) // tk
    def body(j, carry):
        step(j, False)
        return carry
    lax.fori_loop(0, n_full, body, 0)
    for d in range(tq // tk):
        step(n_full + d, True)

    f = lambda r: r[...].astype(F32)
    lam = (jnp.exp(jnp.sum(f(lq1_ref) * f(lk1_ref), axis=1, keepdims=True))
           - jnp.exp(jnp.sum(f(lq2_ref) * f(lk2_ref), axis=1, keepdims=True)) + LAM_INIT)
    o1 = acc_scr[0:tq, :] / l_scr[0:tq, :]
    o2 = acc_scr[tq:2 * tq, :] / l_scr[tq:2 * tq, :]
    o = o1 - lam * o2
    o = _rms(o, gsub_ref[...]) * (1.0 - LAM_INIT)
    o_ref[0] = (o * _silu(za_ref[0].astype(F32))).astype(o_ref.dtype)


def _diff_attention(qa, ka, va, za, lam_q1, lam_k1, lam_q2, lam_k2, g_sub_a, *, tq, tk):
    B, S, _ = qa.shape
    assert tq % tk == 0
    slopes = jnp.exp2(-8.0 * jnp.arange(1, DA_HEADS + 1, dtype=F32) / DA_HEADS)
    vec = lambda v: v.reshape(1, DA_HEAD_DIM)
    vec_spec = pl.BlockSpec((1, DA_HEAD_DIM), lambda b, h, i: (0, 0))
    kern = functools.partial(_attn_kernel, tq=tq, tk=tk)
    return pl.pallas_call(
        kern,
        grid=(B, DA_HEADS, S // tq),
        in_specs=[pl.BlockSpec(memory_space=pltpu.SMEM),
                  vec_spec, vec_spec, vec_spec, vec_spec,
                  pl.BlockSpec((1, tq, LANE), lambda b, h, i: (b, i, h)),
                  pl.BlockSpec((1, S, LANE), lambda b, h, i: (b, 0, h)),
                  pl.BlockSpec((1, S, LANE), lambda b, h, i: (b, 0, h)),
                  pl.BlockSpec((1, tq, LANE), lambda b, h, i: (b, i, h)),
                  pl.BlockSpec((1, DA_V_DIM), lambda b, h, i: (0, 0))],
        out_specs=pl.BlockSpec((1, tq, LANE), lambda b, h, i: (b, i, h)),
        out_shape=jax.ShapeDtypeStruct((B, S, DA_HEADS * DA_V_DIM), BF16),
        scratch_shapes=[pltpu.VMEM((2 * tq, LANE), BF16),
                        pltpu.VMEM((2 * tq, 1), F32),
                        pltpu.VMEM((2 * tq, 1), F32),
                        pltpu.VMEM((2 * tq, DA_V_DIM), F32)],
        compiler_params=pltpu.CompilerParams(
            dimension_semantics=("parallel", "parallel", "arbitrary"), vmem_limit_bytes=VMEM_LIMIT),
        name="diffattn",
    )(slopes, vec(lam_q1), vec(lam_k1), vec(lam_q2), vec(lam_k2), qa, ka, va, za,
      g_sub_a.reshape(1, DA_V_DIM))


def _gla_tables(C, n_heads, dk):
    t = np.arange(C)[:, None]
    u = np.arange(C)[None, :]
    sums = [(u <= t), (u > t)]
    halves = []
    m = C // 2
    while m >= 1:
        halves.append(m)
        m //= 2
    upper_rows, pair_masks = [], []
    for m in halves:
        blk = 2 * m
        mid = (t // blk) * blk + m
        upper = (t % blk) >= m
        sums.append(np.where(upper, (u >= mid) & (u <= t), (u > t) & (u <= mid - 1)))
        upper_rows.append(np.broadcast_to(upper, (C, n_heads * dk)))
        tt, ss = np.arange(C)[:, None], np.arange(C)[None, :]
        pair_masks.append(((tt // blk) == (ss // blk)) & ((tt % blk) >= m) & ((ss % blk) < m))
    pair_masks.append(np.eye(C, dtype=bool))
    sum_sel = np.concatenate(sums, axis=0).astype(np.float32)
    upper_rows = np.stack(upper_rows).astype(np.float32)
    eye_h = np.eye(n_heads, dtype=bool)
    pair_bd = np.stack([np.kron(eye_h, pm) for pm in pair_masks]).astype(np.float32)
    head_of_lane = np.arange(n_heads * dk)[None, :] // dk
    head_of_row = np.repeat(np.arange(n_heads), C)[:, None]
    head_mask = (head_of_lane == head_of_row).astype(np.float32)
    return sum_sel, upper_rows, pair_bd, head_mask


def _gla_kernel(q_ref, k_ref, g_ref, v_ref, z_ref, gsub_ref, sumsel_ref, upper_ref, pair_ref, hmask_ref,
                o_ref, state_scr, *, chunk, n_chunks, n_levels):
    C = chunk
    H, DV = GLA_HEADS, GLA_DV

    @pl.when(pl.program_id(1) == 0)
    def _():
        state_scr[...] = jnp.zeros(state_scr.shape, F32)

    hmask = hmask_ref[...]
    ones_cols = jnp.ones((C, LANE), BF16)

    def stack_heads(a):
        return (jnp.concatenate([a] * H, axis=0) * hmask).astype(BF16)

    def chunk_body(c, carry):
        r0 = pl.multiple_of(c * C, C)
        q = q_ref[0, pl.ds(r0, C), :].astype(F32)
        k = k_ref[0, pl.ds(r0, C), :].astype(F32)
        g = g_ref[0, pl.ds(r0, C), :]
        v = v_ref[0, pl.ds(r0, C), :]
        v_st = jnp.concatenate([v[:, hh * DV:(hh + 1) * DV] for hh in range(H)], axis=0)

        g_hi = g.astype(BF16)
        g_lo = (g - g_hi.astype(F32)).astype(BF16)
        sel = sumsel_ref[...]
        d_all = (jnp.dot(sel, g_hi, preferred_element_type=F32)
                 + jnp.dot(sel, g_lo, preferred_element_type=F32))
        e_all = jnp.exp(d_all)
        col_sum = (lax.dot_general(g_hi, ones_cols, TN_DIMS, preferred_element_type=F32)
                   + lax.dot_general(g_lo, ones_cols, TN_DIMS, preferred_element_type=F32))

        state = state_scr[...]
        q_in = stack_heads(q * e_all[0:C])
        o_st = jnp.dot(q_in, state.astype(BF16), preferred_element_type=F32)

        k_out = stack_heads(k * e_all[C:2 * C])
        state_scr[...] = (jnp.exp(col_sum) * state
                          + lax.dot_general(k_out, v_st, TN_DIMS, preferred_element_type=F32))

        a_bd = None
        for lvl in range(n_levels + 1):
            if lvl < n_levels:
                e = e_all[(2 + lvl) * C:(3 + lvl) * C]
                xs = stack_heads(jnp.where(upper_ref[lvl] > 0.5, q, k) * e)
                prod = lax.dot_general(xs, xs, NT_DIMS, preferred_element_type=F32)
            else:
                prod = lax.dot_general(stack_heads(q), stack_heads(k), NT_DIMS, preferred_element_type=F32)
            term = prod * pair_ref[lvl]
            a_bd = term if a_bd is None else a_bd + term
        o_st = o_st + jnp.dot(a_bd.astype(BF16), v_st, preferred_element_type=F32)

        for hh in range(H):
            o_h = _rms(o_st[hh * C:(hh + 1) * C], gsub_ref[...])
            z_h = z_ref[0, pl.ds(r0, C), hh * DV:(hh + 1) * DV].astype(F32)
            o_ref[0, pl.ds(r0, C), hh * DV:(hh + 1) * DV] = (o_h * _silu(z_h)).astype(o_ref.dtype)
        return carry

    lax.fori_loop(0, n_chunks, chunk_body, 0)


def _gated_linear_attention(qb, kb, glog, vb, zb, g_sub_b, *, chunk, rows_per_step):
    B, S, b_qk = qb.shape
    b_v = vb.shape[-1]
    sum_sel, upper_rows, pair_bd, head_mask = _gla_tables(chunk, GLA_HEADS, GLA_DK)
    n_levels = upper_rows.shape[0]
    T = rows_per_step

    def row_spec(width):
        return pl.BlockSpec((1, T, width), lambda b, i: (b, i, 0))

    def const_spec(shape):
        return pl.BlockSpec(shape, lambda b, i: (0,) * len(shape))

    kern = functools.partial(_gla_kernel, chunk=chunk, n_chunks=T // chunk, n_levels=n_levels)
    return pl.pallas_call(
        kern,
        grid=(B, S // T),
        in_specs=[row_spec(b_qk), row_spec(b_qk), row_spec(b_qk), row_spec(b_v), row_spec(b_v),
                  const_spec((1, GLA_DV)), const_spec(sum_sel.shape), const_spec(upper_rows.shape),
                  const_spec(pair_bd.shape), const_spec(head_mask.shape)],
        out_specs=row_spec(b_v),
        out_shape=jax.ShapeDtypeStruct((B, S, b_v), BF16),
        scratch_shapes=[pltpu.VMEM((b_qk, GLA_DV), F32)],
        compiler_params=pltpu.CompilerParams(
            dimension_semantics=("parallel", "arbitrary"), vmem_limit_bytes=VMEM_LIMIT),
        name="gla",
    )(qb, kb, glog, vb, zb, g_sub_b.reshape(1, GLA_DV), jnp.asarray(sum_sel, BF16),
      jnp.asarray(upper_rows), jnp.asarray(pair_bd), jnp.asarray(head_mask))


def _merge_kernel(x_ref, ua_ref, ub_ref, ga_ref, gb_ref, wua_ref, wub_ref, wout_ref, gpost_ref, o_ref):
    ya = jnp.dot(ua_ref[0], wua_ref[...], preferred_element_type=F32)
    yb = jnp.dot(ub_ref[0], wub_ref[...], preferred_element_type=F32)
    y = jax.nn.sigmoid(ga_ref[0].astype(F32)) * ya + jax.nn.sigmoid(gb_ref[0].astype(F32)) * yb
    out = jnp.dot(y.astype(BF16), wout_ref[...], preferred_element_type=F32)
    o_ref[0] = x_ref[0] + _rms(out, gpost_ref[...])


def _merge(x, ua, ub, ga, gb, w_up_a, w_up_b, w_out, g_post, *, tm):
    B, S, D = x.shape

    def row_spec(width):
        return pl.BlockSpec((1, tm, width), lambda b, i: (b, i, 0))

    def const_spec(shape):
        return pl.BlockSpec(shape, lambda b, i: (0,) * len(shape))

    return pl.pallas_call(
        _merge_kernel,
        grid=(B, S // tm),
        in_specs=[row_spec(D), row_spec(ua.shape[-1]), row_spec(ub.shape[-1]), row_spec(D), row_spec(D),
                  const_spec(w_up_a.shape), const_spec(w_up_b.shape), const_spec(w_out.shape),
                  const_spec((1, D))],
        out_specs=row_spec(D),
        out_shape=jax.ShapeDtypeStruct((B, S, D), x.dtype),
        compiler_params=pltpu.CompilerParams(
            dimension_semantics=("parallel", "parallel"), vmem_limit_bytes=VMEM_LIMIT),
        name="merge",
    )(x, ua, ub, ga, gb, w_up_a, w_up_b, w_out, g_post.reshape(1, D))


def _layer(x, g_pre, w_in, lam_q1, lam_k1, lam_q2, lam_k2, g_sub_a, w_alpha, b_alpha, g_sub_b,
           w_up_a, w_up_b, w_out, g_post):
    D = x.shape[-1]
    lr0 = 2 * (DA_HEADS * 2 * DA_HEAD_DIM) + 2 * (DA_HEADS * DA_V_DIM) + 2 * (GLA_HEADS * GLA_DK) \
        + 2 * (GLA_HEADS * GLA_DV)
    lr1 = lr0 + GLA_GATE_RANK
    w_perm = jnp.concatenate(
        [w_in[:, :lr0], w_in[:, lr1:], w_in[:, lr0:lr1], jnp.zeros((D, LANE - GLA_GATE_RANK), w_in.dtype)],
        axis=1).astype(BF16)
    w_alpha_pad = jnp.concatenate(
        [w_alpha, jnp.zeros((LANE - GLA_GATE_RANK, w_alpha.shape[1]), w_alpha.dtype)], axis=0).astype(BF16)

    qa, ka, va, za, qb, kb, vb, zb, ga, gb, glog = _input_projection(
        x, g_pre, w_perm, w_alpha_pad, b_alpha, tm=512)
    ua = _diff_attention(qa, ka, va, za, lam_q1, lam_k1, lam_q2, lam_k2, g_sub_a, tq=256, tk=256)
    ub = _gated_linear_attention(qb, kb, glog, vb, zb, g_sub_b, chunk=64, rows_per_step=512)
    return _merge(x, ua, ub, ga, gb, w_up_a.astype(BF16), w_up_b.astype(BF16), w_out.astype(BF16),
                  g_post, tm=512)


def kernel(x, g_pre, w_in, lam_q1, lam_k1, lam_q2, lam_k2, g_sub_a, w_alpha, b_alpha, g_sub_b, w_up_a, w_up_b, w_out, g_post):
    depth = w_in.shape[0]
    assert depth == 1, "LAM_INIT is specialised to a single layer"
    for i in range(depth):
        x = _layer(x, g_pre[i], w_in[i], lam_q1[i], lam_k1[i], lam_q2[i], lam_k2[i], g_sub_a[i],
                   w_alpha[i], b_alpha[i], g_sub_b[i], w_up_a[i], w_up_b[i], w_out[i], g_post[i])
    return x
```

```python
import functools
import math

import numpy as np
import jax
import jax.numpy as jnp
from jax import lax
from jax.experimental import pallas as pl
from jax.experimental.pallas import tpu as pltpu

F32 = jnp.float32
BF16 = jnp.bfloat16

DA_HEADS = 4
DA_HEAD_DIM = 64
DA_V_DIM = 128
GLA_HEADS = 4
GLA_DK = 64
GLA_DV = 128
GLA_GATE_RANK = 16
GLA_TAU = 16.0
RMS_EPS = 1e-6
LAYER_IDX = 0
LAM_INIT = 0.8 - 0.6 * math.exp(-0.3 * LAYER_IDX)

LANE = 128
VMEM_LIMIT = 56 * 1024 * 1024

NT_DIMS = (((1,), (1,)), ((), ()))
TN_DIMS = (((0,), (0,)), ((), ()))


def _rms(x, g):
    return x * lax.rsqrt(jnp.mean(x * x, axis=-1, keepdims=True) + RMS_EPS) * g


def _silu(z):
    return z * jax.nn.sigmoid(z)


def _inproj_kernel(x_ref, gpre_ref, w_ref, walpha_ref, balpha_ref,
                   qa_ref, ka_ref, va_ref, za_ref, qb_ref, kb_ref, vb_ref, zb_ref,
                   ga_ref, gb_ref, glog_ref, h_scr, *, segments, q_scale_a, q_scale_b):
    xf = x_ref[0]
    h_scr[...] = _rms(xf, gpre_ref[...]).astype(BF16)
    outs = dict(qa=qa_ref, ka=ka_ref, va=va_ref, za=za_ref, qb=qb_ref, kb=kb_ref, vb=vb_ref,
                zb=zb_ref, ga=ga_ref, gb=gb_ref)
    scales = dict(qa=q_scale_a, qb=q_scale_b)
    for name, c0, width, o0 in segments:
        acc = jnp.dot(h_scr[...], w_ref[:, c0:c0 + width], preferred_element_type=F32)
        if name == "lr":
            z = jnp.dot(acc.astype(BF16), walpha_ref[...], preferred_element_type=F32) + balpha_ref[...]
            log_sig = jnp.minimum(z, 0.0) - jnp.log1p(jnp.exp(-jnp.abs(z)))
            glog_ref[0] = log_sig / GLA_TAU
        else:
            if name in scales:
                acc = acc * scales[name]
            outs[name][0, :, o0:o0 + width] = acc.astype(BF16)


def _input_projection(x, g_pre, w_perm, w_alpha_pad, b_alpha, *, tm):
    B, S, D = x.shape
    a_qk, a_v = DA_HEADS * 2 * DA_HEAD_DIM, DA_HEADS * DA_V_DIM
    b_qk, b_v = GLA_HEADS * GLA_DK, GLA_HEADS * GLA_DV
    widths = [("qa", a_qk), ("ka", a_qk), ("va", a_v), ("za", a_v), ("qb", b_qk), ("kb", b_qk),
              ("vb", b_v), ("zb", b_v), ("ga", D), ("gb", D), ("lr", LANE)]
    max_chunk = 512
    segments = []
    c0 = 0
    for name, width in widths:
        for o0 in range(0, width, max_chunk):
            wdt = min(max_chunk, width - o0)
            segments.append((name, c0 + o0, wdt, o0))
        c0 += width
    assert c0 == w_perm.shape[1]

    def row_spec(width):
        return pl.BlockSpec((1, tm, width), lambda b, i: (b, i, 0))

    def const_spec(shape):
        return pl.BlockSpec(shape, lambda b, i: (0,) * len(shape))

    out_widths = [w for n, w in widths if n != "lr"]
    out_shape = [jax.ShapeDtypeStruct((B, S, w), BF16) for w in out_widths]
    out_shape.append(jax.ShapeDtypeStruct((B, S, b_qk), F32))
    out_specs = [row_spec(w) for w in out_widths] + [row_spec(b_qk)]
    kern = functools.partial(_inproj_kernel, segments=tuple(segments),
                             q_scale_a=DA_HEAD_DIM ** -0.5, q_scale_b=GLA_DK ** -0.5)
    return pl.pallas_call(
        kern,
        grid=(B, S // tm),
        in_specs=[row_spec(D), const_spec((1, D)), const_spec(w_perm.shape),
                  const_spec(w_alpha_pad.shape), const_spec((1, b_qk))],
        out_specs=out_specs,
        out_shape=out_shape,
        scratch_shapes=[pltpu.VMEM((tm, D), BF16)],
        compiler_params=pltpu.CompilerParams(
            dimension_semantics=("parallel", "parallel"), vmem_limit_bytes=VMEM_LIMIT),
        name="inproj",
    )(x, g_pre.reshape(1, D), w_perm, w_alpha_pad, b_alpha.reshape(1, b_qk))


def _attn_kernel(lq1_ref, lk1_ref, lq2_ref, lk2_ref, q_ref, k_ref, v_ref, za_ref, gsub_ref,
                 o_ref, qs_scr, vt_scr, bias_scr, st_scr, m_scr, l_scr, acc_scr, *, tq, tk):
    i = pl.program_id(1)
    H, DV = DA_HEADS, DA_V_DIM
    slopes = [2.0 ** (-8.0 * (h + 1) / H) for h in range(H)]
    n_kv = vt_scr.shape[1]

    @pl.when(i == 0)
    def _():
        def xpose(j, carry):
            k0 = pl.multiple_of(j * tk, tk)
            for h in range(H):
                vt_scr[h, j] = v_ref[0, pl.ds(k0, tk), h * DV:(h + 1) * DV].astype(F32).T.astype(BF16)
            return carry
        lax.fori_loop(0, n_kv, xpose, 0)
        key = lax.broadcasted_iota(jnp.int32, bias_scr.shape[1:], 0).astype(F32)
        for h in range(H):
            bias_scr[h] = slopes[h] * key

    lane = lax.broadcasted_iota(jnp.int32, (tq, LANE), 1)
    zero = jnp.zeros((tq, LANE), BF16)
    for h in range(H):
        q = q_ref[0, :, h * LANE:(h + 1) * LANE]
        qs_scr[h, 0:tq, :] = jnp.where(lane < DA_HEAD_DIM, q, zero)
        qs_scr[h, tq:2 * tq, :] = jnp.where(lane >= DA_HEAD_DIM, q, zero)
    m_scr[...] = jnp.full(m_scr.shape, -jnp.inf, F32)
    l_scr[...] = jnp.zeros(l_scr.shape, F32)
    acc_scr[...] = jnp.zeros(acc_scr.shape, F32)

    def produce(j, buf):
        k0 = pl.multiple_of(j * tk, tk)
        for h in range(H):
            kt = k_ref[0, pl.ds(k0, tk), h * LANE:(h + 1) * LANE]
            st = lax.dot_general(kt, qs_scr[h], NT_DIMS, preferred_element_type=F32)
            st_scr[buf, h] = st + bias_scr[h]

    def consume(j, buf, masked):
        if masked:
            kpos = lax.broadcasted_iota(jnp.int32, (tk, 2 * tq), 0) + j * tk
            qcol = lax.broadcasted_iota(jnp.int32, (tk, 2 * tq), 1)
            qpos = jnp.where(qcol >= tq, qcol - tq, qcol) + i * tq
            visible = kpos <= qpos
        tile_dist = (j * tk - i * tq).astype(F32)
        for h in range(H):
            st = st_scr[buf, h]
            if masked:
                st = jnp.where(visible, st, -1e30)
            off = slopes[h] * tile_dist
            m_prev = m_scr[h]
            m_next = jnp.maximum(m_prev, jnp.max(st, axis=0, keepdims=True) + off)
            alpha = jnp.exp(m_prev - m_next)
            p = jnp.exp(st - (m_next - off))
            l_scr[h] = alpha * l_scr[h] + jnp.sum(p, axis=0, keepdims=True)
            acc_scr[h] = alpha * acc_scr[h] + jnp.dot(vt_scr[h, j], p.astype(BF16),
                                                      preferred_element_type=F32)
            m_scr[h] = m_next

    n_pairs = i // 2
    produce(0, 0)

    def pair(jj, carry):
        j = 2 * jj
        produce(j + 1, 1)
        consume(j, 0, False)
        produce(j + 2, 0)
        consume(j + 1, 1, False)
        return carry
    lax.fori_loop(0, n_pairs, pair, 0)

    @pl.when(i % 2 == 1)
    def _():
        produce(i, 1)
        consume(i - 1, 0, False)
        consume(i, 1, True)

    @pl.when(i % 2 == 0)
    def _():
        consume(i, 0, True)

    f = lambda r: r[...].astype(F32)
    lam = (jnp.exp(jnp.sum(f(lq1_ref) * f(lk1_ref), axis=1, keepdims=True))
           - jnp.exp(jnp.sum(f(lq2_ref) * f(lk2_ref), axis=1, keepdims=True)) + LAM_INIT)
    for h in range(H):
        on = acc_scr[h] / l_scr[h]
        o = (on[:, 0:tq] - lam * on[:, tq:2 * tq]).T
        o = _rms(o, gsub_ref[...]) * (1.0 - LAM_INIT)
        z = za_ref[0, :, h * DV:(h + 1) * DV].astype(F32)
        o_ref[0, :, h * DV:(h + 1) * DV] = (o * _silu(z)).astype(o_ref.dtype)


def _diff_attention(qa, ka, va, za, lam_q1, lam_k1, lam_q2, lam_k2, g_sub_a, *, tq):
    B, S, W = qa.shape
    tk = tq
    assert W == DA_HEADS * LANE and S % tq == 0
    vec = lambda v: v.reshape(1, DA_HEAD_DIM)
    vec_spec = pl.BlockSpec((1, DA_HEAD_DIM), lambda b, i: (0, 0))
    tile_spec = pl.BlockSpec((1, tq, W), lambda b, i: (b, i, 0))
    seq_spec = pl.BlockSpec((1, S, W), lambda b, i: (b, 0, 0))
    kern = functools.partial(_attn_kernel, tq=tq, tk=tk)
    return pl.pallas_call(
        kern,
        grid=(B, S // tq),
        in_specs=[vec_spec, vec_spec, vec_spec, vec_spec, tile_spec, seq_spec, seq_spec, tile_spec,
                  pl.BlockSpec((1, DA_V_DIM), lambda b, i: (0, 0))],
        out_specs=tile_spec,
        out_shape=jax.ShapeDtypeStruct((B, S, W), BF16),
        scratch_shapes=[pltpu.VMEM((DA_HEADS, 2 * tq, LANE), BF16),
                        pltpu.VMEM((DA_HEADS, S // tk, DA_V_DIM, tk), BF16),
                        pltpu.VMEM((DA_HEADS, tk, 2 * tq), F32),
                        pltpu.VMEM((2, DA_HEADS, tk, 2 * tq), F32),
                        pltpu.VMEM((DA_HEADS, 1, 2 * tq), F32),
                        pltpu.VMEM((DA_HEADS, 1, 2 * tq), F32),
                        pltpu.VMEM((DA_HEADS, DA_V_DIM, 2 * tq), F32)],
        compiler_params=pltpu.CompilerParams(
            dimension_semantics=("parallel", "arbitrary"), vmem_limit_bytes=VMEM_LIMIT),
        name="diffattn",
    )(vec(lam_q1), vec(lam_k1), vec(lam_q2), vec(lam_k2), qa, ka, va, za,
      g_sub_a.reshape(1, DA_V_DIM))


def _gla_tables(C, n_heads, dk):
    t = np.arange(C)[:, None]
    u = np.arange(C)[None, :]
    sums = [(u <= t), (u > t)]
    halves = []
    m = C // 2
    while m >= 1:
        halves.append(m)
        m //= 2
    upper_rows, pair_masks = [], []
    for m in halves:
        blk = 2 * m
        mid = (t // blk) * blk + m
        upper = (t % blk) >= m
        sums.append(np.where(upper, (u >= mid) & (u <= t), (u > t) & (u <= mid - 1)))
        upper_rows.append(np.broadcast_to(upper, (C, n_heads * dk)))
        tt, ss = np.arange(C)[:, None], np.arange(C)[None, :]
        pair_masks.append(((tt // blk) == (ss // blk)) & ((tt % blk) >= m) & ((ss % blk) < m))
    pair_masks.append(np.eye(C, dtype=bool))
    sum_sel = np.concatenate(sums, axis=0).astype(np.float32)
    upper_rows = np.stack(upper_rows).astype(np.float32)
    eye_h = np.eye(n_heads, dtype=bool)
    pair_bd = np.stack([np.kron(eye_h, pm) for pm in pair_masks]).astype(np.float32)
    head_of_lane = np.arange(n_heads * dk)[None, :] // dk
    head_of_row = np.repeat(np.arange(n_heads), C)[:, None]
    head_mask = (head_of_lane == head_of_row).astype(np.float32)
    return sum_sel, upper_rows, pair_bd, head_mask


def _gla_kernel(q_ref, k_ref, g_ref, v_ref, z_ref, gsub_ref, sumsel_ref, upper_ref, pair_ref, hmask_ref,
                o_ref, state_scr, *, chunk, n_chunks, n_levels):
    C = chunk
    H, DV = GLA_HEADS, GLA_DV

    @pl.when(pl.program_id(1) == 0)
    def _():
        state_scr[...] = jnp.zeros(state_scr.shape, F32)

    hmask = hmask_ref[...]
    ones_cols = jnp.ones((C, LANE), BF16)

    def stack_heads(a):
        return (jnp.concatenate([a] * H, axis=0) * hmask).astype(BF16)

    def chunk_body(c, carry):
        r0 = pl.multiple_of(c * C, C)
        q = q_ref[0, pl.ds(r0, C), :].astype(F32)
        k = k_ref[0, pl.ds(r0, C), :].astype(F32)
        g = g_ref[0, pl.ds(r0, C), :]
        v = v_ref[0, pl.ds(r0, C), :]
        v_st = jnp.concatenate([v[:, hh * DV:(hh + 1) * DV] for hh in range(H)], axis=0)

        g_hi = g.astype(BF16)
        g_lo = (g - g_hi.astype(F32)).astype(BF16)
        sel = sumsel_ref[...]
        d_all = (jnp.dot(sel, g_hi, preferred_element_type=F32)
                 + jnp.dot(sel, g_lo, preferred_element_type=F32))
        e_all = jnp.exp(d_all)
        col_sum = (lax.dot_general(g_hi, ones_cols, TN_DIMS, preferred_element_type=F32)
                   + lax.dot_general(g_lo, ones_cols, TN_DIMS, preferred_element_type=F32))

        state = state_scr[...]
        q_in = stack_heads(q * e_all[0:C])
        o_st = jnp.dot(q_in, state.astype(BF16), preferred_element_type=F32)

        k_out = stack_heads(k * e_all[C:2 * C])
        state_scr[...] = (jnp.exp(col_sum) * state
                          + lax.dot_general(k_out, v_st, TN_DIMS, preferred_element_type=F32))

        a_bd = None
        for lvl in range(n_levels + 1):
            if lvl < n_levels:
                e = e_all[(2 + lvl) * C:(3 + lvl) * C]
                xs = stack_heads(jnp.where(upper_ref[lvl] > 0.5, q, k) * e)
                prod = lax.dot_general(xs, xs, NT_DIMS, preferred_element_type=F32)
            else:
                prod = lax.dot_general(stack_heads(q), stack_heads(k), NT_DIMS, preferred_element_type=F32)
            term = prod * pair_ref[lvl]
            a_bd = term if a_bd is None else a_bd + term
        o_st = o_st + jnp.dot(a_bd.astype(BF16), v_st, preferred_element_type=F32)

        for hh in range(H):
            o_h = _rms(o_st[hh * C:(hh + 1) * C], gsub_ref[...])
            z_h = z_ref[0, pl.ds(r0, C), hh * DV:(hh + 1) * DV].astype(F32)
            o_ref[0, pl.ds(r0, C), hh * DV:(hh + 1) * DV] = (o_h * _silu(z_h)).astype(o_ref.dtype)
        return carry

    lax.fori_loop(0, n_chunks, chunk_body, 0)


def _gated_linear_attention(qb, kb, glog, vb, zb, g_sub_b, *, chunk, rows_per_step):
    B, S, b_qk = qb.shape
    b_v = vb.shape[-1]
    sum_sel, upper_rows, pair_bd, head_mask = _gla_tables(chunk, GLA_HEADS, GLA_DK)
    n_levels = upper_rows.shape[0]
    T = rows_per_step

    def row_spec(width):
        return pl.BlockSpec((1, T, width), lambda b, i: (b, i, 0))

    def const_spec(shape):
        return pl.BlockSpec(shape, lambda b, i: (0,) * len(shape))

    kern = functools.partial(_gla_kernel, chunk=chunk, n_chunks=T // chunk, n_levels=n_levels)
    return pl.pallas_call(
        kern,
        grid=(B, S // T),
        in_specs=[row_spec(b_qk), row_spec(b_qk), row_spec(b_qk), row_spec(b_v), row_spec(b_v),
                  const_spec((1, GLA_DV)), const_spec(sum_sel.shape), const_spec(upper_rows.shape),
                  const_spec(pair_bd.shape), const_spec(head_mask.shape)],
        out_specs=row_spec(b_v),
        out_shape=jax.ShapeDtypeStruct((B, S, b_v), BF16),
        scratch_shapes=[pltpu.VMEM((b_qk, GLA_DV), F32)],
        compiler_params=pltpu.CompilerParams(
            dimension_semantics=("parallel", "arbitrary"), vmem_limit_bytes=VMEM_LIMIT),
        name="gla",
    )(qb, kb, glog, vb, zb, g_sub_b.reshape(1, GLA_DV), jnp.asarray(sum_sel, BF16),
      jnp.asarray(upper_rows), jnp.asarray(pair_bd), jnp.asarray(head_mask))


def _merge_kernel(x_ref, ua_ref, ub_ref, ga_ref, gb_ref, wua_ref, wub_ref, wout_ref, gpost_ref, o_ref):
    ya = jnp.dot(ua_ref[0], wua_ref[...], preferred_element_type=F32)
    yb = jnp.dot(ub_ref[0], wub_ref[...], preferred_element_type=F32)
    y = jax.nn.sigmoid(ga_ref[0].astype(F32)) * ya + jax.nn.sigmoid(gb_ref[0].astype(F32)) * yb
    out = jnp.dot(y.astype(BF16), wout_ref[...], preferred_element_type=F32)
    o_ref[0] = x_ref[0] + _rms(out, gpost_ref[...])


def _merge(x, ua, ub, ga, gb, w_up_a, w_up_b, w_out, g_post, *, tm):
    B, S, D = x.shape

    def row_spec(width):
        return pl.BlockSpec((1, tm, width), lambda b, i: (b, i, 0))

    def const_spec(shape):
        return pl.BlockSpec(shape, lambda b, i: (0,) * len(shape))

    return pl.pallas_call(
        _merge_kernel,
        grid=(B, S // tm),
        in_specs=[row_spec(D), row_spec(ua.shape[-1]), row_spec(ub.shape[-1]), row_spec(D), row_spec(D),
                  const_spec(w_up_a.shape), const_spec(w_up_b.shape), const_spec(w_out.shape),
                  const_spec((1, D))],
        out_specs=row_spec(D),
        out_shape=jax.ShapeDtypeStruct((B, S, D), x.dtype),
        compiler_params=pltpu.CompilerParams(
            dimension_semantics=("parallel", "parallel"), vmem_limit_bytes=VMEM_LIMIT),
        name="merge",
    )(x, ua, ub, ga, gb, w_up_a, w_up_b, w_out, g_post.reshape(1, D))


def _layer(x, g_pre, w_in, lam_q1, lam_k1, lam_q2, lam_k2, g_sub_a, w_alpha, b_alpha, g_sub_b,
           w_up_a, w_up_b, w_out, g_post):
    D = x.shape[-1]
    lr0 = 2 * (DA_HEADS * 2 * DA_HEAD_DIM) + 2 * (DA_HEADS * DA_V_DIM) + 2 * (GLA_HEADS * GLA_DK) \
        + 2 * (GLA_HEADS * GLA_DV)
    lr1 = lr0 + GLA_GATE_RANK
    w_perm = jnp.concatenate(
        [w_in[:, :lr0], w_in[:, lr1:], w_in[:, lr0:lr1], jnp.zeros((D, LANE - GLA_GATE_RANK), w_in.dtype)],
        axis=1).astype(BF16)
    w_alpha_pad = jnp.concatenate(
        [w_alpha, jnp.zeros((LANE - GLA_GATE_RANK, w_alpha.shape[1]), w_alpha.dtype)], axis=0).astype(BF16)

    qa, ka, va, za, qb, kb, vb, zb, ga, gb, glog = _input_projection(
        x, g_pre, w_perm, w_alpha_pad, b_alpha, tm=512)
    ua = _diff_attention(qa, ka, va, za, lam_q1, lam_k1, lam_q2, lam_k2, g_sub_a, tq=256)
    ub = _gated_linear_attention(qb, kb, glog, vb, zb, g_sub_b, chunk=64, rows_per_step=512)
    return _merge(x, ua, ub, ga, gb, w_up_a.astype(BF16), w_up_b.astype(BF16), w_out.astype(BF16),
                  g_post, tm=512)


def kernel(x, g_pre, w_in, lam_q1, lam_k1, lam_q2, lam_k2, g_sub_a, w_alpha, b_alpha, g_sub_b, w_up_a, w_up_b, w_out, g_post):
    depth = w_in.shape[0]
    assert depth == 1, "LAM_INIT is specialised to a single layer"
    for i in range(depth):
        x = _layer(x, g_pre[i], w_in[i], lam_q1[i], lam_k1[i], lam_q2[i], lam_k2[i], g_sub_a[i],
                   w_alpha[i], b_alpha[i], g_sub_b[i], w_up_a[i], w_up_b[i], w_out[i], g_post[i])
    return x
```

```python
import functools
import math

import numpy as np
import jax
import jax.numpy as jnp
from jax import lax
from jax.experimental import pallas as pl
from jax.experimental.pallas import tpu as pltpu

F32 = jnp.float32
BF16 = jnp.bfloat16

DA_HEADS = 4
DA_HEAD_DIM = 64
DA_V_DIM = 128
GLA_HEADS = 4
GLA_DK = 64
GLA_DV = 128
GLA_GATE_RANK = 16
GLA_TAU = 16.0
RMS_EPS = 1e-6
LAYER_IDX = 0
LAM_INIT = 0.8 - 0.6 * math.exp(-0.3 * LAYER_IDX)

LOG2E = math.log2(math.e)
LANE = 128
ONES_ROWS = 16
VMEM_LIMIT = 56 * 1024 * 1024

NT_DIMS = (((1,), (1,)), ((), ()))
TN_DIMS = (((0,), (0,)), ((), ()))


def _rms(x, g):
    return x * lax.rsqrt(jnp.mean(x * x, axis=-1, keepdims=True) + RMS_EPS) * g


def _silu(z):
    return z * jax.nn.sigmoid(z)


def _inproj_kernel(x_ref, gpre_ref, wmix_ref, wlr_ref, wgate_ref, walpha_ref, balpha_ref,
                   qa_ref, ka_ref, va_ref, za_ref, qb_ref, kb_ref, vb_ref, zb_ref,
                   ga_ref, gb_ref, glog_ref, h_scr, *, segments, q_scale_a, q_scale_b):
    xf = x_ref[0]
    h_scr[...] = _rms(xf, gpre_ref[...]).astype(BF16)
    outs = dict(qa=qa_ref, ka=ka_ref, va=va_ref, za=za_ref, qb=qb_ref, kb=kb_ref, vb=vb_ref,
                zb=zb_ref, ga=ga_ref, gb=gb_ref)
    weights = dict(mix=wmix_ref, gate=wgate_ref)
    scales = dict(qa=q_scale_a, qb=q_scale_b)
    for name, wname, c0, width, o0 in segments:
        acc = jnp.dot(h_scr[...], weights[wname][:, c0:c0 + width], preferred_element_type=F32)
        if name in scales:
            acc = acc * scales[name]
        outs[name][0, :, o0:o0 + width] = acc.astype(BF16)
    lr = jnp.dot(h_scr[...], wlr_ref[...], preferred_element_type=F32)
    z = jnp.dot(lr.astype(BF16), walpha_ref[...], preferred_element_type=F32) + balpha_ref[...]
    log_sig = jnp.minimum(z, 0.0) - jnp.log1p(jnp.exp(-jnp.abs(z)))
    glog_ref[0] = log_sig / GLA_TAU


def _input_projection(x, g_pre, w_mix, w_lr, w_gate, w_alpha, b_alpha, *, tm):
    B, S, D = x.shape
    a_qk, a_v = DA_HEADS * 2 * DA_HEAD_DIM, DA_HEADS * DA_V_DIM
    b_qk, b_v = GLA_HEADS * GLA_DK, GLA_HEADS * GLA_DV
    widths = [("qa", "mix", a_qk), ("ka", "mix", a_qk), ("va", "mix", a_v), ("za", "mix", a_v),
              ("qb", "mix", b_qk), ("kb", "mix", b_qk), ("vb", "mix", b_v), ("zb", "mix", b_v),
              ("ga", "gate", D), ("gb", "gate", D)]
    max_chunk = 512
    segments = []
    col = dict(mix=0, gate=0)
    for name, wname, width in widths:
        for o0 in range(0, width, max_chunk):
            segments.append((name, wname, col[wname] + o0, min(max_chunk, width - o0), o0))
        col[wname] += width
    assert col["mix"] == w_mix.shape[1] and col["gate"] == w_gate.shape[1]

    def row_spec(width):
        return pl.BlockSpec((1, tm, width), lambda b, i: (b, i, 0))

    def const_spec(shape):
        return pl.BlockSpec(shape, lambda b, i: (0,) * len(shape))

    out_widths = [w for _, _, w in widths]
    out_shape = [jax.ShapeDtypeStruct((B, S, w), BF16) for w in out_widths]
    out_shape.append(jax.ShapeDtypeStruct((B, S, b_qk), F32))
    out_specs = [row_spec(w) for w in out_widths] + [row_spec(b_qk)]
    kern = functools.partial(_inproj_kernel, segments=tuple(segments),
                             q_scale_a=DA_HEAD_DIM ** -0.5 * LOG2E,
                             q_scale_b=GLA_DK ** -0.5)
    return pl.pallas_call(
        kern,
        grid=(B, S // tm),
        in_specs=[row_spec(D), const_spec((1, D)), const_spec(w_mix.shape), const_spec(w_lr.shape),
                  const_spec(w_gate.shape), const_spec(w_alpha.shape), const_spec((1, b_qk))],
        out_specs=out_specs,
        out_shape=out_shape,
        scratch_shapes=[pltpu.VMEM((tm, D), BF16)],
        compiler_params=pltpu.CompilerParams(
            dimension_semantics=("parallel", "parallel"), vmem_limit_bytes=VMEM_LIMIT),
        name="inproj",
    )(x, g_pre.reshape(1, D), w_mix, w_lr, w_gate, w_alpha, b_alpha.reshape(1, b_qk))


def _attn_kernel(lq1_ref, lk1_ref, lq2_ref, lk2_ref, q_ref, k_ref, v_ref, za_ref, gsub_ref,
                 o_ref, qs_scr, ks_scr, vt_scr, st_scr, m_scr, acc_scr, *, tq, tk):
    i = pl.program_id(1)
    H, DV = DA_HEADS, DA_V_DIM
    n_kv = vt_scr.shape[1]
    slope_hi, slope_lo, slopes = [], [], []
    for h in range(H):
        s2 = np.float32(2.0 ** (-8.0 * (h + 1) / H) * LOG2E)
        hi = np.asarray(s2, dtype=BF16)
        lo = np.asarray(s2 - np.float32(hi), dtype=BF16)
        slope_hi.append(float(hi))
        slope_lo.append(float(lo))
        slopes.append(float(hi) + float(lo))

    @pl.when(i == 0)
    def _():
        lane = lax.broadcasted_iota(jnp.int32, (tk, LANE), 1)
        key = lax.broadcasted_iota(jnp.int32, (tk, LANE), 0).astype(F32)
        key_lanes = jnp.where(lane < 2, key, 0.0).astype(BF16)
        ones_rows = jnp.ones((ONES_ROWS, tk), BF16)
        def prep(j, carry):
            k0 = pl.multiple_of(j * tk, tk)
            for h in range(H):
                ks_scr[h, pl.ds(k0, tk), 0:LANE] = k_ref[0, pl.ds(k0, tk), h * LANE:(h + 1) * LANE]
                ks_scr[h, pl.ds(k0, tk), LANE:2 * LANE] = key_lanes
                vt_scr[h, j, 0:DV, :] = v_ref[0, pl.ds(k0, tk), h * DV:(h + 1) * DV].astype(F32).T.astype(BF16)
                vt_scr[h, j, DV:DV + ONES_ROWS, :] = ones_rows
            return carry
        lax.fori_loop(0, n_kv, prep, 0)

    lane = lax.broadcasted_iota(jnp.int32, (tq, LANE), 1)
    zero = jnp.zeros((tq, LANE), BF16)
    for h in range(H):
        q = q_ref[0, :, h * LANE:(h + 1) * LANE]
        slope_lanes = jnp.where(lane == 0, slope_hi[h], jnp.where(lane == 1, slope_lo[h], 0.0)).astype(BF16)
        qs_scr[h, 0:tq, 0:LANE] = jnp.where(lane < DA_HEAD_DIM, q, zero)
        qs_scr[h, tq:2 * tq, 0:LANE] = jnp.where(lane >= DA_HEAD_DIM, q, zero)
        qs_scr[h, 0:tq, LANE:2 * LANE] = slope_lanes
        qs_scr[h, tq:2 * tq, LANE:2 * LANE] = slope_lanes
    m_scr[...] = jnp.full(m_scr.shape, -jnp.inf, F32)
    acc_scr[...] = jnp.zeros(acc_scr.shape, F32)

    def produce(j, buf):
        k0 = pl.multiple_of(j * tk, tk)
        for h in range(H):
            st_scr[buf, h] = lax.dot_general(ks_scr[h, pl.ds(k0, tk), :], qs_scr[h], NT_DIMS,
                                             preferred_element_type=F32)

    def consume(j, buf, masked):
        if masked:
            kpos = lax.broadcasted_iota(jnp.int32, (tk, 2 * tq), 0) + j * tk
            qcol = lax.broadcasted_iota(jnp.int32, (tk, 2 * tq), 1)
            qpos = jnp.where(qcol >= tq, qcol - tq, qcol) + i * tq
            visible = kpos <= qpos
        tile_dist = (j * tk - i * tq).astype(F32)
        for h in range(H):
            st = st_scr[buf, h]
            if masked:
                st = jnp.where(visible, st, -1e30)
            off = slopes[h] * tile_dist
            m_prev = m_scr[h]
            m_next = jnp.maximum(m_prev, jnp.max(st, axis=0, keepdims=True) + off)
            alpha = jnp.exp2(m_prev - m_next)
            p = jnp.exp2(st - (m_next - off))
            acc_scr[h] = alpha * acc_scr[h] + jnp.dot(vt_scr[h, j], p.astype(BF16),
                                                      preferred_element_type=F32)
            m_scr[h] = m_next

    n_pairs = i // 2
    produce(0, 0)

    def pair(jj, carry):
        j = 2 * jj
        produce(j + 1, 1)
        consume(j, 0, False)
        produce(j + 2, 0)
        consume(j + 1, 1, False)
        return carry
    lax.fori_loop(0, n_pairs, pair, 0)

    @pl.when(i % 2 == 1)
    def _():
        produce(i, 1)
        consume(i - 1, 0, False)
        consume(i, 1, True)

    @pl.when(i % 2 == 0)
    def _():
        consume(i, 0, True)

    f = lambda r: r[...].astype(F32)
    lam = (jnp.exp(jnp.sum(f(lq1_ref) * f(lk1_ref), axis=1, keepdims=True))
           - jnp.exp(jnp.sum(f(lq2_ref) * f(lk2_ref), axis=1, keepdims=True)) + LAM_INIT)
    for h in range(H):
        on = acc_scr[h, 0:DV, :] / acc_scr[h, DV:DV + 1, :]
        o = (on[:, 0:tq] - lam * on[:, tq:2 * tq]).T
        o = _rms(o, gsub_ref[...]) * (1.0 - LAM_INIT)
        z = za_ref[0, :, h * DV:(h + 1) * DV].astype(F32)
        o_ref[0, :, h * DV:(h + 1) * DV] = (o * _silu(z)).astype(o_ref.dtype)


def _diff_attention(qa, ka, va, za, lam_q1, lam_k1, lam_q2, lam_k2, g_sub_a, *, tq):
    B, S, W = qa.shape
    tk = tq
    assert W == DA_HEADS * LANE and S % tq == 0
    vec = lambda v: v.reshape(1, DA_HEAD_DIM)
    vec_spec = pl.BlockSpec((1, DA_HEAD_DIM), lambda b, i: (0, 0))
    tile_spec = pl.BlockSpec((1, tq, W), lambda b, i: (b, i, 0))
    seq_spec = pl.BlockSpec((1, S, W), lambda b, i: (b, 0, 0))
    kern = functools.partial(_attn_kernel, tq=tq, tk=tk)
    return pl.pallas_call(
        kern,
        grid=(B, S // tq),
        in_specs=[vec_spec, vec_spec, vec_spec, vec_spec, tile_spec, seq_spec, seq_spec, tile_spec,
                  pl.BlockSpec((1, DA_V_DIM), lambda b, i: (0, 0))],
        out_specs=tile_spec,
        out_shape=jax.ShapeDtypeStruct((B, S, W), BF16),
        scratch_shapes=[pltpu.VMEM((DA_HEADS, 2 * tq, 2 * LANE), BF16),
                        pltpu.VMEM((DA_HEADS, S, 2 * LANE), BF16),
                        pltpu.VMEM((DA_HEADS, S // tk, DA_V_DIM + ONES_ROWS, tk), BF16),
                        pltpu.VMEM((2, DA_HEADS, tk, 2 * tq), F32),
                        pltpu.VMEM((DA_HEADS, 1, 2 * tq), F32),
                        pltpu.VMEM((DA_HEADS, DA_V_DIM + ONES_ROWS, 2 * tq), F32)],
        compiler_params=pltpu.CompilerParams(
            dimension_semantics=("parallel", "arbitrary"), vmem_limit_bytes=VMEM_LIMIT),
        name="diffattn",
    )(vec(lam_q1), vec(lam_k1), vec(lam_q2), vec(lam_k2), qa, ka, va, za,
      g_sub_a.reshape(1, DA_V_DIM))


def _gla_tables(C, n_heads, dk):
    t = np.arange(C)[:, None]
    u = np.arange(C)[None, :]
    sums = [(u <= t), (u > t)]
    halves = []
    m = C // 2
    while m >= 1:
        halves.append(m)
        m //= 2
    upper_rows, pair_masks = [], []
    for m in halves:
        blk = 2 * m
        mid = (t // blk) * blk + m
        upper = (t % blk) >= m
        sums.append(np.where(upper, (u >= mid) & (u <= t), (u > t) & (u <= mid - 1)))
        upper_rows.append(np.broadcast_to(upper, (C, n_heads * dk)))
        tt, ss = np.arange(C)[:, None], np.arange(C)[None, :]
        pair_masks.append(((tt // blk) == (ss // blk)) & ((tt % blk) >= m) & ((ss % blk) < m))
    pair_masks.append(np.eye(C, dtype=bool))
    sum_sel = np.concatenate(sums, axis=0).astype(np.float32)
    upper_rows = np.stack(upper_rows).astype(np.float32)
    eye_h = np.eye(n_heads, dtype=bool)
    pair_bd = np.stack([np.kron(eye_h, pm) for pm in pair_masks]).astype(np.float32)
    head_of_lane = np.arange(n_heads * dk)[None, :] // dk
    head_of_row = np.repeat(np.arange(n_heads), C)[:, None]
    head_mask = (head_of_lane == head_of_row).astype(np.float32)
    return sum_sel, upper_rows, pair_bd, head_mask


def _gla_kernel(q_ref, k_ref, g_ref, v_ref, z_ref, gsub_ref, sumsel_ref, upper_ref, pair_ref, hmask_ref,
                o_ref, state_scr, *, chunk, n_chunks, n_levels, chunks_per_iter):
    C = chunk
    H, DV = GLA_HEADS, GLA_DV

    @pl.when(pl.program_id(1) == 0)
    def _():
        state_scr[...] = jnp.zeros(state_scr.shape, F32)

    hmask = hmask_ref[...]
    ones_cols = jnp.ones((2 * C, LANE), BF16)

    def stack_heads(a):
        return jnp.concatenate([a.astype(BF16)] * H, axis=0) * hmask

    def one_chunk(r0, state):
        q = q_ref[0, pl.ds(r0, C), :].astype(F32)
        k = k_ref[0, pl.ds(r0, C), :].astype(F32)
        g = g_ref[0, pl.ds(r0, C), :]
        v = v_ref[0, pl.ds(r0, C), :]
        v_st = jnp.concatenate([v[:, hh * DV:(hh + 1) * DV] for hh in range(H)], axis=0)

        g_hi = g.astype(BF16)
        g_lo = (g - g_hi.astype(F32)).astype(BF16)
        g2 = jnp.concatenate([g_hi, g_lo], axis=0)
        d_all = jnp.dot(sumsel_ref[...], g2, preferred_element_type=F32)
        e_all = jnp.exp(d_all)
        col_sum = lax.dot_general(g2, ones_cols, TN_DIMS, preferred_element_type=F32)

        k_out = stack_heads(k * e_all[C:2 * C])
        new_state = (jnp.exp(col_sum) * state
                     + lax.dot_general(k_out, v_st, TN_DIMS, preferred_element_type=F32))

        a_bd = None
        for lvl in range(n_levels + 1):
            if lvl < n_levels:
                e = e_all[(2 + lvl) * C:(3 + lvl) * C]
                xs = stack_heads(jnp.where(upper_ref[lvl] > 0.5, q, k) * e)
                prod = lax.dot_general(xs, xs, NT_DIMS, preferred_element_type=F32)
            else:
                prod = lax.dot_general(stack_heads(q), stack_heads(k), NT_DIMS, preferred_element_type=F32)
            term = prod * pair_ref[lvl]
            a_bd = term if a_bd is None else a_bd + term

        q_in = stack_heads(q * e_all[0:C])
        lhs = jnp.concatenate([q_in, a_bd.astype(BF16)], axis=1)
        rhs = jnp.concatenate([state.astype(BF16), v_st], axis=0)
        o_st = jnp.dot(lhs, rhs, preferred_element_type=F32)

        for hh in range(H):
            o_h = _rms(o_st[hh * C:(hh + 1) * C], gsub_ref[...])
            z_h = z_ref[0, pl.ds(r0, C), hh * DV:(hh + 1) * DV].astype(F32)
            o_ref[0, pl.ds(r0, C), hh * DV:(hh + 1) * DV] = (o_h * _silu(z_h)).astype(o_ref.dtype)
        return new_state

    def loop_body(c, carry):
        state = state_scr[...]
        for u in range(chunks_per_iter):
            state = one_chunk(pl.multiple_of((c * chunks_per_iter + u) * C, C), state)
        state_scr[...] = state
        return carry

    lax.fori_loop(0, n_chunks // chunks_per_iter, loop_body, 0)


def _gated_linear_attention(qb, kb, glog, vb, zb, g_sub_b, *, chunk, rows_per_step, chunks_per_iter):
    B, S, b_qk = qb.shape
    b_v = vb.shape[-1]
    sum_sel, upper_rows, pair_bd, head_mask = _gla_tables(chunk, GLA_HEADS, GLA_DK)
    n_levels = upper_rows.shape[0]
    T = rows_per_step

    def row_spec(width):
        return pl.BlockSpec((1, T, width), lambda b, i: (b, i, 0))

    def const_spec(shape):
        return pl.BlockSpec(shape, lambda b, i: (0,) * len(shape))

    sum_sel2 = np.concatenate([sum_sel, sum_sel], axis=1)
    assert (T // chunk) % chunks_per_iter == 0
    kern = functools.partial(_gla_kernel, chunk=chunk, n_chunks=T // chunk, n_levels=n_levels,
                             chunks_per_iter=chunks_per_iter)
    return pl.pallas_call(
        kern,
        grid=(B, S // T),
        in_specs=[row_spec(b_qk), row_spec(b_qk), row_spec(b_qk), row_spec(b_v), row_spec(b_v),
                  const_spec((1, GLA_DV)), const_spec(sum_sel2.shape), const_spec(upper_rows.shape),
                  const_spec(pair_bd.shape), const_spec(head_mask.shape)],
        out_specs=row_spec(b_v),
        out_shape=jax.ShapeDtypeStruct((B, S, b_v), BF16),
        scratch_shapes=[pltpu.VMEM((b_qk, GLA_DV), F32)],
        compiler_params=pltpu.CompilerParams(
            dimension_semantics=("parallel", "arbitrary"), vmem_limit_bytes=VMEM_LIMIT),
        name="gla",
    )(qb, kb, glog, vb, zb, g_sub_b.reshape(1, GLA_DV), jnp.asarray(sum_sel2, BF16),
      jnp.asarray(upper_rows), jnp.asarray(pair_bd), jnp.asarray(head_mask, BF16))


def _merge_kernel(x_ref, ua_ref, ub_ref, ga_ref, gb_ref, wua_ref, wub_ref, wout_ref, gpost_ref, o_ref):
    ya = jnp.dot(ua_ref[0], wua_ref[...], preferred_element_type=F32)
    yb = jnp.dot(ub_ref[0], wub_ref[...], preferred_element_type=F32)
    y = jax.nn.sigmoid(ga_ref[0].astype(F32)) * ya + jax.nn.sigmoid(gb_ref[0].astype(F32)) * yb
    out = jnp.dot(y.astype(BF16), wout_ref[...], preferred_element_type=F32)
    o_ref[0] = x_ref[0] + _rms(out, gpost_ref[...])


def _merge(x, ua, ub, ga, gb, w_up_a, w_up_b, w_out, g_post, *, tm):
    B, S, D = x.shape

    def row_spec(width):
        return pl.BlockSpec((1, tm, width), lambda b, i: (b, i, 0))

    def const_spec(shape):
        return pl.BlockSpec(shape, lambda b, i: (0,) * len(shape))

    return pl.pallas_call(
        _merge_kernel,
        grid=(B, S // tm),
        in_specs=[row_spec(D), row_spec(ua.shape[-1]), row_spec(ub.shape[-1]), row_spec(D), row_spec(D),
                  const_spec(w_up_a.shape), const_spec(w_up_b.shape), const_spec(w_out.shape),
                  const_spec((1, D))],
        out_specs=row_spec(D),
        out_shape=jax.ShapeDtypeStruct((B, S, D), x.dtype),
        compiler_params=pltpu.CompilerParams(
            dimension_semantics=("parallel", "parallel"), vmem_limit_bytes=VMEM_LIMIT),
        name="merge",
    )(x, ua, ub, ga, gb, w_up_a, w_up_b, w_out, g_post.reshape(1, D))


def _layer(x, g_pre, w_in, lam_q1, lam_k1, lam_q2, lam_k2, g_sub_a, w_alpha, b_alpha, g_sub_b,
           w_up_a, w_up_b, w_out, g_post):
    D = x.shape[-1]
    lr0 = 2 * (DA_HEADS * 2 * DA_HEAD_DIM) + 2 * (DA_HEADS * DA_V_DIM) + 2 * (GLA_HEADS * GLA_DK) \
        + 2 * (GLA_HEADS * GLA_DV)
    lr1 = lr0 + GLA_GATE_RANK
    qa, ka, va, za, qb, kb, vb, zb, ga, gb, glog = _input_projection(
        x, g_pre, w_in[:, :lr0].astype(BF16), w_in[:, lr0:lr1].astype(BF16), w_in[:, lr1:].astype(BF16),
        w_alpha.astype(BF16), b_alpha, tm=512)
    ua = _diff_attention(qa, ka, va, za, lam_q1, lam_k1, lam_q2, lam_k2, g_sub_a, tq=256)
    ub = _gated_linear_attention(qb, kb, glog, vb, zb, g_sub_b, chunk=64, rows_per_step=512, chunks_per_iter=2)
    return _merge(x, ua, ub, ga, gb, w_up_a.astype(BF16), w_up_b.astype(BF16), w_out.astype(BF16),
                  g_post, tm=512)


def kernel(x, g_pre, w_in, lam_q1, lam_k1, lam_q2, lam_k2, g_sub_a, w_alpha, b_alpha, g_sub_b, w_up_a, w_up_b, w_out, g_post):
    depth = w_in.shape[0]
    assert depth == 1, "LAM_INIT is specialised to a single layer"
    for i in range(depth):
        x = _layer(x, g_pre[i], w_in[i], lam_q1[i], lam_k1[i], lam_q2[i], lam_k2[i], g_sub_a[i],
                   w_alpha[i], b_alpha[i], g_sub_b[i], w_up_a[i], w_up_b[i], w_out[i], g_post[i])
    return x
```

```python
import functools
import math

import numpy as np
import jax
import jax.numpy as jnp
from jax import lax
from jax.experimental import pallas as pl
from jax.experimental.pallas import tpu as pltpu

F32 = jnp.float32
BF16 = jnp.bfloat16

DA_HEADS = 4
DA_HEAD_DIM = 64
DA_V_DIM = 128
GLA_HEADS = 4
GLA_DK = 64
GLA_DV = 128
GLA_GATE_RANK = 16
GLA_TAU = 16.0
RMS_EPS = 1e-6
LAYER_IDX = 0
LAM_INIT = 0.8 - 0.6 * math.exp(-0.3 * LAYER_IDX)

LOG2E = math.log2(math.e)
LANE = 128
ONES_ROWS = 16
VMEM_LIMIT = 56 * 1024 * 1024

NT_DIMS = (((1,), (1,)), ((), ()))
TN_DIMS = (((0,), (0,)), ((), ()))


def _rms(x, g):
    return x * lax.rsqrt(jnp.mean(x * x, axis=-1, keepdims=True) + RMS_EPS) * g


def _silu(z):
    return z * jax.nn.sigmoid(z)


def _inproj_kernel(x_ref, gpre_ref, wmix_ref, wlr_ref, wgate_ref, walpha_ref, balpha_ref,
                   qa_ref, ka_ref, va_ref, za_ref, qb_ref, kb_ref, vb_ref, zb_ref,
                   ga_ref, gb_ref, glog_ref, h_scr, *, segments, q_scale_a, q_scale_b):
    xf = x_ref[0]
    h_scr[...] = _rms(xf, gpre_ref[...]).astype(BF16)
    outs = dict(qa=qa_ref, ka=ka_ref, va=va_ref, za=za_ref, qb=qb_ref, kb=kb_ref, vb=vb_ref,
                zb=zb_ref, ga=ga_ref, gb=gb_ref)
    weights = dict(mix=wmix_ref, gate=wgate_ref)
    scales = dict(qa=q_scale_a, qb=q_scale_b)
    for name, wname, c0, width, o0 in segments:
        acc = jnp.dot(h_scr[...], weights[wname][:, c0:c0 + width], preferred_element_type=F32)
        if name in scales:
            acc = acc * scales[name]
        outs[name][0, :, o0:o0 + width] = acc.astype(BF16)
    lr = jnp.dot(h_scr[...], wlr_ref[...], preferred_element_type=F32)
    z = jnp.dot(lr.astype(BF16), walpha_ref[...], preferred_element_type=F32) + balpha_ref[...]
    log_sig = jnp.minimum(z, 0.0) - jnp.log1p(jnp.exp(-jnp.abs(z)))
    glog_ref[0] = log_sig / GLA_TAU


def _input_projection(x, g_pre, w_mix, w_lr, w_gate, w_alpha, b_alpha, *, tm):
    B, S, D = x.shape
    a_qk, a_v = DA_HEADS * 2 * DA_HEAD_DIM, DA_HEADS * DA_V_DIM
    b_qk, b_v = GLA_HEADS * GLA_DK, GLA_HEADS * GLA_DV
    widths = [("qa", "mix", a_qk), ("ka", "mix", a_qk), ("va", "mix", a_v), ("za", "mix", a_v),
              ("qb", "mix", b_qk), ("kb", "mix", b_qk), ("vb", "mix", b_v), ("zb", "mix", b_v),
              ("ga", "gate", D), ("gb", "gate", D)]
    max_chunk = 512
    segments = []
    col = dict(mix=0, gate=0)
    for name, wname, width in widths:
        for o0 in range(0, width, max_chunk):
            segments.append((name, wname, col[wname] + o0, min(max_chunk, width - o0), o0))
        col[wname] += width
    assert col["mix"] == w_mix.shape[1] and col["gate"] == w_gate.shape[1]

    def row_spec(width):
        return pl.BlockSpec((1, tm, width), lambda b, i: (b, i, 0))

    def const_spec(shape):
        return pl.BlockSpec(shape, lambda b, i: (0,) * len(shape))

    out_widths = [w for _, _, w in widths]
    out_shape = [jax.ShapeDtypeStruct((B, S, w), BF16) for w in out_widths]
    out_shape.append(jax.ShapeDtypeStruct((B, S, b_qk), F32))
    out_specs = [row_spec(w) for w in out_widths] + [row_spec(b_qk)]
    kern = functools.partial(_inproj_kernel, segments=tuple(segments),
                             q_scale_a=DA_HEAD_DIM ** -0.5 * LOG2E,
                             q_scale_b=GLA_DK ** -0.5)
    return pl.pallas_call(
        kern,
        grid=(B, S // tm),
        in_specs=[row_spec(D), const_spec((1, D)), const_spec(w_mix.shape), const_spec(w_lr.shape),
                  const_spec(w_gate.shape), const_spec(w_alpha.shape), const_spec((1, b_qk))],
        out_specs=out_specs,
        out_shape=out_shape,
        scratch_shapes=[pltpu.VMEM((tm, D), BF16)],
        compiler_params=pltpu.CompilerParams(
            dimension_semantics=("parallel", "parallel"), vmem_limit_bytes=VMEM_LIMIT),
        name="inproj",
    )(x, g_pre.reshape(1, D), w_mix, w_lr, w_gate, w_alpha, b_alpha.reshape(1, b_qk))


def _attn_kernel(lq1_ref, lk1_ref, lq2_ref, lk2_ref, q_ref, qn_ref, k_ref, v_ref, za_ref, gsub_ref,
                 o_ref, qs_scr, ks_scr, vt_scr, st_scr, mx_scr, p_scr, alpha_scr, m_scr, acc_scr, *, tq, tk):
    i = pl.program_id(1)
    H, DV = DA_HEADS, DA_V_DIM
    n_kv = vt_scr.shape[1]
    slope_hi, slope_lo, slopes = [], [], []
    for h in range(H):
        s2 = np.float32(2.0 ** (-8.0 * (h + 1) / H) * LOG2E)
        hi = np.asarray(s2, dtype=BF16)
        lo = np.asarray(s2 - np.float32(hi), dtype=BF16)
        slope_hi.append(float(hi))
        slope_lo.append(float(lo))
        slopes.append(float(hi) + float(lo))

    def prepare_keys_values():
        lane = lax.broadcasted_iota(jnp.int32, (tk, LANE), 1)
        key = lax.broadcasted_iota(jnp.int32, (tk, LANE), 0).astype(F32)
        key_lanes = jnp.where(lane < 2, key, 0.0).astype(BF16)
        ones_rows = jnp.ones((ONES_ROWS, tk), BF16)
        def prep(j, carry):
            k0 = pl.multiple_of(j * tk, tk)
            for h in range(H):
                ks_scr[h, pl.ds(k0, tk), 0:LANE] = k_ref[0, pl.ds(k0, tk), h * LANE:(h + 1) * LANE]
                ks_scr[h, pl.ds(k0, tk), LANE:2 * LANE] = key_lanes
                vt_scr[h, j, 0:DV, :] = v_ref[0, pl.ds(k0, tk), h * DV:(h + 1) * DV].astype(F32).T.astype(BF16)
                vt_scr[h, j, DV:DV + ONES_ROWS, :] = ones_rows
            return carry
        lax.fori_loop(0, n_kv, prep, 0)

    def load_queries(ref):
        lane = lax.broadcasted_iota(jnp.int32, (tq, LANE), 1)
        zero = jnp.zeros((tq, LANE), BF16)
        for h in range(H):
            q = ref[0, :, h * LANE:(h + 1) * LANE]
            slope_lanes = jnp.where(lane == 0, slope_hi[h],
                                    jnp.where(lane == 1, slope_lo[h], 0.0)).astype(BF16)
            qs_scr[h, 0:tq, 0:LANE] = jnp.where(lane < DA_HEAD_DIM, q, zero)
            qs_scr[h, tq:2 * tq, 0:LANE] = jnp.where(lane >= DA_HEAD_DIM, q, zero)
            qs_scr[h, 0:tq, LANE:2 * LANE] = slope_lanes
            qs_scr[h, tq:2 * tq, LANE:2 * LANE] = slope_lanes

    def produce(j, buf, masked=False):
        k0 = pl.multiple_of(j * tk, tk)
        if masked:
            key = lax.broadcasted_iota(jnp.int32, (tk, 2 * tq), 0)
            qcol = lax.broadcasted_iota(jnp.int32, (tk, 2 * tq), 1)
            visible = key <= jnp.where(qcol >= tq, qcol - tq, qcol)
        for h in range(H):
            st = lax.dot_general(ks_scr[h, pl.ds(k0, tk), :], qs_scr[h], NT_DIMS,
                                 preferred_element_type=F32)
            if masked:
                st = jnp.where(visible, st, -1e30)
            st_scr[buf, h] = st
            mx_scr[buf, h] = jnp.max(st, axis=0, keepdims=True)

    def softmax(j, buf):
        tile_dist = (j * tk - i * tq).astype(F32)
        for h in range(H):
            off = slopes[h] * tile_dist
            m_prev = m_scr[h]
            m_next = jnp.maximum(m_prev, mx_scr[buf, h] + off)
            alpha_scr[buf, h] = jnp.exp2(m_prev - m_next)
            p_scr[buf, h] = jnp.exp2(st_scr[buf, h] - (m_next - off)).astype(BF16)
            m_scr[h] = m_next

    def weighted_values(j, buf):
        for h in range(H):
            acc_scr[h] = alpha_scr[buf, h] * acc_scr[h] + jnp.dot(vt_scr[h, j], p_scr[buf, h],
                                                                  preferred_element_type=F32)

    def tile_at(pos):
        return jnp.where(pos == 0, i, pos - 1)

    @pl.when(i == 0)
    def _():
        prepare_keys_values()
        load_queries(q_ref)
        produce(0, 0, masked=True)

    m_scr[...] = jnp.full(m_scr.shape, -jnp.inf, F32)
    acc_scr[...] = jnp.zeros(acc_scr.shape, F32)

    @pl.when(i == 0)
    def _():
        softmax(0, 0)
        weighted_values(0, 0)

    @pl.when(i > 0)
    def _():
        produce(0, 1)
        softmax(i, 0)

        def pair(pp, carry):
            p = 2 * pp
            produce(p + 1, 0)
            softmax(p, 1)
            weighted_values(tile_at(p), 0)
            produce(p + 2, 1)
            softmax(p + 1, 0)
            weighted_values(p, 1)
            return carry
        n_pairs = (i - 1) // 2
        lax.fori_loop(0, n_pairs, pair, 0)

        @pl.when(i % 2 == 1)
        def _():
            softmax(i - 1, 1)
            weighted_values(tile_at(i - 1), 0)
            weighted_values(i - 1, 1)

        @pl.when(i % 2 == 0)
        def _():
            produce(i - 1, 0)
            softmax(i - 2, 1)
            weighted_values(tile_at(i - 2), 0)
            softmax(i - 1, 0)
            weighted_values(i - 2, 1)
            weighted_values(i - 1, 0)

    n_q = pl.num_programs(1)
    load_queries(qn_ref)
    produce(jnp.minimum(i + 1, n_q - 1), 0, masked=True)

    f = lambda r: r[...].astype(F32)
    lam = (jnp.exp(jnp.sum(f(lq1_ref) * f(lk1_ref), axis=1, keepdims=True))
           - jnp.exp(jnp.sum(f(lq2_ref) * f(lk2_ref), axis=1, keepdims=True)) + LAM_INIT)
    for h in range(H):
        on = acc_scr[h, 0:DV, :] / acc_scr[h, DV:DV + 1, :]
        o = (on[:, 0:tq] - lam * on[:, tq:2 * tq]).T
        o = _rms(o, gsub_ref[...]) * (1.0 - LAM_INIT)
        z = za_ref[0, :, h * DV:(h + 1) * DV].astype(F32)
        o_ref[0, :, h * DV:(h + 1) * DV] = (o * _silu(z)).astype(o_ref.dtype)


def _diff_attention(qa, ka, va, za, lam_q1, lam_k1, lam_q2, lam_k2, g_sub_a, *, tq):
    B, S, W = qa.shape
    tk = tq
    assert W == DA_HEADS * LANE and S % tq == 0
    vec = lambda v: v.reshape(1, DA_HEAD_DIM)
    vec_spec = pl.BlockSpec((1, DA_HEAD_DIM), lambda b, i: (0, 0))
    tile_spec = pl.BlockSpec((1, tq, W), lambda b, i: (b, i, 0))
    seq_spec = pl.BlockSpec((1, S, W), lambda b, i: (b, 0, 0))
    n_q = S // tq
    next_tile_spec = pl.BlockSpec((1, tq, W), lambda b, i: (b, jnp.minimum(i + 1, n_q - 1), 0))
    kern = functools.partial(_attn_kernel, tq=tq, tk=tk)
    return pl.pallas_call(
        kern,
        grid=(B, n_q),
        in_specs=[vec_spec, vec_spec, vec_spec, vec_spec, tile_spec, next_tile_spec, seq_spec, seq_spec,
                  tile_spec,
                  pl.BlockSpec((1, DA_V_DIM), lambda b, i: (0, 0))],
        out_specs=tile_spec,
        out_shape=jax.ShapeDtypeStruct((B, S, W), BF16),
        scratch_shapes=[pltpu.VMEM((DA_HEADS, 2 * tq, 2 * LANE), BF16),
                        pltpu.VMEM((DA_HEADS, S, 2 * LANE), BF16),
                        pltpu.VMEM((DA_HEADS, S // tk, DA_V_DIM + ONES_ROWS, tk), BF16),
                        pltpu.VMEM((2, DA_HEADS, tk, 2 * tq), F32),
                        pltpu.VMEM((2, DA_HEADS, 1, 2 * tq), F32),
                        pltpu.VMEM((2, DA_HEADS, tk, 2 * tq), BF16),
                        pltpu.VMEM((2, DA_HEADS, 1, 2 * tq), F32),
                        pltpu.VMEM((DA_HEADS, 1, 2 * tq), F32),
                        pltpu.VMEM((DA_HEADS, DA_V_DIM + ONES_ROWS, 2 * tq), F32)],
        compiler_params=pltpu.CompilerParams(
            dimension_semantics=("parallel", "arbitrary"), vmem_limit_bytes=VMEM_LIMIT),
        name="diffattn",
    )(vec(lam_q1), vec(lam_k1), vec(lam_q2), vec(lam_k2), qa, qa, ka, va, za,
      g_sub_a.reshape(1, DA_V_DIM))


def _gla_tables(C, n_heads, dk):
    t = np.arange(C)[:, None]
    u = np.arange(C)[None, :]
    sums = [(u <= t), (u > t)]
    halves = []
    m = C // 2
    while m >= 1:
        halves.append(m)
        m //= 2
    upper_rows, pair_masks = [], []
    for m in halves:
        blk = 2 * m
        mid = (t // blk) * blk + m
        upper = (t % blk) >= m
        sums.append(np.where(upper, (u >= mid) & (u <= t), (u > t) & (u <= mid - 1)))
        upper_rows.append(np.broadcast_to(upper, (C, n_heads * dk)))
        tt, ss = np.arange(C)[:, None], np.arange(C)[None, :]
        pair_masks.append(((tt // blk) == (ss // blk)) & ((tt % blk) >= m) & ((ss % blk) < m))
    pair_masks.append(np.eye(C, dtype=bool))
    sum_sel = np.concatenate(sums, axis=0).astype(np.float32)
    upper_rows = np.stack(upper_rows).astype(np.float32)
    eye_h = np.eye(n_heads, dtype=bool)
    pair_bd = np.stack([np.kron(eye_h, pm) for pm in pair_masks]).astype(np.float32)
    head_of_lane = np.arange(n_heads * dk)[None, :] // dk
    head_of_row = np.repeat(np.arange(n_heads), C)[:, None]
    head_mask = (head_of_lane == head_of_row).astype(np.float32)
    return sum_sel, upper_rows, pair_bd, head_mask


def _gla_kernel(q_ref, k_ref, g_ref, v_ref, z_ref, gsub_ref, sumsel_ref, upper_ref, pair_ref, hmask_ref,
                o_ref, state_scr, *, chunk, n_chunks, n_levels, chunks_per_iter):
    C = chunk
    H, DV = GLA_HEADS, GLA_DV

    @pl.when(pl.program_id(1) == 0)
    def _():
        state_scr[...] = jnp.zeros(state_scr.shape, F32)

    hmask = hmask_ref[...]
    ones_cols = jnp.ones((2 * C, LANE), BF16)

    def stack_heads(a):
        return jnp.concatenate([a.astype(BF16)] * H, axis=0) * hmask

    def one_chunk(r0, state):
        q = q_ref[0, pl.ds(r0, C), :].astype(F32)
        k = k_ref[0, pl.ds(r0, C), :].astype(F32)
        g = g_ref[0, pl.ds(r0, C), :]
        v = v_ref[0, pl.ds(r0, C), :]
        v_st = jnp.concatenate([v[:, hh * DV:(hh + 1) * DV] for hh in range(H)], axis=0)

        g_hi = g.astype(BF16)
        g_lo = (g - g_hi.astype(F32)).astype(BF16)
        g2 = jnp.concatenate([g_hi, g_lo], axis=0)
        d_all = jnp.dot(sumsel_ref[...], g2, preferred_element_type=F32)
        e_all = jnp.exp(d_all)
        col_sum = lax.dot_general(g2, ones_cols, TN_DIMS, preferred_element_type=F32)

        k_out = stack_heads(k * e_all[C:2 * C])
        new_state = (jnp.exp(col_sum) * state
                     + lax.dot_general(k_out, v_st, TN_DIMS, preferred_element_type=F32))

        a_bd = None
        for lvl in range(n_levels + 1):
            if lvl < n_levels:
                e = e_all[(2 + lvl) * C:(3 + lvl) * C]
                xs = stack_heads(jnp.where(upper_ref[lvl] > 0.5, q, k) * e)
                prod = lax.dot_general(xs, xs, NT_DIMS, preferred_element_type=F32)
            else:
                prod = lax.dot_general(stack_heads(q), stack_heads(k), NT_DIMS, preferred_element_type=F32)
            term = prod * pair_ref[lvl]
            a_bd = term if a_bd is None else a_bd + term

        q_in = stack_heads(q * e_all[0:C])
        lhs = jnp.concatenate([q_in, a_bd.astype(BF16)], axis=1)
        rhs = jnp.concatenate([state.astype(BF16), v_st], axis=0)
        o_st = jnp.dot(lhs, rhs, preferred_element_type=F32)

        for hh in range(H):
            o_h = _rms(o_st[hh * C:(hh + 1) * C], gsub_ref[...])
            z_h = z_ref[0, pl.ds(r0, C), hh * DV:(hh + 1) * DV].astype(F32)
            o_ref[0, pl.ds(r0, C), hh * DV:(hh + 1) * DV] = (o_h * _silu(z_h)).astype(o_ref.dtype)
        return new_state

    def loop_body(c, carry):
        state = state_scr[...]
        for u in range(chunks_per_iter):
            state = one_chunk(pl.multiple_of((c * chunks_per_iter + u) * C, C), state)
        state_scr[...] = state
        return carry

    lax.fori_loop(0, n_chunks // chunks_per_iter, loop_body, 0)


def _gated_linear_attention(qb, kb, glog, vb, zb, g_sub_b, *, chunk, rows_per_step, chunks_per_iter):
    B, S, b_qk = qb.shape
    b_v = vb.shape[-1]
    sum_sel, upper_rows, pair_bd, head_mask = _gla_tables(chunk, GLA_HEADS, GLA_DK)
    n_levels = upper_rows.shape[0]
    T = rows_per_step

    def row_spec(width):
        return pl.BlockSpec((1, T, width), lambda b, i: (b, i, 0))

    def const_spec(shape):
        return pl.BlockSpec(shape, lambda b, i: (0,) * len(shape))

    sum_sel2 = np.concatenate([sum_sel, sum_sel], axis=1)
    assert (T // chunk) % chunks_per_iter == 0
    kern = functools.partial(_gla_kernel, chunk=chunk, n_chunks=T // chunk, n_levels=n_levels,
                             chunks_per_iter=chunks_per_iter)
    return pl.pallas_call(
        kern,
        grid=(B, S // T),
        in_specs=[row_spec(b_qk), row_spec(b_qk), row_spec(b_qk), row_spec(b_v), row_spec(b_v),
                  const_spec((1, GLA_DV)), const_spec(sum_sel2.shape), const_spec(upper_rows.shape),
                  const_spec(pair_bd.shape), const_spec(head_mask.shape)],
        out_specs=row_spec(b_v),
        out_shape=jax.ShapeDtypeStruct((B, S, b_v), BF16),
        scratch_shapes=[pltpu.VMEM((b_qk, GLA_DV), F32)],
        compiler_params=pltpu.CompilerParams(
            dimension_semantics=("parallel", "arbitrary"), vmem_limit_bytes=VMEM_LIMIT),
        name="gla",
    )(qb, kb, glog, vb, zb, g_sub_b.reshape(1, GLA_DV), jnp.asarray(sum_sel2, BF16),
      jnp.asarray(upper_rows), jnp.asarray(pair_bd), jnp.asarray(head_mask, BF16))


def _merge_kernel(x_ref, ua_ref, ub_ref, ga_ref, gb_ref, wua_ref, wub_ref, wout_ref, gpost_ref, o_ref):
    ya = jnp.dot(ua_ref[0], wua_ref[...], preferred_element_type=F32)
    yb = jnp.dot(ub_ref[0], wub_ref[...], preferred_element_type=F32)
    y = jax.nn.sigmoid(ga_ref[0].astype(F32)) * ya + jax.nn.sigmoid(gb_ref[0].astype(F32)) * yb
    out = jnp.dot(y.astype(BF16), wout_ref[...], preferred_element_type=F32)
    o_ref[0] = x_ref[0] + _rms(out, gpost_ref[...])


def _merge(x, ua, ub, ga, gb, w_up_a, w_up_b, w_out, g_post, *, tm):
    B, S, D = x.shape

    def row_spec(width):
        return pl.BlockSpec((1, tm, width), lambda b, i: (b, i, 0))

    def const_spec(shape):
        return pl.BlockSpec(shape, lambda b, i: (0,) * len(shape))

    return pl.pallas_call(
        _merge_kernel,
        grid=(B, S // tm),
        in_specs=[row_spec(D), row_spec(ua.shape[-1]), row_spec(ub.shape[-1]), row_spec(D), row_spec(D),
                  const_spec(w_up_a.shape), const_spec(w_up_b.shape), const_spec(w_out.shape),
                  const_spec((1, D))],
        out_specs=row_spec(D),
        out_shape=jax.ShapeDtypeStruct((B, S, D), x.dtype),
        compiler_params=pltpu.CompilerParams(
            dimension_semantics=("parallel", "parallel"), vmem_limit_bytes=VMEM_LIMIT),
        name="merge",
    )(x, ua, ub, ga, gb, w_up_a, w_up_b, w_out, g_post.reshape(1, D))


def _layer(x, g_pre, w_in, lam_q1, lam_k1, lam_q2, lam_k2, g_sub_a, w_alpha, b_alpha, g_sub_b,
           w_up_a, w_up_b, w_out, g_post):
    D = x.shape[-1]
    lr0 = 2 * (DA_HEADS * 2 * DA_HEAD_DIM) + 2 * (DA_HEADS * DA_V_DIM) + 2 * (GLA_HEADS * GLA_DK) \
        + 2 * (GLA_HEADS * GLA_DV)
    lr1 = lr0 + GLA_GATE_RANK
    qa, ka, va, za, qb, kb, vb, zb, ga, gb, glog = _input_projection(
        x, g_pre, w_in[:, :lr0].astype(BF16), w_in[:, lr0:lr1].astype(BF16), w_in[:, lr1:].astype(BF16),
        w_alpha.astype(BF16), b_alpha, tm=512)
    ua = _diff_attention(qa, ka, va, za, lam_q1, lam_k1, lam_q2, lam_k2, g_sub_a, tq=256)
    ub = _gated_linear_attention(qb, kb, glog, vb, zb, g_sub_b, chunk=64, rows_per_step=512, chunks_per_iter=4)
    return _merge(x, ua, ub, ga, gb, w_up_a.astype(BF16), w_up_b.astype(BF16), w_out.astype(BF16),
                  g_post, tm=512)


def kernel(x, g_pre, w_in, lam_q1, lam_k1, lam_q2, lam_k2, g_sub_a, w_alpha, b_alpha, g_sub_b, w_up_a, w_up_b, w_out, g_post):
    depth = w_in.shape[0]
    assert depth == 1, "LAM_INIT is specialised to a single layer"
    first = lambda p: p.reshape(p.shape[1:])
    return _layer(x, *(first(p) for p in (g_pre, w_in, lam_q1, lam_k1, lam_q2, lam_k2, g_sub_a, w_alpha,
                                          b_alpha, g_sub_b, w_up_a, w_up_b, w_out, g_post)))
```

```python
import functools
import math

import numpy as np
import jax
import jax.numpy as jnp
from jax import lax
from jax.experimental import pallas as pl
from jax.experimental.pallas import tpu as pltpu

F32 = jnp.float32
BF16 = jnp.bfloat16

DA_HEADS = 4
DA_HEAD_DIM = 64
DA_V_DIM = 128
GLA_HEADS = 4
GLA_DK = 64
GLA_DV = 128
GLA_GATE_RANK = 16
GLA_TAU = 16.0
RMS_EPS = 1e-6
LAYER_IDX = 0
LAM_INIT = 0.8 - 0.6 * math.exp(-0.3 * LAYER_IDX)

LOG2E = math.log2(math.e)
LANE = 128
ONES_ROWS = 16
SLOPE_PARTS = 3
VMEM_LIMIT = 56 * 1024 * 1024

NT_DIMS = (((1,), (1,)), ((), ()))
TN_DIMS = (((0,), (0,)), ((), ()))


def _rms(x, g):
    return x * lax.rsqrt(jnp.mean(x * x, axis=-1, keepdims=True) + RMS_EPS) * g


def _silu(z):
    return z * jax.nn.sigmoid(z)


def _inproj_kernel(x_ref, gpre_ref, wmix_ref, wlr_ref, wgate_ref, walpha_ref, balpha_ref,
                   qa_ref, ka_ref, va_ref, za_ref, qb_ref, kb_ref, vb_ref, zb_ref,
                   ga_ref, gb_ref, glog_ref, h_scr, t_scr, *, segments, feature_major, q_scale_a, q_scale_b):
    xf = x_ref[0]
    h_scr[...] = _rms(xf, gpre_ref[...]).astype(BF16)
    outs = dict(qa=qa_ref, ka=ka_ref, va=va_ref, za=za_ref, qb=qb_ref, kb=kb_ref, vb=vb_ref,
                zb=zb_ref, ga=ga_ref, gb=gb_ref)
    weights = dict(mix=wmix_ref, gate=wgate_ref)
    scales = dict(qa=q_scale_a, qb=q_scale_b)
    for name, wname, c0, width, o0 in segments:
        acc = jnp.dot(h_scr[...], weights[wname][:, c0:c0 + width], preferred_element_type=F32)
        if name in scales:
            acc = acc * scales[name]
        if name in feature_major:
            t_scr[...] = acc
            outs[name][0, o0:o0 + width, :] = t_scr[...].T.astype(BF16)
        else:
            outs[name][0, :, o0:o0 + width] = acc.astype(BF16)
    lr = jnp.dot(h_scr[...], wlr_ref[...], preferred_element_type=F32)
    z = jnp.dot(lr.astype(BF16), walpha_ref[...], preferred_element_type=F32) + balpha_ref[...]
    log_sig = jnp.minimum(z, 0.0) - jnp.log1p(jnp.exp(-jnp.abs(z)))
    glog_ref[0] = log_sig / GLA_TAU


def _input_projection(x, g_pre, w_mix, w_lr, w_gate, w_alpha, b_alpha, *, tm):
    B, S, D = x.shape
    a_qk, a_v = DA_HEADS * 2 * DA_HEAD_DIM, DA_HEADS * DA_V_DIM
    b_qk, b_v = GLA_HEADS * GLA_DK, GLA_HEADS * GLA_DV
    widths = [("qa", "mix", a_qk), ("ka", "mix", a_qk), ("va", "mix", a_v), ("za", "mix", a_v),
              ("qb", "mix", b_qk), ("kb", "mix", b_qk), ("vb", "mix", b_v), ("zb", "mix", b_v),
              ("ga", "gate", D), ("gb", "gate", D)]
    max_chunk = 512
    segments = []
    col = dict(mix=0, gate=0)
    for name, wname, width in widths:
        for o0 in range(0, width, max_chunk):
            segments.append((name, wname, col[wname] + o0, min(max_chunk, width - o0), o0))
        col[wname] += width
    assert col["mix"] == w_mix.shape[1] and col["gate"] == w_gate.shape[1]

    def row_spec(width):
        return pl.BlockSpec((1, tm, width), lambda b, i: (b, i, 0))

    def const_spec(shape):
        return pl.BlockSpec(shape, lambda b, i: (0,) * len(shape))

    feature_major = ("qa", "va")
    out_shape, out_specs = [], []
    for name, _, w in widths:
        if name in feature_major:
            out_shape.append(jax.ShapeDtypeStruct((B, w, S), BF16))
            out_specs.append(pl.BlockSpec((1, w, tm), lambda b, i: (b, 0, i)))
        else:
            out_shape.append(jax.ShapeDtypeStruct((B, S, w), BF16))
            out_specs.append(row_spec(w))
    out_shape.append(jax.ShapeDtypeStruct((B, S, b_qk), F32))
    out_specs.append(row_spec(b_qk))
    kern = functools.partial(_inproj_kernel, segments=tuple(segments), feature_major=feature_major,
                             q_scale_a=DA_HEAD_DIM ** -0.5 * LOG2E,
                             q_scale_b=GLA_DK ** -0.5)
    return pl.pallas_call(
        kern,
        grid=(B, S // tm),
        in_specs=[row_spec(D), const_spec((1, D)), const_spec(w_mix.shape), const_spec(w_lr.shape),
                  const_spec(w_gate.shape), const_spec(w_alpha.shape), const_spec((1, b_qk))],
        out_specs=out_specs,
        out_shape=out_shape,
        scratch_shapes=[pltpu.VMEM((tm, D), BF16), pltpu.VMEM((tm, max_chunk), F32)],
        compiler_params=pltpu.CompilerParams(
            dimension_semantics=("parallel", "parallel"), vmem_limit_bytes=VMEM_LIMIT),
        name="inproj",
    )(x, g_pre.reshape(1, D), w_mix, w_lr, w_gate, w_alpha, b_alpha.reshape(1, b_qk))


def _attn_kernel(lq1_ref, lk1_ref, lq2_ref, lk2_ref, q_ref, qn_ref, k_ref, vt_ref, za_ref, gsub_ref,
                 o_ref, qs_scr, ks_scr, vt_scr, st_scr, mx_scr, p_scr, alpha_scr, m_scr, acc_scr, *, tq, tk):
    i = pl.program_id(1)
    H, DV = DA_HEADS, DA_V_DIM
    n_kv = vt_scr.shape[1]
    slope_parts, slopes = [], []
    for h in range(H):
        rest = np.float32(2.0 ** (-8.0 * (h + 1) / H) * LOG2E)
        slopes.append(float(rest))
        parts = []
        for _ in range(SLOPE_PARTS):
            part = np.asarray(rest, dtype=BF16)
            parts.append(float(part))
            rest = np.float32(rest - np.float32(part))
        assert rest == 0.0
        slope_parts.append(parts)

    def prepare_keys_values():
        lane = lax.broadcasted_iota(jnp.int32, (tk, LANE), 1)
        key = lax.broadcasted_iota(jnp.int32, (tk, LANE), 0).astype(F32)
        key_lanes = jnp.where(lane < SLOPE_PARTS, key, 0.0).astype(BF16)
        ones_rows = jnp.ones((ONES_ROWS, tk), BF16)
        def prep(j, carry):
            k0 = pl.multiple_of(j * tk, tk)
            for h in range(H):
                ks_scr[h, pl.ds(k0, tk), 0:LANE] = k_ref[0, pl.ds(k0, tk), h * LANE:(h + 1) * LANE]
                ks_scr[h, pl.ds(k0, tk), LANE:2 * LANE] = key_lanes
            return carry
        lax.fori_loop(0, n_kv, prep, 0)
        for j in range(n_kv):
            for h in range(H):
                vt_scr[h, j, 0:DV, :] = vt_ref[0, h * DV:(h + 1) * DV, j * tk:(j + 1) * tk]
                vt_scr[h, j, DV:DV + ONES_ROWS, :] = ones_rows

    def load_queries(ref):
        row = lax.broadcasted_iota(jnp.int32, (LANE, tq), 0)
        zero = jnp.zeros((LANE, tq), BF16)
        for h in range(H):
            qt = ref[0, h * LANE:(h + 1) * LANE, :]
            qs_scr[h, 0:LANE, 0:tq] = jnp.where(row < DA_HEAD_DIM, qt, zero)
            qs_scr[h, 0:LANE, tq:2 * tq] = jnp.where(row >= DA_HEAD_DIM, qt, zero)

    def load_slopes():
        row = lax.broadcasted_iota(jnp.int32, (LANE, 2 * tq), 0)
        for h in range(H):
            rows = jnp.zeros((LANE, 2 * tq), F32)
            for r, part in enumerate(slope_parts[h]):
                rows = jnp.where(row == r, part, rows)
            qs_scr[h, LANE:2 * LANE, :] = rows.astype(BF16)

    def produce(j, buf, masked=False):
        k0 = pl.multiple_of(j * tk, tk)
        if masked:
            key = lax.broadcasted_iota(jnp.int32, (tk, 2 * tq), 0)
            qcol = lax.broadcasted_iota(jnp.int32, (tk, 2 * tq), 1)
            visible = key <= jnp.where(qcol >= tq, qcol - tq, qcol)
        for h in range(H):
            st = jnp.dot(ks_scr[h, pl.ds(k0, tk), :], qs_scr[h], preferred_element_type=F32)
            if masked:
                st = jnp.where(visible, st, -1e30)
            st_scr[buf, h] = st
            mx_scr[buf, h] = jnp.max(st, axis=0, keepdims=True)

    def softmax(j, buf):
        tile_dist = (j * tk - i * tq).astype(F32)
        for h in range(H):
            off = slopes[h] * tile_dist
            m_prev = m_scr[h]
            m_next = jnp.maximum(m_prev, mx_scr[buf, h] + off)
            alpha_scr[buf, h] = jnp.exp2(m_prev - m_next)
            p_scr[buf, h] = jnp.exp2(st_scr[buf, h] - (m_next - off)).astype(BF16)
            m_scr[h] = m_next

    def weighted_values(j, buf):
        for h in range(H):
            acc_scr[h] = alpha_scr[buf, h] * acc_scr[h] + jnp.dot(vt_scr[h, j], p_scr[buf, h],
                                                                  preferred_element_type=F32)

    def tile_at(pos):
        return jnp.where(pos == 0, i, pos - 1)

    @pl.when(i == 0)
    def _():
        prepare_keys_values()
        load_slopes()
        load_queries(q_ref)
        produce(0, 0, masked=True)

    m_scr[...] = jnp.full(m_scr.shape, -jnp.inf, F32)
    acc_scr[...] = jnp.zeros(acc_scr.shape, F32)

    @pl.when(i == 0)
    def _():
        softmax(0, 0)
        weighted_values(0, 0)

    @pl.when(i > 0)
    def _():
        produce(0, 1)
        softmax(i, 0)

        def pair(pp, carry):
            p = 2 * pp
            produce(p + 1, 0)
            weighted_values(tile_at(p), 0)
            softmax(p, 1)
            produce(p + 2, 1)
            weighted_values(p, 1)
            softmax(p + 1, 0)
            return carry
        n_pairs = (i - 1) // 2
        lax.fori_loop(0, n_pairs, pair, 0)

        @pl.when(i % 2 == 1)
        def _():
            softmax(i - 1, 1)
            weighted_values(tile_at(i - 1), 0)
            weighted_values(i - 1, 1)

        @pl.when(i % 2 == 0)
        def _():
            produce(i - 1, 0)
            softmax(i - 2, 1)
            weighted_values(tile_at(i - 2), 0)
            softmax(i - 1, 0)
            weighted_values(i - 2, 1)
            weighted_values(i - 1, 0)

    n_q = pl.num_programs(1)
    load_queries(qn_ref)
    produce(jnp.minimum(i + 1, n_q - 1), 0, masked=True)

    f = lambda r: r[...].astype(F32)
    lam = (jnp.exp(jnp.sum(f(lq1_ref) * f(lk1_ref), axis=1, keepdims=True))
           - jnp.exp(jnp.sum(f(lq2_ref) * f(lk2_ref), axis=1, keepdims=True)) + LAM_INIT)
    for h in range(H):
        on = acc_scr[h, 0:DV, :] / acc_scr[h, DV:DV + 1, :]
        o = (on[:, 0:tq] - lam * on[:, tq:2 * tq]).T
        o = _rms(o, gsub_ref[...]) * (1.0 - LAM_INIT)
        z = za_ref[0, :, h * DV:(h + 1) * DV].astype(F32)
        o_ref[0, :, h * DV:(h + 1) * DV] = (o * _silu(z)).astype(o_ref.dtype)


def _diff_attention(qa_t, ka, va_t, za, lam_q1, lam_k1, lam_q2, lam_k2, g_sub_a, *, tq):
    B, S, W = ka.shape
    tk = tq
    assert W == DA_HEADS * LANE and S % tq == 0 and qa_t.shape == (B, W, S) and va_t.shape == (B, W, S)
    vec = lambda v: v.reshape(1, DA_HEAD_DIM)
    vec_spec = pl.BlockSpec((1, DA_HEAD_DIM), lambda b, i: (0, 0))
    tile_spec = pl.BlockSpec((1, tq, W), lambda b, i: (b, i, 0))
    seq_spec = pl.BlockSpec((1, S, W), lambda b, i: (b, 0, 0))
    n_q = S // tq
    q_spec = pl.BlockSpec((1, W, tq), lambda b, i: (b, 0, i))
    next_q_spec = pl.BlockSpec((1, W, tq), lambda b, i: (b, 0, jnp.minimum(i + 1, n_q - 1)))
    seq_t_spec = pl.BlockSpec((1, W, S), lambda b, i: (b, 0, 0))
    kern = functools.partial(_attn_kernel, tq=tq, tk=tk)
    return pl.pallas_call(
        kern,
        grid=(B, n_q),
        in_specs=[vec_spec, vec_spec, vec_spec, vec_spec, q_spec, next_q_spec, seq_spec, seq_t_spec,
                  tile_spec,
                  pl.BlockSpec((1, DA_V_DIM), lambda b, i: (0, 0))],
        out_specs=tile_spec,
        out_shape=jax.ShapeDtypeStruct((B, S, W), BF16),
        scratch_shapes=[pltpu.VMEM((DA_HEADS, 2 * LANE, 2 * tq), BF16),
                        pltpu.VMEM((DA_HEADS, S, 2 * LANE), BF16),
                        pltpu.VMEM((DA_HEADS, S // tk, DA_V_DIM + ONES_ROWS, tk), BF16),
                        pltpu.VMEM((2, DA_HEADS, tk, 2 * tq), F32),
                        pltpu.VMEM((2, DA_HEADS, 1, 2 * tq), F32),
                        pltpu.VMEM((2, DA_HEADS, tk, 2 * tq), BF16),
                        pltpu.VMEM((2, DA_HEADS, 1, 2 * tq), F32),
                        pltpu.VMEM((DA_HEADS, 1, 2 * tq), F32),
                        pltpu.VMEM((DA_HEADS, DA_V_DIM + ONES_ROWS, 2 * tq), F32)],
        compiler_params=pltpu.CompilerParams(
            dimension_semantics=("parallel", "arbitrary"), vmem_limit_bytes=VMEM_LIMIT),
        name="diffattn",
    )(vec(lam_q1), vec(lam_k1), vec(lam_q2), vec(lam_k2), qa_t, qa_t, ka, va_t, za,
      g_sub_a.reshape(1, DA_V_DIM))


def _gla_tables(C, n_heads, dk):
    t = np.arange(C)[:, None]
    u = np.arange(C)[None, :]
    sums = [(u <= t), (u > t)]
    halves = []
    m = C // 2
    while m >= 1:
        halves.append(m)
        m //= 2
    upper_rows, pair_masks = [], []
    for m in halves:
        blk = 2 * m
        mid = (t // blk) * blk + m
        upper = (t % blk) >= m
        sums.append(np.where(upper, (u >= mid) & (u <= t), (u > t) & (u <= mid - 1)))
        upper_rows.append(np.broadcast_to(upper, (C, n_heads * dk)))
        tt, ss = np.arange(C)[:, None], np.arange(C)[None, :]
        pair_masks.append(((tt // blk) == (ss // blk)) & ((tt % blk) >= m) & ((ss % blk) < m))
    pair_masks.append(np.eye(C, dtype=bool))
    sum_sel = np.concatenate(sums, axis=0).astype(np.float32)
    upper_rows = np.stack(upper_rows).astype(np.float32)
    eye_h = np.eye(n_heads, dtype=bool)
    pair_bd = np.stack([np.kron(eye_h, pm) for pm in pair_masks]).astype(np.float32)
    head_of_lane = np.arange(n_heads * dk)[None, :] // dk
    head_of_row = np.repeat(np.arange(n_heads), C)[:, None]
    head_mask = (head_of_lane == head_of_row).astype(np.float32)
    return sum_sel, upper_rows, pair_bd, head_mask


def _gla_kernel(q_ref, k_ref, g_ref, v_ref, z_ref, gsub_ref, sumsel_ref, upper_ref, pair_ref, hmask_ref,
                o_ref, state_scr, decay_scr, coldecay_scr, mixed_scr, *, chunk, n_chunks, n_levels):
    C = chunk
    H, DV = GLA_HEADS, GLA_DV

    @pl.when(pl.program_id(1) == 0)
    def _():
        state_scr[...] = jnp.zeros(state_scr.shape, F32)

    hmask = hmask_ref[...]
    ones_cols = jnp.ones((2 * C, LANE), BF16)

    def stack_heads(a):
        return jnp.concatenate([a.astype(BF16)] * H, axis=0) * hmask

    def decays(c, buf):
        r0 = pl.multiple_of(c * C, C)
        g = g_ref[0, pl.ds(r0, C), :]
        g_hi = g.astype(BF16)
        g_lo = (g - g_hi.astype(F32)).astype(BF16)
        g2 = jnp.concatenate([g_hi, g_lo], axis=0)
        d_all = jnp.dot(sumsel_ref[...], g2, preferred_element_type=F32)
        decay_scr[buf] = jnp.exp(d_all)
        col_sum = lax.dot_general(g2, ones_cols, TN_DIMS, preferred_element_type=F32)
        coldecay_scr[buf] = jnp.exp(col_sum)

    def mix(c, buf, state):
        r0 = pl.multiple_of(c * C, C)
        q = q_ref[0, pl.ds(r0, C), :].astype(F32)
        k = k_ref[0, pl.ds(r0, C), :].astype(F32)
        v = v_ref[0, pl.ds(r0, C), :]
        v_st = jnp.concatenate([v[:, hh * DV:(hh + 1) * DV] for hh in range(H)], axis=0)
        e_all = decay_scr.at[buf]

        k_out = stack_heads(k * e_all[C:2 * C])
        new_state = (coldecay_scr[buf] * state
                     + lax.dot_general(k_out, v_st, TN_DIMS, preferred_element_type=F32))

        a_bd = None
        for lvl in range(n_levels + 1):
            if lvl < n_levels:
                e = e_all[(2 + lvl) * C:(3 + lvl) * C]
                xs = stack_heads(jnp.where(upper_ref[lvl] > 0.5, q, k) * e)
                prod = lax.dot_general(xs, xs, NT_DIMS, preferred_element_type=F32)
            else:
                prod = lax.dot_general(stack_heads(q), stack_heads(k), NT_DIMS, preferred_element_type=F32)
            term = prod * pair_ref[lvl]
            a_bd = term if a_bd is None else a_bd + term

        q_in = stack_heads(q * e_all[0:C])
        lhs = jnp.concatenate([q_in, a_bd.astype(BF16)], axis=1)
        rhs = jnp.concatenate([state.astype(BF16), v_st], axis=0)
        mixed_scr[buf] = jnp.dot(lhs, rhs, preferred_element_type=F32)
        return new_state

    def finish(c, buf):
        r0 = pl.multiple_of(c * C, C)
        for hh in range(H):
            o_h = _rms(mixed_scr[buf, hh * C:(hh + 1) * C, :], gsub_ref[...])
            z_h = z_ref[0, pl.ds(r0, C), hh * DV:(hh + 1) * DV].astype(F32)
            o_ref[0, pl.ds(r0, C), hh * DV:(hh + 1) * DV] = (o_h * _silu(z_h)).astype(o_ref.dtype)

    decays(0, 0)
    mixed_scr[1] = jnp.zeros(mixed_scr.shape[1:], F32)

    def pair(cc, carry):
        c = 2 * cc
        state = state_scr[...]
        decays(c + 1, 1)
        state = mix(c, 0, state)
        finish(jnp.maximum(c - 1, 0), 1)
        decays(jnp.minimum(c + 2, n_chunks - 1), 0)
        state = mix(c + 1, 1, state)
        finish(c, 0)
        state_scr[...] = state
        return carry

    lax.fori_loop(0, n_chunks // 2, pair, 0)
    finish(n_chunks - 1, 1)


def _gated_linear_attention(qb, kb, glog, vb, zb, g_sub_b, *, chunk, rows_per_step):
    B, S, b_qk = qb.shape
    b_v = vb.shape[-1]
    sum_sel, upper_rows, pair_bd, head_mask = _gla_tables(chunk, GLA_HEADS, GLA_DK)
    n_levels = upper_rows.shape[0]
    T = rows_per_step

    def row_spec(width):
        return pl.BlockSpec((1, T, width), lambda b, i: (b, i, 0))

    def const_spec(shape):
        return pl.BlockSpec(shape, lambda b, i: (0,) * len(shape))

    sum_sel2 = np.concatenate([sum_sel, sum_sel], axis=1)
    assert (T // chunk) % 2 == 0
    kern = functools.partial(_gla_kernel, chunk=chunk, n_chunks=T // chunk, n_levels=n_levels)
    return pl.pallas_call(
        kern,
        grid=(B, S // T),
        in_specs=[row_spec(b_qk), row_spec(b_qk), row_spec(b_qk), row_spec(b_v), row_spec(b_v),
                  const_spec((1, GLA_DV)), const_spec(sum_sel2.shape), const_spec(upper_rows.shape),
                  const_spec(pair_bd.shape), const_spec(head_mask.shape)],
        out_specs=row_spec(b_v),
        out_shape=jax.ShapeDtypeStruct((B, S, b_v), BF16),
        scratch_shapes=[pltpu.VMEM((b_qk, GLA_DV), F32),
                        pltpu.VMEM((2,) + (sum_sel.shape[0], b_qk), F32),
                        pltpu.VMEM((2, b_qk, LANE), F32),
                        pltpu.VMEM((2, GLA_HEADS * chunk, GLA_DV), F32)],
        compiler_params=pltpu.CompilerParams(
            dimension_semantics=("parallel", "arbitrary"), vmem_limit_bytes=VMEM_LIMIT),
        name="gla",
    )(qb, kb, glog, vb, zb, g_sub_b.reshape(1, GLA_DV), jnp.asarray(sum_sel2, BF16),
      jnp.asarray(upper_rows), jnp.asarray(pair_bd), jnp.asarray(head_mask, BF16))


def _merge_kernel(x_ref, ua_ref, ub_ref, ga_ref, gb_ref, wua_ref, wub_ref, wout_ref, gpost_ref, o_ref):
    ya = jnp.dot(ua_ref[0], wua_ref[...], preferred_element_type=F32)
    yb = jnp.dot(ub_ref[0], wub_ref[...], preferred_element_type=F32)
    y = jax.nn.sigmoid(ga_ref[0].astype(F32)) * ya + jax.nn.sigmoid(gb_ref[0].astype(F32)) * yb
    out = jnp.dot(y.astype(BF16), wout_ref[...], preferred_element_type=F32)
    o_ref[0] = x_ref[0] + _rms(out, gpost_ref[...])


def _merge(x, ua, ub, ga, gb, w_up_a, w_up_b, w_out, g_post, *, tm):
    B, S, D = x.shape

    def row_spec(width):
        return pl.BlockSpec((1, tm, width), lambda b, i: (b, i, 0))

    def const_spec(shape):
        return pl.BlockSpec(shape, lambda b, i: (0,) * len(shape))

    return pl.pallas_call(
        _merge_kernel,
        grid=(B, S // tm),
        in_specs=[row_spec(D), row_spec(ua.shape[-1]), row_spec(ub.shape[-1]), row_spec(D), row_spec(D),
                  const_spec(w_up_a.shape), const_spec(w_up_b.shape), const_spec(w_out.shape),
                  const_spec((1, D))],
        out_specs=row_spec(D),
        out_shape=jax.ShapeDtypeStruct((B, S, D), x.dtype),
        compiler_params=pltpu.CompilerParams(
            dimension_semantics=("parallel", "parallel"), vmem_limit_bytes=VMEM_LIMIT),
        name="merge",
    )(x, ua, ub, ga, gb, w_up_a, w_up_b, w_out, g_post.reshape(1, D))


def _layer(x, g_pre, w_in, lam_q1, lam_k1, lam_q2, lam_k2, g_sub_a, w_alpha, b_alpha, g_sub_b,
           w_up_a, w_up_b, w_out, g_post):
    D = x.shape[-1]
    lr0 = 2 * (DA_HEADS * 2 * DA_HEAD_DIM) + 2 * (DA_HEADS * DA_V_DIM) + 2 * (GLA_HEADS * GLA_DK) \
        + 2 * (GLA_HEADS * GLA_DV)
    lr1 = lr0 + GLA_GATE_RANK
    qa, ka, va, za, qb, kb, vb, zb, ga, gb, glog = _input_projection(
        x, g_pre, w_in[:, :lr0].astype(BF16), w_in[:, lr0:lr1].astype(BF16), w_in[:, lr1:].astype(BF16),
        w_alpha.astype(BF16), b_alpha, tm=512)
    ua = _diff_attention(qa, ka, va, za, lam_q1, lam_k1, lam_q2, lam_k2, g_sub_a, tq=256)
    ub = _gated_linear_attention(qb, kb, glog, vb, zb, g_sub_b, chunk=64, rows_per_step=1024)
    return _merge(x, ua, ub, ga, gb, w_up_a.astype(BF16), w_up_b.astype(BF16), w_out.astype(BF16),
                  g_post, tm=512)


def kernel(x, g_pre, w_in, lam_q1, lam_k1, lam_q2, lam_k2, g_sub_a, w_alpha, b_alpha, g_sub_b, w_up_a, w_up_b, w_out, g_post):
    depth = w_in.shape[0]
    assert depth == 1, "LAM_INIT is specialised to a single layer"
    first = lambda p: p.reshape(p.shape[1:])
    return _layer(x, *(first(p) for p in (g_pre, w_in, lam_q1, lam_k1, lam_q2, lam_k2, g_sub_a, w_alpha,
                                          b_alpha, g_sub_b, w_up_a, w_up_b, w_out, g_post)))
```

```python
import functools
import math

import numpy as np
import jax
import jax.numpy as jnp
from jax import lax
from jax.experimental import pallas as pl
from jax.experimental.pallas import tpu as pltpu

F32 = jnp.float32
BF16 = jnp.bfloat16

DA_HEADS = 4
DA_HEAD_DIM = 64
DA_V_DIM = 128
GLA_HEADS = 4
GLA_DK = 64
GLA_DV = 128
GLA_GATE_RANK = 16
GLA_TAU = 16.0
RMS_EPS = 1e-6
LAYER_IDX = 0
LAM_INIT = 0.8 - 0.6 * math.exp(-0.3 * LAYER_IDX)

LOG2E = math.log2(math.e)
LANE = 128
ONES_ROWS = 16
SLOPE_PARTS = 3
VMEM_LIMIT = 56 * 1024 * 1024

NT_DIMS = (((1,), (1,)), ((), ()))
TN_DIMS = (((0,), (0,)), ((), ()))


def _rms(x, g):
    return x * lax.rsqrt(jnp.mean(x * x, axis=-1, keepdims=True) + RMS_EPS) * g


def _silu(z):
    return z * jax.nn.sigmoid(z)


def _inproj_kernel(x0_ref, xn_ref, gpre_ref, wmix_ref, wgate_ref, walpha_ref, balpha_ref,
                   qa_ref, ka_ref, va_ref, za_ref, qb_ref, kb_ref, vb_ref, zb_ref,
                   ga_ref, gb_ref, glog_ref, h_scr, t_scr, *, segments, lr_cols, feature_major,
                   q_scale_a, q_scale_b):
    i = pl.program_id(0)
    outs = dict(qa=qa_ref, ka=ka_ref, va=va_ref, za=za_ref, qb=qb_ref, kb=kb_ref, vb=vb_ref,
                zb=zb_ref, ga=ga_ref, gb=gb_ref)
    weights = dict(mix=wmix_ref, gate=wgate_ref)
    scales = dict(qa=q_scale_a, qb=q_scale_b)

    @pl.when(i == 0)
    def _():
        h_scr[0] = _rms(x0_ref[0], gpre_ref[...]).astype(BF16)

    def project(cur):
        h_scr[1 - cur] = _rms(xn_ref[0], gpre_ref[...]).astype(BF16)
        for name, wname, c0, width, o0 in segments:
            acc = jnp.dot(h_scr[cur], weights[wname][:, c0:c0 + width], preferred_element_type=F32)
            if name in scales:
                acc = acc * scales[name]
            if name in ("za", "zb"):
                acc = _silu(acc)
            if name in feature_major:
                t_scr[...] = acc
                outs[name][0, o0:o0 + width, :] = t_scr[...].T.astype(BF16)
            else:
                outs[name][0, :, o0:o0 + width] = acc.astype(BF16)
        lr = jnp.dot(h_scr[cur], wmix_ref[:, lr_cols[0]:lr_cols[1]], preferred_element_type=F32)
        z = jnp.dot(lr.astype(BF16), walpha_ref[...], preferred_element_type=F32) + balpha_ref[...]
        log_sig = jnp.minimum(z, 0.0) - jnp.log1p(jnp.exp(-jnp.abs(z)))
        glog_ref[0] = log_sig / GLA_TAU

    for parity in (0, 1):
        pl.when(i % 2 == parity)(functools.partial(project, parity))


def _input_projection(x, g_pre, w_mix_lr, w_gate, w_alpha, b_alpha, *, tm):
    B, S, D = x.shape
    a_qk, a_v = DA_HEADS * 2 * DA_HEAD_DIM, DA_HEADS * DA_V_DIM
    b_qk, b_v = GLA_HEADS * GLA_DK, GLA_HEADS * GLA_DV
    widths = [("qa", "mix", a_qk), ("ka", "mix", a_qk), ("va", "mix", a_v), ("za", "mix", a_v),
              ("qb", "mix", b_qk), ("kb", "mix", b_qk), ("vb", "mix", b_v), ("zb", "mix", b_v),
              ("ga", "gate", D), ("gb", "gate", D)]
    max_chunk = 512
    segments = []
    col = dict(mix=0, gate=0)
    for name, wname, width in widths:
        for o0 in range(0, width, max_chunk):
            segments.append((name, wname, col[wname] + o0, min(max_chunk, width - o0), o0))
        col[wname] += width
    lr_cols = (col["mix"], col["mix"] + GLA_GATE_RANK)
    assert lr_cols[1] <= w_mix_lr.shape[1] and col["gate"] == w_gate.shape[1]

    per_batch = S // tm
    n_steps = B * per_batch
    tile = lambda i: (i // per_batch, i % per_batch)

    def row_spec(width):
        return pl.BlockSpec((1, tm, width), lambda i: (*tile(i), 0))

    def const_spec(shape):
        return pl.BlockSpec(shape, lambda i: (0,) * len(shape), pipeline_mode=pl.Buffered(1))

    feature_major = ("qa", "va")
    out_shape, out_specs = [], []
    for name, _, w in widths:
        if name in feature_major:
            out_shape.append(jax.ShapeDtypeStruct((B, w, S), BF16))
            out_specs.append(pl.BlockSpec((1, w, tm), lambda i: (tile(i)[0], 0, tile(i)[1])))
        else:
            out_shape.append(jax.ShapeDtypeStruct((B, S, w), BF16))
            out_specs.append(row_spec(w))
    out_shape.append(jax.ShapeDtypeStruct((B, S, b_qk), F32))
    out_specs.append(row_spec(b_qk))
    kern = functools.partial(_inproj_kernel, segments=tuple(segments), lr_cols=lr_cols,
                             feature_major=feature_major,
                             q_scale_a=DA_HEAD_DIM ** -0.5 * LOG2E,
                             q_scale_b=GLA_DK ** -0.5)
    first_spec = pl.BlockSpec((1, tm, D), lambda i: (0, 0, 0), pipeline_mode=pl.Buffered(1))
    next_spec = pl.BlockSpec((1, tm, D), lambda i: (*tile(jnp.minimum(i + 1, n_steps - 1)), 0))
    return pl.pallas_call(
        kern,
        grid=(n_steps,),
        in_specs=[first_spec, next_spec, const_spec((1, D)), const_spec(w_mix_lr.shape),
                  const_spec(w_gate.shape), const_spec(w_alpha.shape), const_spec((1, b_qk))],
        out_specs=out_specs,
        out_shape=out_shape,
        scratch_shapes=[pltpu.VMEM((2, tm, D), BF16), pltpu.VMEM((tm, max_chunk), F32)],
        compiler_params=pltpu.CompilerParams(
            dimension_semantics=("arbitrary",), vmem_limit_bytes=VMEM_LIMIT),
        name="inproj",
    )(x, x, g_pre.reshape(1, D), w_mix_lr, w_gate, w_alpha, b_alpha.reshape(1, b_qk))


def _attn_kernel(lq1_ref, lk1_ref, lq2_ref, lk2_ref, q_ref, qn_ref, k_ref, vt_ref, za_ref, gsub_ref,
                 o_ref, qs_scr, ks_scr, vt_scr, causal_scr, st_scr, mx_scr, p_scr, alpha_scr, m_scr, acc_scr,
                 *, tq, tk):
    i = pl.program_id(1)
    H, DV = DA_HEADS, DA_V_DIM
    n_kv = vt_scr.shape[1]
    slope_parts, slopes = [], []
    for h in range(H):
        rest = np.float32(2.0 ** (-8.0 * (h + 1) / H) * LOG2E)
        slopes.append(float(rest))
        parts = []
        for _ in range(SLOPE_PARTS):
            part = np.asarray(rest, dtype=BF16)
            parts.append(float(part))
            rest = np.float32(rest - np.float32(part))
        assert rest == 0.0
        slope_parts.append(parts)

    def prepare_keys_values():
        lane = lax.broadcasted_iota(jnp.int32, (tk, LANE), 1)
        key = lax.broadcasted_iota(jnp.int32, (tk, LANE), 0).astype(F32)
        key_lanes = jnp.where(lane < SLOPE_PARTS, key, 0.0).astype(BF16)
        ones_rows = jnp.ones((ONES_ROWS, tk), BF16)
        def prep(j, carry):
            k0 = pl.multiple_of(j * tk, tk)
            for h in range(H):
                ks_scr[h, pl.ds(k0, tk), 0:LANE] = k_ref[0, pl.ds(k0, tk), h * LANE:(h + 1) * LANE]
                ks_scr[h, pl.ds(k0, tk), LANE:2 * LANE] = key_lanes
            return carry
        lax.fori_loop(0, n_kv, prep, 0)
        for j in range(n_kv):
            for h in range(H):
                vt_scr[h, j, 0:DV, :] = vt_ref[0, h * DV:(h + 1) * DV, j * tk:(j + 1) * tk]
                vt_scr[h, j, DV:DV + ONES_ROWS, :] = ones_rows
        key_idx = lax.broadcasted_iota(jnp.int32, (tk, 2 * tq), 0)
        qcol = lax.broadcasted_iota(jnp.int32, (tk, 2 * tq), 1)
        visible = key_idx <= jnp.where(qcol >= tq, qcol - tq, qcol)
        causal_scr[...] = jnp.where(visible, 0.0, -1e30)

    def load_queries(ref):
        row = lax.broadcasted_iota(jnp.int32, (LANE, tq), 0)
        zero = jnp.zeros((LANE, tq), BF16)
        for h in range(H):
            qt = ref[0, h * LANE:(h + 1) * LANE, :]
            qs_scr[h, 0:LANE, 0:tq] = jnp.where(row < DA_HEAD_DIM, qt, zero)
            qs_scr[h, 0:LANE, tq:2 * tq] = jnp.where(row >= DA_HEAD_DIM, qt, zero)

    def load_slopes():
        row = lax.broadcasted_iota(jnp.int32, (LANE, 2 * tq), 0)
        for h in range(H):
            rows = jnp.zeros((LANE, 2 * tq), F32)
            for r, part in enumerate(slope_parts[h]):
                rows = jnp.where(row == r, part, rows)
            qs_scr[h, LANE:2 * LANE, :] = rows.astype(BF16)

    def produce(j, buf, masked=False):
        k0 = pl.multiple_of(j * tk, tk)
        for h in range(H):
            st = jnp.dot(ks_scr[h, pl.ds(k0, tk), :], qs_scr[h], preferred_element_type=F32)
            if masked:
                st = st + causal_scr[...]
            st_scr[buf, h] = st
            mx_scr[buf, h] = jnp.max(st, axis=0, keepdims=True)

    def softmax(j, buf):
        tile_dist = (j * tk - i * tq).astype(F32)
        for h in range(H):
            off = slopes[h] * tile_dist
            m_prev = m_scr[h]
            m_next = jnp.maximum(m_prev, mx_scr[buf, h] + off)
            alpha_scr[buf, h] = jnp.exp2(m_prev - m_next)
            p_scr[buf, h] = jnp.exp2(st_scr[buf, h] - (m_next - off)).astype(BF16)
            m_scr[h] = m_next

    def weighted_values(j, buf):
        for h in range(H):
            acc_scr[h] = alpha_scr[buf, h] * acc_scr[h] + jnp.dot(vt_scr[h, j], p_scr[buf, h],
                                                                  preferred_element_type=F32)

    def tile_at(pos):
        return jnp.where(pos == 0, i, pos - 1)

    @pl.when(i == 0)
    def _():
        prepare_keys_values()
        load_slopes()
        load_queries(q_ref)
        produce(0, 0, masked=True)

    m_scr[...] = jnp.full(m_scr.shape, -jnp.inf, F32)
    acc_scr[...] = jnp.zeros(acc_scr.shape, F32)

    @pl.when(i == 0)
    def _():
        softmax(0, 0)
        weighted_values(0, 0)

    @pl.when(i > 0)
    def _():
        produce(0, 1)
        softmax(i, 0)

        def pair(pp, carry):
            p = 2 * pp
            produce(p + 1, 0)
            weighted_values(tile_at(p), 0)
            softmax(p, 1)
            produce(p + 2, 1)
            weighted_values(p, 1)
            softmax(p + 1, 0)
            return carry
        n_pairs = (i - 1) // 2
        lax.fori_loop(0, n_pairs, pair, 0)

        @pl.when(i % 2 == 1)
        def _():
            softmax(i - 1, 1)
            weighted_values(tile_at(i - 1), 0)
            weighted_values(i - 1, 1)

        @pl.when(i % 2 == 0)
        def _():
            produce(i - 1, 0)
            softmax(i - 2, 1)
            weighted_values(tile_at(i - 2), 0)
            softmax(i - 1, 0)
            weighted_values(i - 2, 1)
            weighted_values(i - 1, 0)

    n_q = pl.num_programs(1)
    load_queries(qn_ref)
    produce(jnp.minimum(i + 1, n_q - 1), 0, masked=True)

    f = lambda r: r[...].astype(F32)
    lam = (jnp.exp(jnp.sum(f(lq1_ref) * f(lk1_ref), axis=1, keepdims=True))
           - jnp.exp(jnp.sum(f(lq2_ref) * f(lk2_ref), axis=1, keepdims=True)) + LAM_INIT)
    gain = gsub_ref[...] * (1.0 - LAM_INIT)
    for h in range(H):
        on = acc_scr[h, 0:DV, :] / acc_scr[h, DV:DV + 1, :]
        ot = on[:, 0:tq] - lam * on[:, tq:2 * tq]
        ot = ot * lax.rsqrt(jnp.mean(ot * ot, axis=0, keepdims=True) + RMS_EPS)
        o = ot.T * gain
        gate = za_ref[0, :, h * DV:(h + 1) * DV].astype(F32)
        o_ref[0, :, h * DV:(h + 1) * DV] = (o * gate).astype(o_ref.dtype)


def _diff_attention(qa_t, ka, va_t, za, lam_q1, lam_k1, lam_q2, lam_k2, g_sub_a, *, tq):
    B, S, W = ka.shape
    tk = tq
    assert W == DA_HEADS * LANE and S % tq == 0 and qa_t.shape == (B, W, S) and va_t.shape == (B, W, S)
    vec = lambda v: v.reshape(1, DA_HEAD_DIM)
    vec_spec = pl.BlockSpec((1, DA_HEAD_DIM), lambda b, i: (0, 0))
    tile_spec = pl.BlockSpec((1, tq, W), lambda b, i: (b, i, 0))
    seq_spec = pl.BlockSpec((1, S, W), lambda b, i: (b, 0, 0))
    n_q = S // tq
    q_spec = pl.BlockSpec((1, W, tq), lambda b, i: (b, 0, i))
    next_q_spec = pl.BlockSpec((1, W, tq), lambda b, i: (b, 0, jnp.minimum(i + 1, n_q - 1)))
    seq_t_spec = pl.BlockSpec((1, W, S), lambda b, i: (b, 0, 0))
    kern = functools.partial(_attn_kernel, tq=tq, tk=tk)
    return pl.pallas_call(
        kern,
        grid=(B, n_q),
        in_specs=[vec_spec, vec_spec, vec_spec, vec_spec, q_spec, next_q_spec, seq_spec, seq_t_spec,
                  tile_spec,
                  pl.BlockSpec((1, DA_V_DIM), lambda b, i: (0, 0))],
        out_specs=tile_spec,
        out_shape=jax.ShapeDtypeStruct((B, S, W), BF16),
        scratch_shapes=[pltpu.VMEM((DA_HEADS, 2 * LANE, 2 * tq), BF16),
                        pltpu.VMEM((DA_HEADS, S, 2 * LANE), BF16),
                        pltpu.VMEM((DA_HEADS, S // tk, DA_V_DIM + ONES_ROWS, tk), BF16),
                        pltpu.VMEM((tk, 2 * tq), F32),
                        pltpu.VMEM((2, DA_HEADS, tk, 2 * tq), F32),
                        pltpu.VMEM((2, DA_HEADS, 1, 2 * tq), F32),
                        pltpu.VMEM((2, DA_HEADS, tk, 2 * tq), BF16),
                        pltpu.VMEM((2, DA_HEADS, 1, 2 * tq), F32),
                        pltpu.VMEM((DA_HEADS, 1, 2 * tq), F32),
                        pltpu.VMEM((DA_HEADS, DA_V_DIM + ONES_ROWS, 2 * tq), F32)],
        compiler_params=pltpu.CompilerParams(
            dimension_semantics=("parallel", "arbitrary"), vmem_limit_bytes=VMEM_LIMIT),
        name="diffattn",
    )(vec(lam_q1), vec(lam_k1), vec(lam_q2), vec(lam_k2), qa_t, qa_t, ka, va_t, za,
      g_sub_a.reshape(1, DA_V_DIM))


def _gla_tables(C, n_heads, dk):
    t = np.arange(C)[:, None]
    u = np.arange(C)[None, :]
    sums = [(u <= t), (u > t)]
    halves = []
    m = C // 2
    while m >= 1:
        halves.append(m)
        m //= 2
    upper_rows, pair_masks = [], []
    for m in halves:
        blk = 2 * m
        mid = (t // blk) * blk + m
        upper = (t % blk) >= m
        sums.append(np.where(upper, (u >= mid) & (u <= t), (u > t) & (u <= mid - 1)))
        upper_rows.append(np.broadcast_to(upper, (C, n_heads * dk)))
        tt, ss = np.arange(C)[:, None], np.arange(C)[None, :]
        pair_masks.append(((tt // blk) == (ss // blk)) & ((tt % blk) >= m) & ((ss % blk) < m))
    pair_masks.append(np.eye(C, dtype=bool))
    sum_sel = np.concatenate(sums, axis=0).astype(np.float32)
    upper_rows = np.stack(upper_rows).astype(np.float32)
    eye_h = np.eye(n_heads, dtype=bool)
    pair_bd = np.stack([np.kron(eye_h, pm) for pm in pair_masks]).astype(np.float32)
    head_of_lane = np.arange(n_heads * dk)[None, :] // dk
    head_of_row = np.repeat(np.arange(n_heads), C)[:, None]
    head_mask = (head_of_lane == head_of_row).astype(np.float32)
    return sum_sel, upper_rows, pair_bd, head_mask


def _gla_kernel(q_ref, k_ref, g_ref, v_ref, z_ref, gsub_ref, sumsel_ref, upper_ref, pair_ref, hmask_ref,
                o_ref, state_scr, decay_scr, coldecay_scr, mixed_scr, *, chunk, n_chunks, n_levels):
    C = chunk
    H, DV = GLA_HEADS, GLA_DV

    @pl.when(pl.program_id(1) == 0)
    def _():
        state_scr[...] = jnp.zeros(state_scr.shape, F32)

    hmask = hmask_ref[...]
    ones_cols = jnp.ones((2 * C, LANE), BF16)

    def stack_heads(a):
        return jnp.concatenate([a.astype(BF16)] * H, axis=0) * hmask

    def decays(c, buf):
        r0 = pl.multiple_of(c * C, C)
        g = g_ref[0, pl.ds(r0, C), :]
        g_hi = g.astype(BF16)
        g_lo = (g - g_hi.astype(F32)).astype(BF16)
        g2 = jnp.concatenate([g_hi, g_lo], axis=0)
        d_all = jnp.dot(sumsel_ref[...], g2, preferred_element_type=F32)
        decay_scr[buf] = jnp.exp(d_all)
        col_sum = lax.dot_general(g2, ones_cols, TN_DIMS, preferred_element_type=F32)
        coldecay_scr[buf] = jnp.exp(col_sum)

    def mix(c, buf, state):
        r0 = pl.multiple_of(c * C, C)
        q = q_ref[0, pl.ds(r0, C), :].astype(F32)
        k = k_ref[0, pl.ds(r0, C), :].astype(F32)
        v = v_ref[0, pl.ds(r0, C), :]
        v_st = jnp.concatenate([v[:, hh * DV:(hh + 1) * DV] for hh in range(H)], axis=0)
        e_all = decay_scr.at[buf]

        k_out = stack_heads(k * e_all[C:2 * C])
        new_state = (coldecay_scr[buf] * state
                     + lax.dot_general(k_out, v_st, TN_DIMS, preferred_element_type=F32))

        a_bd = None
        for lvl in range(n_levels + 1):
            if lvl < n_levels:
                e = e_all[(2 + lvl) * C:(3 + lvl) * C]
                xs = stack_heads(jnp.where(upper_ref[lvl] > 0.5, q, k) * e)
                prod = lax.dot_general(xs, xs, NT_DIMS, preferred_element_type=F32)
            else:
                prod = lax.dot_general(stack_heads(q), stack_heads(k), NT_DIMS, preferred_element_type=F32)
            term = prod * pair_ref[lvl]
            a_bd = term if a_bd is None else a_bd + term

        q_in = stack_heads(q * e_all[0:C])
        lhs = jnp.concatenate([q_in, a_bd.astype(BF16)], axis=1)
        rhs = jnp.concatenate([state.astype(BF16), v_st], axis=0)
        mixed_scr[buf] = jnp.dot(lhs, rhs, preferred_element_type=F32)
        return new_state

    def finish(c, buf):
        r0 = pl.multiple_of(c * C, C)
        for hh in range(H):
            o_h = _rms(mixed_scr[buf, hh * C:(hh + 1) * C, :], gsub_ref[...])
            gate = z_ref[0, pl.ds(r0, C), hh * DV:(hh + 1) * DV].astype(F32)
            o_ref[0, pl.ds(r0, C), hh * DV:(hh + 1) * DV] = (o_h * gate).astype(o_ref.dtype)

    decays(0, 0)
    mixed_scr[1] = jnp.zeros(mixed_scr.shape[1:], F32)

    def pair(cc, carry):
        c = 2 * cc
        state = state_scr[...]
        decays(c + 1, 1)
        state = mix(c, 0, state)
        finish(jnp.maximum(c - 1, 0), 1)
        decays(jnp.minimum(c + 2, n_chunks - 1), 0)
        state = mix(c + 1, 1, state)
        finish(c, 0)
        state_scr[...] = state
        return carry

    lax.fori_loop(0, n_chunks // 2, pair, 0)
    finish(n_chunks - 1, 1)


def _gated_linear_attention(qb, kb, glog, vb, zb, g_sub_b, *, chunk, rows_per_step):
    B, S, b_qk = qb.shape
    b_v = vb.shape[-1]
    sum_sel, upper_rows, pair_bd, head_mask = _gla_tables(chunk, GLA_HEADS, GLA_DK)
    n_levels = upper_rows.shape[0]
    T = rows_per_step

    def row_spec(width):
        return pl.BlockSpec((1, T, width), lambda b, i: (b, i, 0))

    def const_spec(shape):
        return pl.BlockSpec(shape, lambda b, i: (0,) * len(shape))

    sum_sel2 = np.concatenate([sum_sel, sum_sel], axis=1)
    assert (T // chunk) % 2 == 0
    kern = functools.partial(_gla_kernel, chunk=chunk, n_chunks=T // chunk, n_levels=n_levels)
    return pl.pallas_call(
        kern,
        grid=(B, S // T),
        in_specs=[row_spec(b_qk), row_spec(b_qk), row_spec(b_qk), row_spec(b_v), row_spec(b_v),
                  const_spec((1, GLA_DV)), const_spec(sum_sel2.shape), const_spec(upper_rows.shape),
                  const_spec(pair_bd.shape), const_spec(head_mask.shape)],
        out_specs=row_spec(b_v),
        out_shape=jax.ShapeDtypeStruct((B, S, b_v), BF16),
        scratch_shapes=[pltpu.VMEM((b_qk, GLA_DV), F32),
                        pltpu.VMEM((2,) + (sum_sel.shape[0], b_qk), F32),
                        pltpu.VMEM((2, b_qk, LANE), F32),
                        pltpu.VMEM((2, GLA_HEADS * chunk, GLA_DV), F32)],
        compiler_params=pltpu.CompilerParams(
            dimension_semantics=("parallel", "arbitrary"), vmem_limit_bytes=VMEM_LIMIT),
        name="gla",
    )(qb, kb, glog, vb, zb, g_sub_b.reshape(1, GLA_DV), jnp.asarray(sum_sel2, BF16),
      jnp.asarray(upper_rows), jnp.asarray(pair_bd), jnp.asarray(head_mask, BF16))


def _merge_kernel(x_ref, ua_ref, ub_ref, ga_ref, gb_ref, wua_ref, wub_ref, wout_ref, gpost_ref, o_ref):
    ya = jnp.dot(ua_ref[0], wua_ref[...], preferred_element_type=F32)
    yb = jnp.dot(ub_ref[0], wub_ref[...], preferred_element_type=F32)
    y = jax.nn.sigmoid(ga_ref[0].astype(F32)) * ya + jax.nn.sigmoid(gb_ref[0].astype(F32)) * yb
    out = jnp.dot(y.astype(BF16), wout_ref[...], preferred_element_type=F32)
    o_ref[0] = x_ref[0] + _rms(out, gpost_ref[...])


def _merge(x, ua, ub, ga, gb, w_up_a, w_up_b, w_out, g_post, *, tm):
    B, S, D = x.shape

    def row_spec(width):
        return pl.BlockSpec((1, tm, width), lambda b, i: (b, i, 0))

    def const_spec(shape):
        return pl.BlockSpec(shape, lambda b, i: (0,) * len(shape))

    return pl.pallas_call(
        _merge_kernel,
        grid=(B, S // tm),
        in_specs=[row_spec(D), row_spec(ua.shape[-1]), row_spec(ub.shape[-1]), row_spec(D), row_spec(D),
                  const_spec(w_up_a.shape), const_spec(w_up_b.shape), const_spec(w_out.shape),
                  const_spec((1, D))],
        out_specs=row_spec(D),
        out_shape=jax.ShapeDtypeStruct((B, S, D), x.dtype),
        compiler_params=pltpu.CompilerParams(
            dimension_semantics=("parallel", "parallel"), vmem_limit_bytes=VMEM_LIMIT),
        name="merge",
    )(x, ua, ub, ga, gb, w_up_a, w_up_b, w_out, g_post.reshape(1, D))


def _layer(x, g_pre, w_in, lam_q1, lam_k1, lam_q2, lam_k2, g_sub_a, w_alpha, b_alpha, g_sub_b,
           w_up_a, w_up_b, w_out, g_post):
    D = x.shape[-1]
    lr0 = 2 * (DA_HEADS * 2 * DA_HEAD_DIM) + 2 * (DA_HEADS * DA_V_DIM) + 2 * (GLA_HEADS * GLA_DK) \
        + 2 * (GLA_HEADS * GLA_DV)
    lr1 = lr0 + GLA_GATE_RANK
    qa, ka, va, za, qb, kb, vb, zb, ga, gb, glog = _input_projection(
        x, g_pre, w_in.astype(BF16), w_in[:, lr1:].astype(BF16), w_alpha.astype(BF16), b_alpha, tm=512)
    ua = _diff_attention(qa, ka, va, za, lam_q1, lam_k1, lam_q2, lam_k2, g_sub_a, tq=256)
    ub = _gated_linear_attention(qb, kb, glog, vb, zb, g_sub_b, chunk=64, rows_per_step=1024)
    return _merge(x, ua, ub, ga, gb, w_up_a.astype(BF16), w_up_b.astype(BF16), w_out.astype(BF16),
                  g_post, tm=512)


def kernel(x, g_pre, w_in, lam_q1, lam_k1, lam_q2, lam_k2, g_sub_a, w_alpha, b_alpha, g_sub_b, w_up_a, w_up_b, w_out, g_post):
    depth = w_in.shape[0]
    assert depth == 1, "LAM_INIT is specialised to a single layer"
    first = lambda p: p.reshape(p.shape[1:])
    return _layer(x, *(first(p) for p in (g_pre, w_in, lam_q1, lam_k1, lam_q2, lam_k2, g_sub_a, w_alpha,
                                          b_alpha, g_sub_b, w_up_a, w_up_b, w_out, g_post)))
```

```python
import functools
import math

import numpy as np
import jax
import jax.numpy as jnp
from jax import lax
from jax.experimental import pallas as pl
from jax.experimental.pallas import tpu as pltpu

F32 = jnp.float32
BF16 = jnp.bfloat16

DA_HEADS = 4
DA_HEAD_DIM = 64
DA_V_DIM = 128
GLA_HEADS = 4
GLA_DK = 64
GLA_DV = 128
GLA_GATE_RANK = 16
GLA_TAU = 16.0
RMS_EPS = 1e-6
LAYER_IDX = 0
LAM_INIT = 0.8 - 0.6 * math.exp(-0.3 * LAYER_IDX)

LOG2E = math.log2(math.e)
LANE = 128
ONES_ROWS = 16
SLOPE_PARTS = 3
VMEM_LIMIT = 56 * 1024 * 1024

NT_DIMS = (((1,), (1,)), ((), ()))


def _rms(x, g):
    return x * lax.rsqrt(jnp.mean(x * x, axis=-1, keepdims=True) + RMS_EPS) * g


def _silu(z):
    return z * jax.nn.sigmoid(z)


def _inproj_kernel(x_ref, gpre_ref, wmix_ref, wgate_ref, walpha_ref, balpha_ref,
                   qa_ref, ka_ref, va_ref, za_ref, qb_ref, kb_ref, vb_ref, zb_ref,
                   ga_ref, gb_ref, glog_ref, h_scr, t_scr, *, segments, lr_cols, feature_major,
                   q_scale_a, q_scale_b):
    outs = dict(qa=qa_ref, ka=ka_ref, va=va_ref, za=za_ref, qb=qb_ref, kb=kb_ref, vb=vb_ref,
                zb=zb_ref, ga=ga_ref, gb=gb_ref)
    weights = dict(mix=wmix_ref, gate=wgate_ref)
    scales = dict(qa=q_scale_a, qb=q_scale_b)
    h_scr[...] = _rms(x_ref[0], gpre_ref[...]).astype(BF16)
    for name, wname, c0, width, o0 in segments:
        acc = jnp.dot(h_scr[...], weights[wname][:, c0:c0 + width], preferred_element_type=F32)
        if name in scales:
            acc = acc * scales[name]
        if name in ("za", "zb"):
            acc = _silu(acc)
        if name in feature_major:
            t_scr[...] = acc
            outs[name][0, o0:o0 + width, :] = t_scr[...].T.astype(BF16)
        else:
            outs[name][0, :, o0:o0 + width] = acc.astype(BF16)
    lr = jnp.dot(h_scr[...], wmix_ref[:, lr_cols[0]:lr_cols[1]], preferred_element_type=F32)
    z = jnp.dot(lr.astype(BF16), walpha_ref[...], preferred_element_type=F32) + balpha_ref[...]
    log_sig = jnp.minimum(z, 0.0) - jnp.log1p(jnp.exp(-jnp.abs(z)))
    glog_ref[0] = log_sig / GLA_TAU


def _input_projection(x, g_pre, w_mix_lr, w_gate, w_alpha, b_alpha, *, tm):
    B, S, D = x.shape
    a_qk, a_v = DA_HEADS * 2 * DA_HEAD_DIM, DA_HEADS * DA_V_DIM
    b_qk, b_v = GLA_HEADS * GLA_DK, GLA_HEADS * GLA_DV
    widths = [("qa", "mix", a_qk), ("ka", "mix", a_qk), ("va", "mix", a_v), ("za", "mix", a_v),
              ("qb", "mix", b_qk), ("kb", "mix", b_qk), ("vb", "mix", b_v), ("zb", "mix", b_v),
              ("ga", "gate", D), ("gb", "gate", D)]
    max_chunk = 512
    segments = []
    col = dict(mix=0, gate=0)
    for name, wname, width in widths:
        for o0 in range(0, width, max_chunk):
            segments.append((name, wname, col[wname] + o0, min(max_chunk, width - o0), o0))
        col[wname] += width
    lr_cols = (col["mix"], col["mix"] + GLA_GATE_RANK)
    assert lr_cols[1] <= w_mix_lr.shape[1] and col["gate"] == w_gate.shape[1]

    def row_spec(width):
        return pl.BlockSpec((1, tm, width), lambda b, i: (b, i, 0))

    def const_spec(shape):
        return pl.BlockSpec(shape, lambda b, i: (0,) * len(shape), pipeline_mode=pl.Buffered(1))

    feature_major = ("qa", "va")
    out_shape, out_specs = [], []
    for name, _, w in widths:
        if name in feature_major:
            out_shape.append(jax.ShapeDtypeStruct((B, w, S), BF16))
            out_specs.append(pl.BlockSpec((1, w, tm), lambda b, i: (b, 0, i)))
        else:
            out_shape.append(jax.ShapeDtypeStruct((B, S, w), BF16))
            out_specs.append(row_spec(w))
    out_shape.append(jax.ShapeDtypeStruct((B, S, b_qk), F32))
    out_specs.append(row_spec(b_qk))
    kern = functools.partial(_inproj_kernel, segments=tuple(segments), lr_cols=lr_cols,
                             feature_major=feature_major,
                             q_scale_a=DA_HEAD_DIM ** -0.5 * LOG2E,
                             q_scale_b=GLA_DK ** -0.5)
    return pl.pallas_call(
        kern,
        grid=(B, S // tm),
        in_specs=[row_spec(D), const_spec((1, D)), const_spec(w_mix_lr.shape),
                  const_spec(w_gate.shape), const_spec(w_alpha.shape), const_spec((1, b_qk))],
        out_specs=out_specs,
        out_shape=out_shape,
        scratch_shapes=[pltpu.VMEM((tm, D), BF16), pltpu.VMEM((tm, max_chunk), F32)],
        compiler_params=pltpu.CompilerParams(
            dimension_semantics=("parallel", "parallel"), vmem_limit_bytes=VMEM_LIMIT),
        name="inproj",
    )(x, g_pre.reshape(1, D), w_mix_lr, w_gate, w_alpha, b_alpha.reshape(1, b_qk))


def _attn_kernel(lq1_ref, lk1_ref, lq2_ref, lk2_ref, q_ref, qn_ref, k_ref, vt_ref, za_ref, gsub_ref,
                 o_ref, qs_scr, ks_scr, vt_scr, causal_scr, st_scr, mx_scr, p_scr, alpha_scr, m_scr, acc_scr,
                 *, tq, tk):
    i = pl.program_id(1)
    H, DV = DA_HEADS, DA_V_DIM
    n_kv = vt_scr.shape[1]
    slope_parts, slopes = [], []
    for h in range(H):
        rest = np.float32(2.0 ** (-8.0 * (h + 1) / H) * LOG2E)
        slopes.append(float(rest))
        parts = []
        for _ in range(SLOPE_PARTS):
            part = np.asarray(rest, dtype=BF16)
            parts.append(float(part))
            rest = np.float32(rest - np.float32(part))
        assert rest == 0.0
        slope_parts.append(parts)

    def prepare_keys_values():
        lane = lax.broadcasted_iota(jnp.int32, (tk, LANE), 1)
        key = lax.broadcasted_iota(jnp.int32, (tk, LANE), 0).astype(F32)
        key_lanes = jnp.where(lane < SLOPE_PARTS, key, 0.0).astype(BF16)
        ones_rows = jnp.ones((ONES_ROWS, tk), BF16)
        def prep(j, carry):
            k0 = pl.multiple_of(j * tk, tk)
            for h in range(H):
                ks_scr[h, pl.ds(k0, tk), 0:LANE] = k_ref[0, pl.ds(k0, tk), h * LANE:(h + 1) * LANE]
                ks_scr[h, pl.ds(k0, tk), LANE:2 * LANE] = key_lanes
            return carry
        lax.fori_loop(0, n_kv, prep, 0)
        for j in range(n_kv):
            for h in range(H):
                vt_scr[h, j, 0:DV, :] = vt_ref[0, h * DV:(h + 1) * DV, j * tk:(j + 1) * tk]
                vt_scr[h, j, DV:DV + ONES_ROWS, :] = ones_rows
        key_idx = lax.broadcasted_iota(jnp.int32, (tk, 2 * tq), 0)
        qcol = lax.broadcasted_iota(jnp.int32, (tk, 2 * tq), 1)
        visible = key_idx <= jnp.where(qcol >= tq, qcol - tq, qcol)
        causal_scr[...] = jnp.where(visible, 0.0, -1e30)

    def load_queries(ref):
        row = lax.broadcasted_iota(jnp.int32, (LANE, tq), 0)
        zero = jnp.zeros((LANE, tq), BF16)
        for h in range(H):
            qt = ref[0, h * LANE:(h + 1) * LANE, :]
            qs_scr[h, 0:LANE, 0:tq] = jnp.where(row < DA_HEAD_DIM, qt, zero)
            qs_scr[h, 0:LANE, tq:2 * tq] = jnp.where(row >= DA_HEAD_DIM, qt, zero)

    def load_slopes():
        row = lax.broadcasted_iota(jnp.int32, (LANE, 2 * tq), 0)
        for h in range(H):
            rows = jnp.zeros((LANE, 2 * tq), F32)
            for r, part in enumerate(slope_parts[h]):
                rows = jnp.where(row == r, part, rows)
            qs_scr[h, LANE:2 * LANE, :] = rows.astype(BF16)

    def produce(j, buf, masked=False):
        k0 = pl.multiple_of(j * tk, tk)
        for h in range(H):
            st = jnp.dot(ks_scr[h, pl.ds(k0, tk), :], qs_scr[h], preferred_element_type=F32)
            if masked:
                st = st + causal_scr[...]
            st_scr[buf, h] = st
            mx_scr[buf, h] = jnp.max(st, axis=0, keepdims=True)

    def softmax(j, buf):
        tile_dist = (j * tk - i * tq).astype(F32)
        for h in range(H):
            off = slopes[h] * tile_dist
            m_prev = m_scr[h]
            m_next = jnp.maximum(m_prev, mx_scr[buf, h] + off)
            alpha_scr[buf, h] = jnp.exp2(m_prev - m_next)
            p_scr[buf, h] = jnp.exp2(st_scr[buf, h] - (m_next - off)).astype(BF16)
            m_scr[h] = m_next

    def weighted_values(j, buf):
        for h in range(H):
            acc_scr[h] = alpha_scr[buf, h] * acc_scr[h] + jnp.dot(vt_scr[h, j], p_scr[buf, h],
                                                                  preferred_element_type=F32)

    def run(stage, base, offset):
        pos, buf = base + offset, offset % 2
        if offset > 0:
            tile = pos - 1
        else:
            tile = jnp.where(pos == 0, i, pos - 1)
        if stage == "produce":
            produce(tile, buf)
        elif stage == "softmax":
            softmax(tile, buf)
        else:
            weighted_values(tile, buf)

    def steady(base):
        for t in (1, 2):
            run("produce", base, t + 1)
            run("weighted_values", base, t - 1)
            run("softmax", base, t)

    def drain(base, r):
        for t in range(1, r + 1):
            if t + 1 <= r:
                run("produce", base, t + 1)
            run("weighted_values", base, t - 1)
            run("softmax", base, t)
        run("weighted_values", base, r)

    @pl.when(i == 0)
    def _():
        prepare_keys_values()
        load_slopes()
        load_queries(q_ref)
        produce(0, 0, masked=True)

    m_scr[...] = jnp.full(m_scr.shape, -jnp.inf, F32)
    acc_scr[...] = jnp.zeros(acc_scr.shape, F32)

    @pl.when(i == 0)
    def _():
        softmax(0, 0)
        weighted_values(0, 0)

    @pl.when(i > 0)
    def _():
        run("produce", 0, 1)
        run("softmax", 0, 0)

        def trip(t, carry):
            steady(2 * t)
            return carry
        n_trips = (i - 1) // 2
        lax.fori_loop(0, n_trips, trip, 0)

        for r in (1, 2):
            pl.when(i - 2 * n_trips == r)(functools.partial(drain, 2 * n_trips, r))

    n_q = pl.num_programs(1)
    load_queries(qn_ref)
    produce(jnp.minimum(i + 1, n_q - 1), 0, masked=True)

    f = lambda r: r[...].astype(F32)
    lam = (jnp.exp(jnp.sum(f(lq1_ref) * f(lk1_ref), axis=1, keepdims=True))
           - jnp.exp(jnp.sum(f(lq2_ref) * f(lk2_ref), axis=1, keepdims=True)) + LAM_INIT)
    gain = gsub_ref[...] * (1.0 - LAM_INIT)
    for h in range(H):
        on = acc_scr[h, 0:DV, :] / acc_scr[h, DV:DV + 1, :]
        ot = on[:, 0:tq] - lam * on[:, tq:2 * tq]
        ot = ot * lax.rsqrt(jnp.mean(ot * ot, axis=0, keepdims=True) + RMS_EPS)
        o = ot.T * gain
        gate = za_ref[0, :, h * DV:(h + 1) * DV].astype(F32)
        o_ref[0, :, h * DV:(h + 1) * DV] = (o * gate).astype(o_ref.dtype)


def _diff_attention(qa_t, ka, va_t, za, lam_q1, lam_k1, lam_q2, lam_k2, g_sub_a, *, tq):
    B, S, W = ka.shape
    tk = tq
    assert W == DA_HEADS * LANE and S % tq == 0 and qa_t.shape == (B, W, S) and va_t.shape == (B, W, S)
    vec = lambda v: v.reshape(1, DA_HEAD_DIM)
    vec_spec = pl.BlockSpec((1, DA_HEAD_DIM), lambda b, i: (0, 0))
    tile_spec = pl.BlockSpec((1, tq, W), lambda b, i: (b, i, 0))
    seq_spec = pl.BlockSpec((1, S, W), lambda b, i: (b, 0, 0))
    n_q = S // tq
    q_spec = pl.BlockSpec((1, W, tq), lambda b, i: (b, 0, i))
    next_q_spec = pl.BlockSpec((1, W, tq), lambda b, i: (b, 0, jnp.minimum(i + 1, n_q - 1)))
    seq_t_spec = pl.BlockSpec((1, W, S), lambda b, i: (b, 0, 0))
    kern = functools.partial(_attn_kernel, tq=tq, tk=tk)
    return pl.pallas_call(
        kern,
        grid=(B, n_q),
        in_specs=[vec_spec, vec_spec, vec_spec, vec_spec, q_spec, next_q_spec, seq_spec, seq_t_spec,
                  tile_spec,
                  pl.BlockSpec((1, DA_V_DIM), lambda b, i: (0, 0))],
        out_specs=tile_spec,
        out_shape=jax.ShapeDtypeStruct((B, S, W), BF16),
        scratch_shapes=[pltpu.VMEM((DA_HEADS, 2 * LANE, 2 * tq), BF16),
                        pltpu.VMEM((DA_HEADS, S, 2 * LANE), BF16),
                        pltpu.VMEM((DA_HEADS, S // tk, DA_V_DIM + ONES_ROWS, tk), BF16),
                        pltpu.VMEM((tk, 2 * tq), F32),
                        pltpu.VMEM((2, DA_HEADS, tk, 2 * tq), F32),
                        pltpu.VMEM((2, DA_HEADS, 1, 2 * tq), F32),
                        pltpu.VMEM((2, DA_HEADS, tk, 2 * tq), BF16),
                        pltpu.VMEM((2, DA_HEADS, 1, 2 * tq), F32),
                        pltpu.VMEM((DA_HEADS, 1, 2 * tq), F32),
                        pltpu.VMEM((DA_HEADS, DA_V_DIM + ONES_ROWS, 2 * tq), F32)],
        compiler_params=pltpu.CompilerParams(
            dimension_semantics=("parallel", "arbitrary"), vmem_limit_bytes=VMEM_LIMIT),
        name="diffattn",
    )(vec(lam_q1), vec(lam_k1), vec(lam_q2), vec(lam_k2), qa_t, qa_t, ka, va_t, za,
      g_sub_a.reshape(1, DA_V_DIM))


def _gla_tables(C, n_heads, dk):
    t = np.arange(C)[:, None]
    u = np.arange(C)[None, :]
    sums = [(u <= t), (u > t)]
    halves = []
    m = C // 2
    while m >= 1:
        halves.append(m)
        m //= 2
    upper_rows, pair_masks = [], []
    for m in halves:
        blk = 2 * m
        mid = (t // blk) * blk + m
        upper = (t % blk) >= m
        sums.append(np.where(upper, (u >= mid) & (u <= t), (u > t) & (u <= mid - 1)))
        upper_rows.append(np.broadcast_to(upper, (C, n_heads * dk)))
        tt, ss = np.arange(C)[:, None], np.arange(C)[None, :]
        pair_masks.append(((tt // blk) == (ss // blk)) & ((tt % blk) >= m) & ((ss % blk) < m))
    pair_masks.append(np.eye(C, dtype=bool))
    sum_sel = np.concatenate(sums, axis=0).astype(np.float32)
    upper_rows = np.stack(upper_rows).astype(np.float32)
    eye_h = np.eye(n_heads, dtype=bool)
    pair_bd = np.stack([np.kron(eye_h, pm) for pm in pair_masks]).astype(np.float32)
    head_of_lane = np.arange(n_heads * dk)[None, :] // dk
    head_of_row = np.repeat(np.arange(n_heads), C)[:, None]
    head_mask = (head_of_lane == head_of_row).astype(np.float32)
    return sum_sel, upper_rows, pair_bd, head_mask


def _gla_kernel(q_ref, k_ref, g_ref, v_ref, z_ref, gsub_ref, sumsel_ref, upper_ref, pair_ref, hmask_ref,
                o_ref, state_scr, decay_scr, mixed_scr, *, chunk, n_chunks, n_levels):
    C = chunk
    H, DV = GLA_HEADS, GLA_DV

    @pl.when(pl.program_id(1) == 0)
    def _():
        state_scr[...] = jnp.zeros(state_scr.shape, F32)

    hmask = hmask_ref[...]

    def stack_heads(a):
        return jnp.concatenate([a.astype(BF16)] * H, axis=0) * hmask

    def decays(c, buf):
        r0 = pl.multiple_of(c * C, C)
        g = g_ref[0, pl.ds(r0, C), :]
        g_hi = g.astype(BF16)
        g_lo = (g - g_hi.astype(F32)).astype(BF16)
        g2 = jnp.concatenate([g_hi, g_lo], axis=0)
        d_all = jnp.dot(sumsel_ref[...], g2, preferred_element_type=F32)
        decay_scr[buf] = jnp.exp(d_all)

    def mix(c, buf, state_t):
        r0 = pl.multiple_of(c * C, C)
        q = q_ref[0, pl.ds(r0, C), :].astype(F32)
        k = k_ref[0, pl.ds(r0, C), :].astype(F32)
        v = v_ref[0, pl.ds(r0, C), :]
        v_st = jnp.concatenate([v[:, hh * DV:(hh + 1) * DV] for hh in range(H)], axis=0)
        v_st_t = v_st.astype(F32).T.astype(BF16)
        e_all = decay_scr.at[buf]

        k_out = stack_heads(k * e_all[C:2 * C])
        new_state_t = (e_all[C - 1:C] * state_t
                       + jnp.dot(v_st_t, k_out, preferred_element_type=F32))

        a_bd = None
        for lvl in range(n_levels + 1):
            if lvl < n_levels:
                e = e_all[(2 + lvl) * C:(3 + lvl) * C]
                xs = stack_heads(jnp.where(upper_ref[lvl] > 0.5, q, k) * e)
                prod = lax.dot_general(xs, xs, NT_DIMS, preferred_element_type=F32)
            else:
                prod = lax.dot_general(stack_heads(q), stack_heads(k), NT_DIMS, preferred_element_type=F32)
            term = prod * pair_ref[lvl]
            a_bd = term if a_bd is None else a_bd + term

        q_in = stack_heads(q * e_all[0:C])
        mixed_scr[buf] = (lax.dot_general(q_in, state_t.astype(BF16), NT_DIMS, preferred_element_type=F32)
                          + jnp.dot(a_bd.astype(BF16), v_st, preferred_element_type=F32))
        return new_state_t

    def finish(c, buf):
        r0 = pl.multiple_of(c * C, C)
        for hh in range(H):
            o_h = _rms(mixed_scr[buf, hh * C:(hh + 1) * C, :], gsub_ref[...])
            gate = z_ref[0, pl.ds(r0, C), hh * DV:(hh + 1) * DV].astype(F32)
            o_ref[0, pl.ds(r0, C), hh * DV:(hh + 1) * DV] = (o_h * gate).astype(o_ref.dtype)

    decays(0, 0)
    mixed_scr[1] = jnp.zeros(mixed_scr.shape[1:], F32)

    def pair(cc, carry):
        c = 2 * cc
        state = state_scr[...]
        decays(c + 1, 1)
        state = mix(c, 0, state)
        finish(jnp.maximum(c - 1, 0), 1)
        decays(jnp.minimum(c + 2, n_chunks - 1), 0)
        state = mix(c + 1, 1, state)
        finish(c, 0)
        state_scr[...] = state
        return carry

    lax.fori_loop(0, n_chunks // 2, pair, 0)
    finish(n_chunks - 1, 1)


def _gated_linear_attention(qb, kb, glog, vb, zb, g_sub_b, *, chunk, rows_per_step):
    B, S, b_qk = qb.shape
    b_v = vb.shape[-1]
    sum_sel, upper_rows, pair_bd, head_mask = _gla_tables(chunk, GLA_HEADS, GLA_DK)
    n_levels = upper_rows.shape[0]
    T = rows_per_step

    def row_spec(width):
        return pl.BlockSpec((1, T, width), lambda b, i: (b, i, 0))

    def const_spec(shape):
        return pl.BlockSpec(shape, lambda b, i: (0,) * len(shape))

    sum_sel2 = np.concatenate([sum_sel, sum_sel], axis=1)
    assert (T // chunk) % 2 == 0
    kern = functools.partial(_gla_kernel, chunk=chunk, n_chunks=T // chunk, n_levels=n_levels)
    return pl.pallas_call(
        kern,
        grid=(B, S // T),
        in_specs=[row_spec(b_qk), row_spec(b_qk), row_spec(b_qk), row_spec(b_v), row_spec(b_v),
                  const_spec((1, GLA_DV)), const_spec(sum_sel2.shape), const_spec(upper_rows.shape),
                  const_spec(pair_bd.shape), const_spec(head_mask.shape)],
        out_specs=row_spec(b_v),
        out_shape=jax.ShapeDtypeStruct((B, S, b_v), BF16),
        scratch_shapes=[pltpu.VMEM((GLA_DV, b_qk), F32),
                        pltpu.VMEM((2,) + (sum_sel.shape[0], b_qk), F32),
                        pltpu.VMEM((2, GLA_HEADS * chunk, GLA_DV), F32)],
        compiler_params=pltpu.CompilerParams(
            dimension_semantics=("parallel", "arbitrary"), vmem_limit_bytes=VMEM_LIMIT),
        name="gla",
    )(qb, kb, glog, vb, zb, g_sub_b.reshape(1, GLA_DV), jnp.asarray(sum_sel2, BF16),
      jnp.asarray(upper_rows), jnp.asarray(pair_bd), jnp.asarray(head_mask, BF16))


def _merge_kernel(x_ref, ua_ref, ub_ref, ga_ref, gb_ref, wua_ref, wub_ref, wout_ref, gpost_ref, o_ref):
    ya = jnp.dot(ua_ref[0], wua_ref[...], preferred_element_type=F32)
    yb = jnp.dot(ub_ref[0], wub_ref[...], preferred_element_type=F32)
    y = jax.nn.sigmoid(ga_ref[0].astype(F32)) * ya + jax.nn.sigmoid(gb_ref[0].astype(F32)) * yb
    out = jnp.dot(y.astype(BF16), wout_ref[...], preferred_element_type=F32)
    o_ref[0] = x_ref[0] + _rms(out, gpost_ref[...])


def _merge(x, ua, ub, ga, gb, w_up_a, w_up_b, w_out, g_post, *, tm):
    B, S, D = x.shape

    def row_spec(width):
        return pl.BlockSpec((1, tm, width), lambda b, i: (b, i, 0))

    def const_spec(shape):
        return pl.BlockSpec(shape, lambda b, i: (0,) * len(shape))

    return pl.pallas_call(
        _merge_kernel,
        grid=(B, S // tm),
        in_specs=[row_spec(D), row_spec(ua.shape[-1]), row_spec(ub.shape[-1]), row_spec(D), row_spec(D),
                  const_spec(w_up_a.shape), const_spec(w_up_b.shape), const_spec(w_out.shape),
                  const_spec((1, D))],
        out_specs=row_spec(D),
        out_shape=jax.ShapeDtypeStruct((B, S, D), x.dtype),
        compiler_params=pltpu.CompilerParams(
            dimension_semantics=("parallel", "parallel"), vmem_limit_bytes=VMEM_LIMIT),
        name="merge",
    )(x, ua, ub, ga, gb, w_up_a, w_up_b, w_out, g_post.reshape(1, D))


ROW_TILE = 512
ATTN_TILE = 256
GLA_CHUNK = 64
GLA_STEP_ROWS = 2048


def _layer(x, g_pre, w_in, lam_q1, lam_k1, lam_q2, lam_k2, g_sub_a, w_alpha, b_alpha, g_sub_b,
           w_up_a, w_up_b, w_out, g_post):
    n_mix = 2 * (DA_HEADS * 2 * DA_HEAD_DIM) + 2 * (DA_HEADS * DA_V_DIM) + 2 * (GLA_HEADS * GLA_DK) \
        + 2 * (GLA_HEADS * GLA_DV)
    gate0 = n_mix + GLA_GATE_RANK
    qa_t, ka, va_t, za, qb, kb, vb, zb, ga, gb, glog = _input_projection(
        x, g_pre, w_in.astype(BF16), w_in[:, gate0:].astype(BF16), w_alpha.astype(BF16), b_alpha,
        tm=ROW_TILE)
    ua = _diff_attention(qa_t, ka, va_t, za, lam_q1, lam_k1, lam_q2, lam_k2, g_sub_a, tq=ATTN_TILE)
    ub = _gated_linear_attention(qb, kb, glog, vb, zb, g_sub_b, chunk=GLA_CHUNK, rows_per_step=GLA_STEP_ROWS)
    return _merge(x, ua, ub, ga, gb, w_up_a.astype(BF16), w_up_b.astype(BF16), w_out.astype(BF16),
                  g_post, tm=ROW_TILE)


def kernel(x, g_pre, w_in, lam_q1, lam_k1, lam_q2, lam_k2, g_sub_a, w_alpha, b_alpha, g_sub_b, w_up_a, w_up_b, w_out, g_post):
    depth = w_in.shape[0]
    assert depth == 1, "LAM_INIT is specialised to a single layer"
    first = lambda p: p.reshape(p.shape[1:])
    return _layer(x, *(first(p) for p in (g_pre, w_in, lam_q1, lam_k1, lam_q2, lam_k2, g_sub_a, w_alpha,
                                          b_alpha, g_sub_b, w_up_a, w_up_b, w_out, g_post)))
```

```python
import functools
import math

import numpy as np
import jax
import jax.numpy as jnp
from jax import lax
from jax.experimental import pallas as pl
from jax.experimental.pallas import tpu as pltpu

F32 = jnp.float32
BF16 = jnp.bfloat16

DA_HEADS = 4
DA_HEAD_DIM = 64
DA_V_DIM = 128
GLA_HEADS = 4
GLA_DK = 64
GLA_DV = 128
GLA_GATE_RANK = 16
GLA_TAU = 16.0
RMS_EPS = 1e-6
LAYER_IDX = 0
LAM_INIT = 0.8 - 0.6 * math.exp(-0.3 * LAYER_IDX)

LOG2E = math.log2(math.e)
LANE = 128
ONES_ROWS = 16
SLOPE_PARTS = 3
VMEM_LIMIT = 56 * 1024 * 1024

NT_DIMS = (((1,), (1,)), ((), ()))


def _rms(x, g):
    return x * lax.rsqrt(jnp.mean(x * x, axis=-1, keepdims=True) + RMS_EPS) * g


def _silu(z):
    return z * jax.nn.sigmoid(z)


def _inproj_kernel(x_ref, gpre_ref, wmix_ref, wgate_ref, walpha_ref, balpha_ref,
                   qa_ref, ka_ref, va_ref, za_ref, qb_ref, kb_ref, vb_ref, zb_ref,
                   ga_ref, gb_ref, glog_ref, h_scr, t_scr, *, segments, lr_cols, feature_major,
                   q_scale_a, q_scale_b):
    outs = dict(qa=qa_ref, ka=ka_ref, va=va_ref, za=za_ref, qb=qb_ref, kb=kb_ref, vb=vb_ref,
                zb=zb_ref, ga=ga_ref, gb=gb_ref)
    weights = dict(mix=wmix_ref, gate=wgate_ref)
    scales = dict(qa=q_scale_a, qb=q_scale_b)
    h_scr[...] = _rms(x_ref[0], gpre_ref[...]).astype(BF16)
    for name, wname, c0, width, o0 in segments:
        acc = jnp.dot(h_scr[...], weights[wname][:, c0:c0 + width], preferred_element_type=F32)
        if name in scales:
            acc = acc * scales[name]
        if name in ("za", "zb"):
            acc = _silu(acc)
        if name in feature_major:
            t_scr[...] = acc
            outs[name][0, o0:o0 + width, :] = t_scr[...].T.astype(BF16)
        else:
            outs[name][0, :, o0:o0 + width] = acc.astype(BF16)
    lr = jnp.dot(h_scr[...], wmix_ref[:, lr_cols[0]:lr_cols[1]], preferred_element_type=F32)
    z = jnp.dot(lr.astype(BF16), walpha_ref[...], preferred_element_type=F32) + balpha_ref[...]
    log_sig = jnp.minimum(z, 0.0) - jnp.log1p(jnp.exp(-jnp.abs(z)))
    glog_ref[0] = log_sig / GLA_TAU


def _input_projection(x, g_pre, w_mix_lr, w_gate, w_alpha, b_alpha, *, tm):
    B, S, D = x.shape
    a_qk, a_v = DA_HEADS * 2 * DA_HEAD_DIM, DA_HEADS * DA_V_DIM
    b_qk, b_v = GLA_HEADS * GLA_DK, GLA_HEADS * GLA_DV
    widths = [("qa", "mix", a_qk), ("ka", "mix", a_qk), ("va", "mix", a_v), ("za", "mix", a_v),
              ("qb", "mix", b_qk), ("kb", "mix", b_qk), ("vb", "mix", b_v), ("zb", "mix", b_v),
              ("ga", "gate", D), ("gb", "gate", D)]
    max_chunk = 512
    segments = []
    col = dict(mix=0, gate=0)
    for name, wname, width in widths:
        for o0 in range(0, width, max_chunk):
            segments.append((name, wname, col[wname] + o0, min(max_chunk, width - o0), o0))
        col[wname] += width
    lr_cols = (col["mix"], col["mix"] + GLA_GATE_RANK)
    assert lr_cols[1] <= w_mix_lr.shape[1] and col["gate"] == w_gate.shape[1]

    def row_spec(width):
        return pl.BlockSpec((1, tm, width), lambda b, i: (b, i, 0))

    def const_spec(shape):
        return pl.BlockSpec(shape, lambda b, i: (0,) * len(shape), pipeline_mode=pl.Buffered(1))

    feature_major = ("qa", "va")
    out_shape, out_specs = [], []
    for name, _, w in widths:
        if name in feature_major:
            out_shape.append(jax.ShapeDtypeStruct((B, w, S), BF16))
            out_specs.append(pl.BlockSpec((1, w, tm), lambda b, i: (b, 0, i)))
        else:
            out_shape.append(jax.ShapeDtypeStruct((B, S, w), BF16))
            out_specs.append(row_spec(w))
    out_shape.append(jax.ShapeDtypeStruct((B, S, b_qk), F32))
    out_specs.append(row_spec(b_qk))
    kern = functools.partial(_inproj_kernel, segments=tuple(segments), lr_cols=lr_cols,
                             feature_major=feature_major,
                             q_scale_a=DA_HEAD_DIM ** -0.5 * LOG2E,
                             q_scale_b=GLA_DK ** -0.5)
    return pl.pallas_call(
        kern,
        grid=(B, S // tm),
        in_specs=[row_spec(D), const_spec((1, D)), const_spec(w_mix_lr.shape),
                  const_spec(w_gate.shape), const_spec(w_alpha.shape), const_spec((1, b_qk))],
        out_specs=out_specs,
        out_shape=out_shape,
        scratch_shapes=[pltpu.VMEM((tm, D), BF16), pltpu.VMEM((tm, max_chunk), F32)],
        compiler_params=pltpu.CompilerParams(
            dimension_semantics=("parallel", "parallel"), vmem_limit_bytes=VMEM_LIMIT),
        name="inproj",
    )(x, g_pre.reshape(1, D), w_mix_lr, w_gate, w_alpha, b_alpha.reshape(1, b_qk))


def _attn_kernel(lq1_ref, lk1_ref, lq2_ref, lk2_ref, q_ref, qn_ref, k_ref, vt_ref, za_ref, gsub_ref,
                 o_ref, qs_scr, ks_scr, vt_scr, causal_scr, st_scr, mx_scr, p_scr, alpha_scr, m_scr, acc_scr,
                 *, tq, tk):
    i = pl.program_id(1)
    H, DV = DA_HEADS, DA_V_DIM
    n_kv = vt_scr.shape[1]
    slope_parts, slopes = [], []
    for h in range(H):
        rest = np.float32(2.0 ** (-8.0 * (h + 1) / H) * LOG2E)
        slopes.append(float(rest))
        parts = []
        for _ in range(SLOPE_PARTS):
            part = np.asarray(rest, dtype=BF16)
            parts.append(float(part))
            rest = np.float32(rest - np.float32(part))
        assert rest == 0.0
        slope_parts.append(parts)

    def prepare_keys_values():
        lane = lax.broadcasted_iota(jnp.int32, (tk, LANE), 1)
        key = lax.broadcasted_iota(jnp.int32, (tk, LANE), 0).astype(F32)
        key_lanes = jnp.where(lane < SLOPE_PARTS, key, 0.0).astype(BF16)
        ones_rows = jnp.ones((ONES_ROWS, tk), BF16)
        def prep(j, carry):
            k0 = pl.multiple_of(j * tk, tk)
            for h in range(H):
                ks_scr[h, pl.ds(k0, tk), 0:LANE] = k_ref[0, pl.ds(k0, tk), h * LANE:(h + 1) * LANE]
                ks_scr[h, pl.ds(k0, tk), LANE:2 * LANE] = key_lanes
            return carry
        lax.fori_loop(0, n_kv, prep, 0)
        for j in range(n_kv):
            for h in range(H):
                vt_scr[h, j, 0:DV, :] = vt_ref[0, h * DV:(h + 1) * DV, j * tk:(j + 1) * tk]
                vt_scr[h, j, DV:DV + ONES_ROWS, :] = ones_rows
        key_idx = lax.broadcasted_iota(jnp.int32, (tk, 2 * tq), 0)
        qcol = lax.broadcasted_iota(jnp.int32, (tk, 2 * tq), 1)
        visible = key_idx <= jnp.where(qcol >= tq, qcol - tq, qcol)
        causal_scr[...] = jnp.where(visible, 0.0, -jnp.inf)

    def load_queries(ref):
        row = lax.broadcasted_iota(jnp.int32, (LANE, tq), 0)
        zero = jnp.zeros((LANE, tq), BF16)
        for h in range(H):
            qt = ref[0, h * LANE:(h + 1) * LANE, :]
            qs_scr[h, 0:LANE, 0:tq] = jnp.where(row < DA_HEAD_DIM, qt, zero)
            qs_scr[h, 0:LANE, tq:2 * tq] = jnp.where(row >= DA_HEAD_DIM, qt, zero)

    def load_slopes():
        row = lax.broadcasted_iota(jnp.int32, (LANE, 2 * tq), 0)
        for h in range(H):
            rows = jnp.zeros((LANE, 2 * tq), F32)
            for r, part in enumerate(slope_parts[h]):
                rows = jnp.where(row == r, part, rows)
            qs_scr[h, LANE:2 * LANE, :] = rows.astype(BF16)

    def produce(j, buf, masked=False):
        k0 = pl.multiple_of(j * tk, tk)
        for h in range(H):
            st = jnp.dot(ks_scr[h, pl.ds(k0, tk), :], qs_scr[h], preferred_element_type=F32)
            if masked:
                st = st + causal_scr[...]
            st_scr[buf, h] = st
            mx_scr[buf, h] = jnp.max(st, axis=0, keepdims=True)

    def softmax(j, buf):
        tile_dist = (j * tk - i * tq).astype(F32)
        for h in range(H):
            off = slopes[h] * tile_dist
            m_prev = m_scr[h]
            m_next = jnp.maximum(m_prev, mx_scr[buf, h] + off)
            alpha_scr[buf, h] = jnp.exp2(m_prev - m_next)
            p_scr[buf, h] = jnp.exp2(st_scr[buf, h] - (m_next - off)).astype(BF16)
            m_scr[h] = m_next

    def weighted_values(j, buf):
        for h in range(H):
            acc_scr[h] = alpha_scr[buf, h] * acc_scr[h] + jnp.dot(vt_scr[h, j], p_scr[buf, h],
                                                                  preferred_element_type=F32)

    def run(stage, base, offset):
        pos, buf = base + offset, offset % 2
        if offset > 0:
            tile = pos - 1
        else:
            tile = jnp.where(pos == 0, i, pos - 1)
        if stage == "produce":
            produce(tile, buf)
        elif stage == "softmax":
            softmax(tile, buf)
        else:
            weighted_values(tile, buf)

    def steady(base):
        for t in (1, 2):
            run("produce", base, t + 1)
            run("weighted_values", base, t - 1)
            run("softmax", base, t)

    def drain(base, r):
        for t in range(1, r + 1):
            if t + 1 <= r:
                run("produce", base, t + 1)
            run("weighted_values", base, t - 1)
            run("softmax", base, t)
        run("weighted_values", base, r)

    @pl.when(i == 0)
    def _():
        prepare_keys_values()
        load_slopes()
        load_queries(q_ref)
        produce(0, 0, masked=True)

    m_scr[...] = jnp.full(m_scr.shape, -jnp.inf, F32)
    acc_scr[...] = jnp.zeros(acc_scr.shape, F32)

    @pl.when(i == 0)
    def _():
        softmax(0, 0)
        weighted_values(0, 0)

    @pl.when(i > 0)
    def _():
        run("produce", 0, 1)
        run("softmax", 0, 0)

        def trip(t, carry):
            steady(2 * t)
            return carry
        n_trips = (i - 1) // 2
        lax.fori_loop(0, n_trips, trip, 0)

        for r in (1, 2):
            pl.when(i - 2 * n_trips == r)(functools.partial(drain, 2 * n_trips, r))

    n_q = pl.num_programs(1)
    load_queries(qn_ref)
    produce(jnp.minimum(i + 1, n_q - 1), 0, masked=True)

    f = lambda r: r[...].astype(F32)
    lam = (jnp.exp(jnp.sum(f(lq1_ref) * f(lk1_ref), axis=1, keepdims=True))
           - jnp.exp(jnp.sum(f(lq2_ref) * f(lk2_ref), axis=1, keepdims=True)) + LAM_INIT)
    gain = gsub_ref[...] * (1.0 - LAM_INIT)
    for h in range(H):
        on = acc_scr[h, 0:DV, :] / acc_scr[h, DV:DV + 1, :]
        ot = on[:, 0:tq] - lam * on[:, tq:2 * tq]
        ot = ot * lax.rsqrt(jnp.mean(ot * ot, axis=0, keepdims=True) + RMS_EPS)
        o = ot.T * gain
        gate = za_ref[0, :, h * DV:(h + 1) * DV].astype(F32)
        o_ref[0, :, h * DV:(h + 1) * DV] = (o * gate).astype(o_ref.dtype)


def _diff_attention(qa_t, ka, va_t, za, lam_q1, lam_k1, lam_q2, lam_k2, g_sub_a, *, tq):
    B, S, W = ka.shape
    tk = tq
    assert W == DA_HEADS * LANE and S % tq == 0 and qa_t.shape == (B, W, S) and va_t.shape == (B, W, S)
    vec = lambda v: v.reshape(1, DA_HEAD_DIM)
    vec_spec = pl.BlockSpec((1, DA_HEAD_DIM), lambda b, i: (0, 0))
    tile_spec = pl.BlockSpec((1, tq, W), lambda b, i: (b, i, 0))
    seq_spec = pl.BlockSpec((1, S, W), lambda b, i: (b, 0, 0))
    n_q = S // tq
    q_spec = pl.BlockSpec((1, W, tq), lambda b, i: (b, 0, i))
    next_q_spec = pl.BlockSpec((1, W, tq), lambda b, i: (b, 0, jnp.minimum(i + 1, n_q - 1)))
    seq_t_spec = pl.BlockSpec((1, W, S), lambda b, i: (b, 0, 0))
    kern = functools.partial(_attn_kernel, tq=tq, tk=tk)
    return pl.pallas_call(
        kern,
        grid=(B, n_q),
        in_specs=[vec_spec, vec_spec, vec_spec, vec_spec, q_spec, next_q_spec, seq_spec, seq_t_spec,
                  tile_spec,
                  pl.BlockSpec((1, DA_V_DIM), lambda b, i: (0, 0))],
        out_specs=tile_spec,
        out_shape=jax.ShapeDtypeStruct((B, S, W), BF16),
        scratch_shapes=[pltpu.VMEM((DA_HEADS, 2 * LANE, 2 * tq), BF16),
                        pltpu.VMEM((DA_HEADS, S, 2 * LANE), BF16),
                        pltpu.VMEM((DA_HEADS, S // tk, DA_V_DIM + ONES_ROWS, tk), BF16),
                        pltpu.VMEM((tk, 2 * tq), F32),
                        pltpu.VMEM((2, DA_HEADS, tk, 2 * tq), F32),
                        pltpu.VMEM((2, DA_HEADS, 1, 2 * tq), F32),
                        pltpu.VMEM((2, DA_HEADS, tk, 2 * tq), BF16),
                        pltpu.VMEM((2, DA_HEADS, 1, 2 * tq), F32),
                        pltpu.VMEM((DA_HEADS, 1, 2 * tq), F32),
                        pltpu.VMEM((DA_HEADS, DA_V_DIM + ONES_ROWS, 2 * tq), F32)],
        compiler_params=pltpu.CompilerParams(
            dimension_semantics=("parallel", "arbitrary"), vmem_limit_bytes=VMEM_LIMIT),
        name="diffattn",
    )(vec(lam_q1), vec(lam_k1), vec(lam_q2), vec(lam_k2), qa_t, qa_t, ka, va_t, za,
      g_sub_a.reshape(1, DA_V_DIM))


def _gla_tables(C, n_heads, dk):
    t = np.arange(C)[:, None]
    u = np.arange(C)[None, :]
    sums = [(u <= t), (u > t)]
    halves = []
    m = C // 2
    while m >= 1:
        halves.append(m)
        m //= 2
    upper_rows, pair_masks = [], []
    for m in halves:
        blk = 2 * m
        mid = (t // blk) * blk + m
        upper = (t % blk) >= m
        sums.append(np.where(upper, (u >= mid) & (u <= t), (u > t) & (u <= mid - 1)))
        upper_rows.append(np.broadcast_to(upper, (C, n_heads * dk)))
        tt, ss = np.arange(C)[:, None], np.arange(C)[None, :]
        pair_masks.append(((tt // blk) == (ss // blk)) & ((tt % blk) >= m) & ((ss % blk) < m))
    pair_masks.append(np.eye(C, dtype=bool))
    sum_sel = np.concatenate(sums, axis=0).astype(np.float32)
    upper_rows = np.stack(upper_rows).astype(np.float32)
    eye_h = np.eye(n_heads, dtype=bool)
    pair_bd = np.stack([np.kron(eye_h, pm) for pm in pair_masks]).astype(np.float32)
    head_of_lane = np.arange(n_heads * dk)[None, :] // dk
    head_of_row = np.repeat(np.arange(n_heads), C)[:, None]
    head_mask = (head_of_lane == head_of_row).astype(np.float32)
    return sum_sel, upper_rows, pair_bd, head_mask


CHUNKS_PER_TRIP = 8


def _gla_kernel(q_ref, k_ref, g_ref, v_ref, z_ref, gsub_ref, sumsel_ref, upper_ref, pair_ref, hmask_ref,
                o_ref, state_scr, decay_scr, mixed_scr, *, chunk, n_chunks, n_levels):
    C = chunk
    H, DV = GLA_HEADS, GLA_DV

    @pl.when(pl.program_id(1) == 0)
    def _():
        state_scr[...] = jnp.zeros(state_scr.shape, F32)

    hmask = hmask_ref[...]

    def stack_heads(a):
        return jnp.concatenate([a.astype(BF16)] * H, axis=0) * hmask

    def decays(c, buf):
        r0 = pl.multiple_of(c * C, C)
        g = g_ref[0, pl.ds(r0, C), :]
        g_hi = g.astype(BF16)
        g_lo = (g - g_hi.astype(F32)).astype(BF16)
        g2 = jnp.concatenate([g_hi, g_lo], axis=0)
        d_all = jnp.dot(sumsel_ref[...], g2, preferred_element_type=F32)
        decay_scr[buf] = jnp.exp(d_all)

    def mix(c, buf, state_t):
        r0 = pl.multiple_of(c * C, C)
        q = q_ref[0, pl.ds(r0, C), :].astype(F32)
        k = k_ref[0, pl.ds(r0, C), :].astype(F32)
        v = v_ref[0, pl.ds(r0, C), :]
        v_st = jnp.concatenate([v[:, hh * DV:(hh + 1) * DV] for hh in range(H)], axis=0)
        v_st_t = v_st.astype(F32).T.astype(BF16)
        e_all = decay_scr.at[buf]

        k_out = stack_heads(k * e_all[C:2 * C])
        new_state_t = (e_all[C - 1:C] * state_t
                       + jnp.dot(v_st_t, k_out, preferred_element_type=F32))

        a_bd = None
        for lvl in range(n_levels + 1):
            if lvl < n_levels:
                e = e_all[(2 + lvl) * C:(3 + lvl) * C]
                xs = stack_heads(jnp.where(upper_ref[lvl] > 0.5, q, k) * e)
                prod = lax.dot_general(xs, xs, NT_DIMS, preferred_element_type=F32)
            else:
                prod = lax.dot_general(stack_heads(q), stack_heads(k), NT_DIMS, preferred_element_type=F32)
            term = prod * pair_ref[lvl]
            a_bd = term if a_bd is None else a_bd + term

        q_in = stack_heads(q * e_all[0:C])
        mixed_scr[buf] = (lax.dot_general(q_in, state_t.astype(BF16), NT_DIMS, preferred_element_type=F32)
                          + jnp.dot(a_bd.astype(BF16), v_st, preferred_element_type=F32))
        return new_state_t

    def finish(c, buf):
        r0 = pl.multiple_of(c * C, C)
        for hh in range(H):
            o_h = _rms(mixed_scr[buf, hh * C:(hh + 1) * C, :], gsub_ref[...])
            gate = z_ref[0, pl.ds(r0, C), hh * DV:(hh + 1) * DV].astype(F32)
            o_ref[0, pl.ds(r0, C), hh * DV:(hh + 1) * DV] = (o_h * gate).astype(o_ref.dtype)

    decays(0, 0)
    mixed_scr[1] = jnp.zeros(mixed_scr.shape[1:], F32)

    def trip(cc, carry):
        c0 = CHUNKS_PER_TRIP * cc
        state = state_scr[...]
        for u in range(CHUNKS_PER_TRIP):
            c = c0 + u
            decays(jnp.minimum(c + 1, n_chunks - 1), (u + 1) % 2)
            state = mix(c, u % 2, state)
            finish(jnp.maximum(c - 1, 0), (u + 1) % 2)
        state_scr[...] = state
        return carry

    lax.fori_loop(0, n_chunks // CHUNKS_PER_TRIP, trip, 0)
    finish(n_chunks - 1, 1)


def _gated_linear_attention(qb, kb, glog, vb, zb, g_sub_b, *, chunk, rows_per_step):
    B, S, b_qk = qb.shape
    b_v = vb.shape[-1]
    sum_sel, upper_rows, pair_bd, head_mask = _gla_tables(chunk, GLA_HEADS, GLA_DK)
    n_levels = upper_rows.shape[0]
    T = rows_per_step

    def row_spec(width):
        return pl.BlockSpec((1, T, width), lambda b, i: (b, i, 0))

    def const_spec(shape):
        return pl.BlockSpec(shape, lambda b, i: (0,) * len(shape))

    sum_sel2 = np.concatenate([sum_sel, sum_sel], axis=1)
    assert CHUNKS_PER_TRIP % 2 == 0 and (T // chunk) % CHUNKS_PER_TRIP == 0
    kern = functools.partial(_gla_kernel, chunk=chunk, n_chunks=T // chunk, n_levels=n_levels)
    return pl.pallas_call(
        kern,
        grid=(B, S // T),
        in_specs=[row_spec(b_qk), row_spec(b_qk), row_spec(b_qk), row_spec(b_v), row_spec(b_v),
                  const_spec((1, GLA_DV)), const_spec(sum_sel2.shape), const_spec(upper_rows.shape),
                  const_spec(pair_bd.shape), const_spec(head_mask.shape)],
        out_specs=row_spec(b_v),
        out_shape=jax.ShapeDtypeStruct((B, S, b_v), BF16),
        scratch_shapes=[pltpu.VMEM((GLA_DV, b_qk), F32),
                        pltpu.VMEM((2,) + (sum_sel.shape[0], b_qk), F32),
                        pltpu.VMEM((2, GLA_HEADS * chunk, GLA_DV), F32)],
        compiler_params=pltpu.CompilerParams(
            dimension_semantics=("parallel", "arbitrary"), vmem_limit_bytes=VMEM_LIMIT),
        name="gla",
    )(qb, kb, glog, vb, zb, g_sub_b.reshape(1, GLA_DV), jnp.asarray(sum_sel2, BF16),
      jnp.asarray(upper_rows), jnp.asarray(pair_bd), jnp.asarray(head_mask, BF16))


def _merge_kernel(x_ref, ua_ref, ub_ref, ga_ref, gb_ref, wua_ref, wub_ref, wout_ref, gpost_ref, o_ref):
    ya = jnp.dot(ua_ref[0], wua_ref[...], preferred_element_type=F32)
    yb = jnp.dot(ub_ref[0], wub_ref[...], preferred_element_type=F32)
    y = jax.nn.sigmoid(ga_ref[0].astype(F32)) * ya + jax.nn.sigmoid(gb_ref[0].astype(F32)) * yb
    out = jnp.dot(y.astype(BF16), wout_ref[...], preferred_element_type=F32)
    o_ref[0] = x_ref[0] + _rms(out, gpost_ref[...])


def _merge(x, ua, ub, ga, gb, w_up_a, w_up_b, w_out, g_post, *, tm):
    B, S, D = x.shape

    def row_spec(width):
        return pl.BlockSpec((1, tm, width), lambda b, i: (b, i, 0))

    def const_spec(shape):
        return pl.BlockSpec(shape, lambda b, i: (0,) * len(shape))

    return pl.pallas_call(
        _merge_kernel,
        grid=(B, S // tm),
        in_specs=[row_spec(D), row_spec(ua.shape[-1]), row_spec(ub.shape[-1]), row_spec(D), row_spec(D),
                  const_spec(w_up_a.shape), const_spec(w_up_b.shape), const_spec(w_out.shape),
                  const_spec((1, D))],
        out_specs=row_spec(D),
        out_shape=jax.ShapeDtypeStruct((B, S, D), x.dtype),
        compiler_params=pltpu.CompilerParams(
            dimension_semantics=("parallel", "parallel"), vmem_limit_bytes=VMEM_LIMIT),
        name="merge",
    )(x, ua, ub, ga, gb, w_up_a, w_up_b, w_out, g_post.reshape(1, D))


ROW_TILE = 512
ATTN_TILE = 256
GLA_CHUNK = 64
GLA_STEP_ROWS = 2048


def _layer(x, g_pre, w_in, lam_q1, lam_k1, lam_q2, lam_k2, g_sub_a, w_alpha, b_alpha, g_sub_b,
           w_up_a, w_up_b, w_out, g_post):
    n_mix = 2 * (DA_HEADS * 2 * DA_HEAD_DIM) + 2 * (DA_HEADS * DA_V_DIM) + 2 * (GLA_HEADS * GLA_DK) \
        + 2 * (GLA_HEADS * GLA_DV)
    gate0 = n_mix + GLA_GATE_RANK
    qa_t, ka, va_t, za, qb, kb, vb, zb, ga, gb, glog = _input_projection(
        x, g_pre, w_in.astype(BF16), w_in[:, gate0:].astype(BF16), w_alpha.astype(BF16), b_alpha,
        tm=ROW_TILE)
    ua = _diff_attention(qa_t, ka, va_t, za, lam_q1, lam_k1, lam_q2, lam_k2, g_sub_a, tq=ATTN_TILE)
    ub = _gated_linear_attention(qb, kb, glog, vb, zb, g_sub_b, chunk=GLA_CHUNK, rows_per_step=GLA_STEP_ROWS)
    return _merge(x, ua, ub, ga, gb, w_up_a.astype(BF16), w_up_b.astype(BF16), w_out.astype(BF16),
                  g_post, tm=ROW_TILE)


def kernel(x, g_pre, w_in, lam_q1, lam_k1, lam_q2, lam_k2, g_sub_a, w_alpha, b_alpha, g_sub_b, w_up_a, w_up_b, w_out, g_post):
    depth = w_in.shape[0]
    assert depth == 1, "LAM_INIT is specialised to a single layer"
    first = lambda p: p.reshape(p.shape[1:])
    return _layer(x, *(first(p) for p in (g_pre, w_in, lam_q1, lam_k1, lam_q2, lam_k2, g_sub_a, w_alpha,
                                          b_alpha, g_sub_b, w_up_a, w_up_b, w_out, g_post)))
```

```python
import functools
import math

import numpy as np
import jax
import jax.numpy as jnp
from jax import lax
from jax.experimental import pallas as pl
from jax.experimental.pallas import tpu as pltpu

F32 = jnp.float32
BF16 = jnp.bfloat16

DA_HEADS = 4
DA_HEAD_DIM = 64
DA_V_DIM = 128
GLA_HEADS = 4
GLA_DK = 64
GLA_DV = 128
GLA_GATE_RANK = 16
GLA_TAU = 16.0
RMS_EPS = 1e-6
LAYER_IDX = 0
LAM_INIT = 0.8 - 0.6 * math.exp(-0.3 * LAYER_IDX)

LOG2E = math.log2(math.e)
LANE = 128
ONES_ROWS = 16
SLOPE_PARTS = 3
VMEM_LIMIT = 56 * 1024 * 1024

NT_DIMS = (((1,), (1,)), ((), ()))


def _rms(x, g):
    return x * lax.rsqrt(jnp.mean(x * x, axis=-1, keepdims=True) + RMS_EPS) * g


def _silu(z):
    return z * jax.nn.sigmoid(z)


def _inproj_kernel(x_ref, gpre_ref, wmix_ref, wgate_ref, walpha_ref, balpha_ref,
                   qa_ref, ka_ref, va_ref, za_ref, qb_ref, kb_ref, vb_ref, zb_ref,
                   ga_ref, gb_ref, glog_ref, h_scr, t_scr, *, segments, lr_cols, feature_major,
                   q_scale_a, q_scale_b):
    outs = dict(qa=qa_ref, ka=ka_ref, va=va_ref, za=za_ref, qb=qb_ref, kb=kb_ref, vb=vb_ref,
                zb=zb_ref, ga=ga_ref, gb=gb_ref)
    weights = dict(mix=wmix_ref, gate=wgate_ref)
    scales = dict(qa=q_scale_a, qb=q_scale_b)
    h_scr[...] = _rms(x_ref[0], gpre_ref[...]).astype(BF16)
    for name, wname, c0, width, o0 in segments:
        acc = jnp.dot(h_scr[...], weights[wname][:, c0:c0 + width], preferred_element_type=F32)
        if name in scales:
            acc = acc * scales[name]
        if name in ("za", "zb"):
            acc = _silu(acc)
        if name in feature_major:
            t_scr[...] = acc
            outs[name][0, o0:o0 + width, :] = t_scr[...].T.astype(BF16)
        else:
            outs[name][0, :, o0:o0 + width] = acc.astype(BF16)
    lr = jnp.dot(h_scr[...], wmix_ref[:, lr_cols[0]:lr_cols[1]], preferred_element_type=F32)
    z = jnp.dot(lr.astype(BF16), walpha_ref[...], preferred_element_type=F32) + balpha_ref[...]
    log_sig = jnp.minimum(z, 0.0) - jnp.log1p(jnp.exp(-jnp.abs(z)))
    glog_ref[0] = log_sig / GLA_TAU


def _input_projection(x, g_pre, w_mix_lr, w_gate, w_alpha, b_alpha, *, tm):
    B, S, D = x.shape
    a_qk, a_v = DA_HEADS * 2 * DA_HEAD_DIM, DA_HEADS * DA_V_DIM
    b_qk, b_v = GLA_HEADS * GLA_DK, GLA_HEADS * GLA_DV
    widths = [("qa", "mix", a_qk), ("ka", "mix", a_qk), ("va", "mix", a_v), ("za", "mix", a_v),
              ("qb", "mix", b_qk), ("kb", "mix", b_qk), ("vb", "mix", b_v), ("zb", "mix", b_v),
              ("ga", "gate", D), ("gb", "gate", D)]
    max_chunk = 512
    segments = []
    col = dict(mix=0, gate=0)
    for name, wname, width in widths:
        for o0 in range(0, width, max_chunk):
            segments.append((name, wname, col[wname] + o0, min(max_chunk, width - o0), o0))
        col[wname] += width
    lr_cols = (col["mix"], col["mix"] + GLA_GATE_RANK)
    assert lr_cols[1] <= w_mix_lr.shape[1] and col["gate"] == w_gate.shape[1]

    def row_spec(width):
        return pl.BlockSpec((1, tm, width), lambda b, i: (b, i, 0))

    def const_spec(shape):
        return pl.BlockSpec(shape, lambda b, i: (0,) * len(shape), pipeline_mode=pl.Buffered(1))

    feature_major = ("qa", "va")
    out_shape, out_specs = [], []
    for name, _, w in widths:
        if name in feature_major:
            out_shape.append(jax.ShapeDtypeStruct((B, w, S), BF16))
            out_specs.append(pl.BlockSpec((1, w, tm), lambda b, i: (b, 0, i)))
        else:
            out_shape.append(jax.ShapeDtypeStruct((B, S, w), BF16))
            out_specs.append(row_spec(w))
    out_shape.append(jax.ShapeDtypeStruct((B, S, b_qk), F32))
    out_specs.append(row_spec(b_qk))
    kern = functools.partial(_inproj_kernel, segments=tuple(segments), lr_cols=lr_cols,
                             feature_major=feature_major,
                             q_scale_a=DA_HEAD_DIM ** -0.5 * LOG2E,
                             q_scale_b=GLA_DK ** -0.5)
    return pl.pallas_call(
        kern,
        grid=(B, S // tm),
        in_specs=[row_spec(D), const_spec((1, D)), const_spec(w_mix_lr.shape),
                  const_spec(w_gate.shape), const_spec(w_alpha.shape), const_spec((1, b_qk))],
        out_specs=out_specs,
        out_shape=out_shape,
        scratch_shapes=[pltpu.VMEM((tm, D), BF16), pltpu.VMEM((tm, max_chunk), F32)],
        compiler_params=pltpu.CompilerParams(
            dimension_semantics=("parallel", "parallel"), vmem_limit_bytes=VMEM_LIMIT),
        name="inproj",
    )(x, g_pre.reshape(1, D), w_mix_lr, w_gate, w_alpha, b_alpha.reshape(1, b_qk))


POSITIONS_PER_TRIP = 4


def _attn_kernel(lq1_ref, lk1_ref, lq2_ref, lk2_ref, q_ref, qn_ref, k_ref, vt_ref, za_ref, gsub_ref,
                 o_ref, qs_scr, ks_scr, vt_scr, causal_scr, st_scr, mx_scr, p_scr, alpha_scr, m_scr, acc_scr,
                 *, tq, tk):
    i = pl.program_id(1)
    H, DV = DA_HEADS, DA_V_DIM
    n_kv = vt_scr.shape[1]
    slope_parts, slopes = [], []
    for h in range(H):
        rest = np.float32(2.0 ** (-8.0 * (h + 1) / H) * LOG2E)
        slopes.append(float(rest))
        parts = []
        for _ in range(SLOPE_PARTS):
            part = np.asarray(rest, dtype=BF16)
            parts.append(float(part))
            rest = np.float32(rest - np.float32(part))
        assert rest == 0.0
        slope_parts.append(parts)

    def prepare_keys_values():
        lane = lax.broadcasted_iota(jnp.int32, (tk, LANE), 1)
        key = lax.broadcasted_iota(jnp.int32, (tk, LANE), 0).astype(F32)
        key_lanes = jnp.where(lane < SLOPE_PARTS, key, 0.0).astype(BF16)
        ones_rows = jnp.ones((ONES_ROWS, tk), BF16)
        def prep(j, carry):
            k0 = pl.multiple_of(j * tk, tk)
            for h in range(H):
                ks_scr[h, pl.ds(k0, tk), 0:LANE] = k_ref[0, pl.ds(k0, tk), h * LANE:(h + 1) * LANE]
                ks_scr[h, pl.ds(k0, tk), LANE:2 * LANE] = key_lanes
            return carry
        lax.fori_loop(0, n_kv, prep, 0)
        for j in range(n_kv):
            for h in range(H):
                vt_scr[h, j, 0:DV, :] = vt_ref[0, h * DV:(h + 1) * DV, j * tk:(j + 1) * tk]
                vt_scr[h, j, DV:DV + ONES_ROWS, :] = ones_rows
        key_idx = lax.broadcasted_iota(jnp.int32, (tk, 2 * tq), 0)
        qcol = lax.broadcasted_iota(jnp.int32, (tk, 2 * tq), 1)
        visible = key_idx <= jnp.where(qcol >= tq, qcol - tq, qcol)
        causal_scr[...] = jnp.where(visible, 0.0, -jnp.inf)

    def load_queries(ref):
        row = lax.broadcasted_iota(jnp.int32, (LANE, tq), 0)
        zero = jnp.zeros((LANE, tq), BF16)
        for h in range(H):
            qt = ref[0, h * LANE:(h + 1) * LANE, :]
            qs_scr[h, 0:LANE, 0:tq] = jnp.where(row < DA_HEAD_DIM, qt, zero)
            qs_scr[h, 0:LANE, tq:2 * tq] = jnp.where(row >= DA_HEAD_DIM, qt, zero)

    def load_slopes():
        row = lax.broadcasted_iota(jnp.int32, (LANE, 2 * tq), 0)
        for h in range(H):
            rows = jnp.zeros((LANE, 2 * tq), F32)
            for r, part in enumerate(slope_parts[h]):
                rows = jnp.where(row == r, part, rows)
            qs_scr[h, LANE:2 * LANE, :] = rows.astype(BF16)

    def produce(j, buf, masked=False):
        k0 = pl.multiple_of(j * tk, tk)
        for h in range(H):
            st = jnp.dot(ks_scr[h, pl.ds(k0, tk), :], qs_scr[h], preferred_element_type=F32)
            if masked:
                st = st + causal_scr[...]
            st_scr[buf, h] = st
            mx_scr[buf, h] = jnp.max(st, axis=0, keepdims=True)

    def softmax(j, buf):
        tile_dist = (j * tk - i * tq).astype(F32)
        for h in range(H):
            off = slopes[h] * tile_dist
            m_prev = m_scr[h]
            m_next = jnp.maximum(m_prev, mx_scr[buf, h] + off)
            alpha_scr[buf, h] = jnp.exp2(m_prev - m_next)
            p_scr[buf, h] = jnp.exp2(st_scr[buf, h] - (m_next - off)).astype(BF16)
            m_scr[h] = m_next

    def weighted_values(j, buf):
        for h in range(H):
            acc_scr[h] = alpha_scr[buf, h] * acc_scr[h] + jnp.dot(vt_scr[h, j], p_scr[buf, h],
                                                                  preferred_element_type=F32)

    def run(stage, base, offset):
        pos, buf = base + offset, offset % 2
        if offset > 0:
            tile = pos - 1
        else:
            tile = jnp.where(pos == 0, i, pos - 1)
        if stage == "produce":
            produce(tile, buf)
        elif stage == "softmax":
            softmax(tile, buf)
        else:
            weighted_values(tile, buf)

    def steady(base):
        for t in range(1, POSITIONS_PER_TRIP + 1):
            run("produce", base, t + 1)
            run("weighted_values", base, t - 1)
            run("softmax", base, t)

    def drain(base, r):
        for t in range(1, r + 1):
            if t + 1 <= r:
                run("produce", base, t + 1)
            run("weighted_values", base, t - 1)
            run("softmax", base, t)
        run("weighted_values", base, r)

    @pl.when(i == 0)
    def _():
        prepare_keys_values()
        load_slopes()
        load_queries(q_ref)
        produce(0, 0, masked=True)

    m_scr[...] = jnp.full(m_scr.shape, -jnp.inf, F32)
    acc_scr[...] = jnp.zeros(acc_scr.shape, F32)

    @pl.when(i == 0)
    def _():
        softmax(0, 0)
        weighted_values(0, 0)

    @pl.when(i > 0)
    def _():
        run("produce", 0, 1)
        run("softmax", 0, 0)

        def trip(t, carry):
            steady(POSITIONS_PER_TRIP * t)
            return carry
        n_trips = (i - 1) // POSITIONS_PER_TRIP
        lax.fori_loop(0, n_trips, trip, 0)

        done = POSITIONS_PER_TRIP * n_trips
        for r in range(1, POSITIONS_PER_TRIP + 1):
            pl.when(i - done == r)(functools.partial(drain, done, r))

    n_q = pl.num_programs(1)
    load_queries(qn_ref)
    produce(jnp.minimum(i + 1, n_q - 1), 0, masked=True)

    f = lambda r: r[...].astype(F32)
    lam = (jnp.exp(jnp.sum(f(lq1_ref) * f(lk1_ref), axis=1, keepdims=True))
           - jnp.exp(jnp.sum(f(lq2_ref) * f(lk2_ref), axis=1, keepdims=True)) + LAM_INIT)
    gain = gsub_ref[...] * (1.0 - LAM_INIT)
    for h in range(H):
        on = acc_scr[h, 0:DV, :] / acc_scr[h, DV:DV + 1, :]
        ot = on[:, 0:tq] - lam * on[:, tq:2 * tq]
        ot = ot * lax.rsqrt(jnp.mean(ot * ot, axis=0, keepdims=True) + RMS_EPS)
        o = ot.T * gain
        gate = za_ref[0, :, h * DV:(h + 1) * DV].astype(F32)
        o_ref[0, :, h * DV:(h + 1) * DV] = (o * gate).astype(o_ref.dtype)


def _diff_attention(qa_t, ka, va_t, za, lam_q1, lam_k1, lam_q2, lam_k2, g_sub_a, *, tq):
    B, S, W = ka.shape
    tk = tq
    assert W == DA_HEADS * LANE and S % tq == 0 and qa_t.shape == (B, W, S) and va_t.shape == (B, W, S)
    vec = lambda v: v.reshape(1, DA_HEAD_DIM)
    vec_spec = pl.BlockSpec((1, DA_HEAD_DIM), lambda b, i: (0, 0))
    tile_spec = pl.BlockSpec((1, tq, W), lambda b, i: (b, i, 0))
    seq_spec = pl.BlockSpec((1, S, W), lambda b, i: (b, 0, 0))
    n_q = S // tq
    q_spec = pl.BlockSpec((1, W, tq), lambda b, i: (b, 0, i))
    next_q_spec = pl.BlockSpec((1, W, tq), lambda b, i: (b, 0, jnp.minimum(i + 1, n_q - 1)))
    seq_t_spec = pl.BlockSpec((1, W, S), lambda b, i: (b, 0, 0))
    kern = functools.partial(_attn_kernel, tq=tq, tk=tk)
    return pl.pallas_call(
        kern,
        grid=(B, n_q),
        in_specs=[vec_spec, vec_spec, vec_spec, vec_spec, q_spec, next_q_spec, seq_spec, seq_t_spec,
                  tile_spec,
                  pl.BlockSpec((1, DA_V_DIM), lambda b, i: (0, 0))],
        out_specs=tile_spec,
        out_shape=jax.ShapeDtypeStruct((B, S, W), BF16),
        scratch_shapes=[pltpu.VMEM((DA_HEADS, 2 * LANE, 2 * tq), BF16),
                        pltpu.VMEM((DA_HEADS, S, 2 * LANE), BF16),
                        pltpu.VMEM((DA_HEADS, S // tk, DA_V_DIM + ONES_ROWS, tk), BF16),
                        pltpu.VMEM((tk, 2 * tq), F32),
                        pltpu.VMEM((2, DA_HEADS, tk, 2 * tq), F32),
                        pltpu.VMEM((2, DA_HEADS, 1, 2 * tq), F32),
                        pltpu.VMEM((2, DA_HEADS, tk, 2 * tq), BF16),
                        pltpu.VMEM((2, DA_HEADS, 1, 2 * tq), F32),
                        pltpu.VMEM((DA_HEADS, 1, 2 * tq), F32),
                        pltpu.VMEM((DA_HEADS, DA_V_DIM + ONES_ROWS, 2 * tq), F32)],
        compiler_params=pltpu.CompilerParams(
            dimension_semantics=("parallel", "arbitrary"), vmem_limit_bytes=VMEM_LIMIT),
        name="diffattn",
    )(vec(lam_q1), vec(lam_k1), vec(lam_q2), vec(lam_k2), qa_t, qa_t, ka, va_t, za,
      g_sub_a.reshape(1, DA_V_DIM))


def _gla_tables(C, n_heads, dk):
    t = np.arange(C)[:, None]
    u = np.arange(C)[None, :]
    sums = [(u <= t), (u > t)]
    halves = []
    m = C // 2
    while m >= 1:
        halves.append(m)
        m //= 2
    upper_rows, pair_masks = [], []
    for m in halves:
        blk = 2 * m
        mid = (t // blk) * blk + m
        upper = (t % blk) >= m
        sums.append(np.where(upper, (u >= mid) & (u <= t), (u > t) & (u <= mid - 1)))
        upper_rows.append(np.broadcast_to(upper, (C, n_heads * dk)))
        tt, ss = np.arange(C)[:, None], np.arange(C)[None, :]
        pair_masks.append(((tt // blk) == (ss // blk)) & ((tt % blk) >= m) & ((ss % blk) < m))
    pair_masks.append(np.eye(C, dtype=bool))
    sum_sel = np.concatenate(sums, axis=0).astype(np.float32)
    upper_rows = np.stack(upper_rows).astype(np.float32)
    eye_h = np.eye(n_heads, dtype=bool)
    pair_bd = np.stack([np.kron(eye_h, pm) for pm in pair_masks]).astype(np.float32)
    head_of_lane = np.arange(n_heads * dk)[None, :] // dk
    head_of_row = np.repeat(np.arange(n_heads), C)[:, None]
    head_mask = (head_of_lane == head_of_row).astype(np.float32)
    return sum_sel, upper_rows, pair_bd, head_mask


CHUNKS_PER_TRIP = 8


def _gla_kernel(q_ref, k_ref, g_ref, v_ref, z_ref, gsub_ref, sumsel_ref, upper_ref, pair_ref, hmask_ref,
                o_ref, state_scr, decay_scr, mixed_scr, *, chunk, n_chunks, n_levels):
    C = chunk
    H, DV = GLA_HEADS, GLA_DV

    @pl.when(pl.program_id(1) == 0)
    def _():
        state_scr[...] = jnp.zeros(state_scr.shape, F32)

    hmask = hmask_ref[...]

    def stack_heads(a):
        return jnp.concatenate([a.astype(BF16)] * H, axis=0) * hmask

    def decays(c, buf):
        r0 = pl.multiple_of(c * C, C)
        g = g_ref[0, pl.ds(r0, C), :]
        g_hi = g.astype(BF16)
        g_lo = (g - g_hi.astype(F32)).astype(BF16)
        g2 = jnp.concatenate([g_hi, g_lo], axis=0)
        d_all = jnp.dot(sumsel_ref[...], g2, preferred_element_type=F32)
        decay_scr[buf] = jnp.exp(d_all)

    def mix(c, buf, state_t):
        r0 = pl.multiple_of(c * C, C)
        q = q_ref[0, pl.ds(r0, C), :].astype(F32)
        k = k_ref[0, pl.ds(r0, C), :].astype(F32)
        v = v_ref[0, pl.ds(r0, C), :]
        v_st = jnp.concatenate([v[:, hh * DV:(hh + 1) * DV] for hh in range(H)], axis=0)
        v_st_t = v_st.astype(F32).T.astype(BF16)
        e_all = decay_scr.at[buf]

        k_out = stack_heads(k * e_all[C:2 * C])
        new_state_t = (e_all[C - 1:C] * state_t
                       + jnp.dot(v_st_t, k_out, preferred_element_type=F32))

        a_bd = None
        for lvl in range(n_levels + 1):
            if lvl < n_levels:
                e = e_all[(2 + lvl) * C:(3 + lvl) * C]
                xs = stack_heads(jnp.where(upper_ref[lvl] > 0.5, q, k) * e)
                prod = lax.dot_general(xs, xs, NT_DIMS, preferred_element_type=F32)
            else:
                prod = lax.dot_general(stack_heads(q), stack_heads(k), NT_DIMS, preferred_element_type=F32)
            term = prod * pair_ref[lvl]
            a_bd = term if a_bd is None else a_bd + term

        q_in = stack_heads(q * e_all[0:C])
        mixed_scr[buf] = (lax.dot_general(q_in, state_t.astype(BF16), NT_DIMS, preferred_element_type=F32)
                          + jnp.dot(a_bd.astype(BF16), v_st, preferred_element_type=F32))
        return new_state_t

    def finish(c, buf):
        r0 = pl.multiple_of(c * C, C)
        for hh in range(H):
            o_h = _rms(mixed_scr[buf, hh * C:(hh + 1) * C, :], gsub_ref[...])
            gate = z_ref[0, pl.ds(r0, C), hh * DV:(hh + 1) * DV].astype(F32)
            o_ref[0, pl.ds(r0, C), hh * DV:(hh + 1) * DV] = (o_h * gate).astype(o_ref.dtype)

    decays(0, 0)
    mixed_scr[1] = jnp.zeros(mixed_scr.shape[1:], F32)

    def trip(cc, carry):
        c0 = CHUNKS_PER_TRIP * cc
        state = state_scr[...]
        for u in range(CHUNKS_PER_TRIP):
            c = c0 + u
            decays(jnp.minimum(c + 1, n_chunks - 1), (u + 1) % 2)
            state = mix(c, u % 2, state)
            finish(jnp.maximum(c - 1, 0), (u + 1) % 2)
        state_scr[...] = state
        return carry

    lax.fori_loop(0, n_chunks // CHUNKS_PER_TRIP, trip, 0)
    finish(n_chunks - 1, 1)


def _gated_linear_attention(qb, kb, glog, vb, zb, g_sub_b, *, chunk, rows_per_step):
    B, S, b_qk = qb.shape
    b_v = vb.shape[-1]
    sum_sel, upper_rows, pair_bd, head_mask = _gla_tables(chunk, GLA_HEADS, GLA_DK)
    n_levels = upper_rows.shape[0]
    T = rows_per_step

    def row_spec(width):
        return pl.BlockSpec((1, T, width), lambda b, i: (b, i, 0))

    def const_spec(shape):
        return pl.BlockSpec(shape, lambda b, i: (0,) * len(shape))

    sum_sel2 = np.concatenate([sum_sel, sum_sel], axis=1)
    assert CHUNKS_PER_TRIP % 2 == 0 and (T // chunk) % CHUNKS_PER_TRIP == 0
    kern = functools.partial(_gla_kernel, chunk=chunk, n_chunks=T // chunk, n_levels=n_levels)
    return pl.pallas_call(
        kern,
        grid=(B, S // T),
        in_specs=[row_spec(b_qk), row_spec(b_qk), row_spec(b_qk), row_spec(b_v), row_spec(b_v),
                  const_spec((1, GLA_DV)), const_spec(sum_sel2.shape), const_spec(upper_rows.shape),
                  const_spec(pair_bd.shape), const_spec(head_mask.shape)],
        out_specs=row_spec(b_v),
        out_shape=jax.ShapeDtypeStruct((B, S, b_v), BF16),
        scratch_shapes=[pltpu.VMEM((GLA_DV, b_qk), F32),
                        pltpu.VMEM((2,) + (sum_sel.shape[0], b_qk), F32),
                        pltpu.VMEM((2, GLA_HEADS * chunk, GLA_DV), F32)],
        compiler_params=pltpu.CompilerParams(
            dimension_semantics=("parallel", "arbitrary"), vmem_limit_bytes=VMEM_LIMIT),
        name="gla",
    )(qb, kb, glog, vb, zb, g_sub_b.reshape(1, GLA_DV), jnp.asarray(sum_sel2, BF16),
      jnp.asarray(upper_rows), jnp.asarray(pair_bd), jnp.asarray(head_mask, BF16))


def _merge_kernel(x_ref, ua_ref, ub_ref, ga_ref, gb_ref, wua_ref, wub_ref, wout_ref, gpost_ref, o_ref):
    ya = jnp.dot(ua_ref[0], wua_ref[...], preferred_element_type=F32)
    yb = jnp.dot(ub_ref[0], wub_ref[...], preferred_element_type=F32)
    y = jax.nn.sigmoid(ga_ref[0].astype(F32)) * ya + jax.nn.sigmoid(gb_ref[0].astype(F32)) * yb
    out = jnp.dot(y.astype(BF16), wout_ref[...], preferred_element_type=F32)
    o_ref[0] = x_ref[0] + _rms(out, gpost_ref[...])


def _merge(x, ua, ub, ga, gb, w_up_a, w_up_b, w_out, g_post, *, tm):
    B, S, D = x.shape

    def row_spec(width):
        return pl.BlockSpec((1, tm, width), lambda b, i: (b, i, 0))

    def const_spec(shape):
        return pl.BlockSpec(shape, lambda b, i: (0,) * len(shape))

    return pl.pallas_call(
        _merge_kernel,
        grid=(B, S // tm),
        in_specs=[row_spec(D), row_spec(ua.shape[-1]), row_spec(ub.shape[-1]), row_spec(D), row_spec(D),
                  const_spec(w_up_a.shape), const_spec(w_up_b.shape), const_spec(w_out.shape),
                  const_spec((1, D))],
        out_specs=row_spec(D),
        out_shape=jax.ShapeDtypeStruct((B, S, D), x.dtype),
        compiler_params=pltpu.CompilerParams(
            dimension_semantics=("parallel", "parallel"), vmem_limit_bytes=VMEM_LIMIT),
        name="merge",
    )(x, ua, ub, ga, gb, w_up_a, w_up_b, w_out, g_post.reshape(1, D))


ROW_TILE = 512
ATTN_TILE = 256
GLA_CHUNK = 64
GLA_STEP_ROWS = 2048


def _layer(x, g_pre, w_in, lam_q1, lam_k1, lam_q2, lam_k2, g_sub_a, w_alpha, b_alpha, g_sub_b,
           w_up_a, w_up_b, w_out, g_post):
    n_mix = 2 * (DA_HEADS * 2 * DA_HEAD_DIM) + 2 * (DA_HEADS * DA_V_DIM) + 2 * (GLA_HEADS * GLA_DK) \
        + 2 * (GLA_HEADS * GLA_DV)
    gate0 = n_mix + GLA_GATE_RANK
    qa_t, ka, va_t, za, qb, kb, vb, zb, ga, gb, glog = _input_projection(
        x, g_pre, w_in.astype(BF16), w_in[:, gate0:].astype(BF16), w_alpha.astype(BF16), b_alpha,
        tm=ROW_TILE)
    ua = _diff_attention(qa_t, ka, va_t, za, lam_q1, lam_k1, lam_q2, lam_k2, g_sub_a, tq=ATTN_TILE)
    ub = _gated_linear_attention(qb, kb, glog, vb, zb, g_sub_b, chunk=GLA_CHUNK, rows_per_step=GLA_STEP_ROWS)
    return _merge(x, ua, ub, ga, gb, w_up_a.astype(BF16), w_up_b.astype(BF16), w_out.astype(BF16),
                  g_post, tm=ROW_TILE)


def kernel(x, g_pre, w_in, lam_q1, lam_k1, lam_q2, lam_k2, g_sub_a, w_alpha, b_alpha, g_sub_b, w_up_a, w_up_b, w_out, g_post):
    depth = w_in.shape[0]
    assert depth == 1, "LAM_INIT is specialised to a single layer"
    first = lambda p: p.reshape(p.shape[1:])
    return _layer(x, *(first(p) for p in (g_pre, w_in, lam_q1, lam_k1, lam_q2, lam_k2, g_sub_a, w_alpha,
                                          b_alpha, g_sub_b, w_up_a, w_up_b, w_out, g_post)))
```

```python
import functools
import math

import numpy as np
import jax
import jax.numpy as jnp
from jax import lax
from jax.experimental import pallas as pl
from jax.experimental.pallas import tpu as pltpu

F32 = jnp.float32
BF16 = jnp.bfloat16

DA_HEADS = 4
DA_HEAD_DIM = 64
DA_V_DIM = 128
GLA_HEADS = 4
GLA_DK = 64
GLA_DV = 128
GLA_GATE_RANK = 16
GLA_TAU = 16.0
RMS_EPS = 1e-6
LAYER_IDX = 0
LAM_INIT = 0.8 - 0.6 * math.exp(-0.3 * LAYER_IDX)

LOG2E = math.log2(math.e)
LANE = 128
ONES_ROWS = 16
SLOPE_PARTS = 3
VMEM_LIMIT = 56 * 1024 * 1024

NT_DIMS = (((1,), (1,)), ((), ()))


def _rms(x, g):
    return x * lax.rsqrt(jnp.mean(x * x, axis=-1, keepdims=True) + RMS_EPS) * g


def _silu(z):
    return z * jax.nn.sigmoid(z)


def _inproj_kernel(x_ref, gpre_ref, wmix_ref, wgate_ref, walpha_ref, balpha_ref,
                   qa_ref, ka_ref, va_ref, za_ref, qb_ref, kb_ref, vb_ref, zb_ref,
                   ga_ref, gb_ref, glog_ref, h_scr, t_scr, *, segments, lr_cols, feature_major,
                   q_scale_a, q_scale_b):
    outs = dict(qa=qa_ref, ka=ka_ref, va=va_ref, za=za_ref, qb=qb_ref, kb=kb_ref, vb=vb_ref,
                zb=zb_ref, ga=ga_ref, gb=gb_ref)
    weights = dict(mix=wmix_ref, gate=wgate_ref)
    scales = dict(qa=q_scale_a, qb=q_scale_b)
    h_scr[...] = _rms(x_ref[0], gpre_ref[...]).astype(BF16)
    for name, wname, c0, width, o0 in segments:
        acc = jnp.dot(h_scr[...], weights[wname][:, c0:c0 + width], preferred_element_type=F32)
        if name in scales:
            acc = acc * scales[name]
        if name in ("za", "zb"):
            acc = _silu(acc)
        if name in feature_major:
            t_scr[...] = acc
            outs[name][0, o0:o0 + width, :] = t_scr[...].T.astype(BF16)
        else:
            outs[name][0, :, o0:o0 + width] = acc.astype(BF16)
    lr = jnp.dot(h_scr[...], wmix_ref[:, lr_cols[0]:lr_cols[1]], preferred_element_type=F32)
    z = jnp.dot(lr.astype(BF16), walpha_ref[...], preferred_element_type=F32) + balpha_ref[...]
    log_sig = jnp.minimum(z, 0.0) - jnp.log1p(jnp.exp(-jnp.abs(z)))
    glog_ref[0] = log_sig / GLA_TAU


def _input_projection(x, g_pre, w_mix_lr, w_gate, w_alpha, b_alpha, *, tm):
    B, S, D = x.shape
    a_qk, a_v = DA_HEADS * 2 * DA_HEAD_DIM, DA_HEADS * DA_V_DIM
    b_qk, b_v = GLA_HEADS * GLA_DK, GLA_HEADS * GLA_DV
    widths = [("qa", "mix", a_qk), ("ka", "mix", a_qk), ("va", "mix", a_v), ("za", "mix", a_v),
              ("qb", "mix", b_qk), ("kb", "mix", b_qk), ("vb", "mix", b_v), ("zb", "mix", b_v),
              ("ga", "gate", D), ("gb", "gate", D)]
    max_chunk = 512
    segments = []
    col = dict(mix=0, gate=0)
    for name, wname, width in widths:
        for o0 in range(0, width, max_chunk):
            segments.append((name, wname, col[wname] + o0, min(max_chunk, width - o0), o0))
        col[wname] += width
    lr_cols = (col["mix"], col["mix"] + GLA_GATE_RANK)
    assert lr_cols[1] <= w_mix_lr.shape[1] and col["gate"] == w_gate.shape[1]

    def row_spec(width):
        return pl.BlockSpec((1, tm, width), lambda b, i: (b, i, 0))

    def const_spec(shape):
        return pl.BlockSpec(shape, lambda b, i: (0,) * len(shape), pipeline_mode=pl.Buffered(1))

    feature_major = ("qa", "va")
    out_shape, out_specs = [], []
    for name, _, w in widths:
        if name in feature_major:
            out_shape.append(jax.ShapeDtypeStruct((B, w, S), BF16))
            out_specs.append(pl.BlockSpec((1, w, tm), lambda b, i: (b, 0, i)))
        else:
            out_shape.append(jax.ShapeDtypeStruct((B, S, w), BF16))
            out_specs.append(row_spec(w))
    out_shape.append(jax.ShapeDtypeStruct((B, S, b_qk), F32))
    out_specs.append(row_spec(b_qk))
    kern = functools.partial(_inproj_kernel, segments=tuple(segments), lr_cols=lr_cols,
                             feature_major=feature_major,
                             q_scale_a=DA_HEAD_DIM ** -0.5 * LOG2E,
                             q_scale_b=GLA_DK ** -0.5)
    return pl.pallas_call(
        kern,
        grid=(B, S // tm),
        in_specs=[row_spec(D), const_spec((1, D)), const_spec(w_mix_lr.shape),
                  const_spec(w_gate.shape), const_spec(w_alpha.shape), const_spec((1, b_qk))],
        out_specs=out_specs,
        out_shape=out_shape,
        scratch_shapes=[pltpu.VMEM((tm, D), BF16), pltpu.VMEM((tm, max_chunk), F32)],
        compiler_params=pltpu.CompilerParams(
            dimension_semantics=("parallel", "parallel"), vmem_limit_bytes=VMEM_LIMIT),
        name="inproj",
    )(x, g_pre.reshape(1, D), w_mix_lr, w_gate, w_alpha, b_alpha.reshape(1, b_qk))


POSITIONS_PER_TRIP = 8


def _attn_kernel(lq1_ref, lk1_ref, lq2_ref, lk2_ref, q_ref, qn_ref, k_ref, vt_ref, za_ref, gsub_ref,
                 o_ref, qs_scr, ks_scr, vt_scr, causal_scr, st_scr, mx_scr, p_scr, alpha_scr, m_scr, acc_scr,
                 *, tq, tk):
    i = pl.program_id(1)
    H, DV = DA_HEADS, DA_V_DIM
    n_kv = vt_scr.shape[1]
    slope_parts, slopes = [], []
    for h in range(H):
        rest = np.float32(2.0 ** (-8.0 * (h + 1) / H) * LOG2E)
        slopes.append(float(rest))
        parts = []
        for _ in range(SLOPE_PARTS):
            part = np.asarray(rest, dtype=BF16)
            parts.append(float(part))
            rest = np.float32(rest - np.float32(part))
        assert rest == 0.0
        slope_parts.append(parts)

    def prepare_keys_values():
        lane = lax.broadcasted_iota(jnp.int32, (tk, LANE), 1)
        key = lax.broadcasted_iota(jnp.int32, (tk, LANE), 0).astype(F32)
        key_lanes = jnp.where(lane < SLOPE_PARTS, key, 0.0).astype(BF16)
        ones_rows = jnp.ones((ONES_ROWS, tk), BF16)
        def prep(j, carry):
            k0 = pl.multiple_of(j * tk, tk)
            for h in range(H):
                ks_scr[h, pl.ds(k0, tk), 0:LANE] = k_ref[0, pl.ds(k0, tk), h * LANE:(h + 1) * LANE]
                ks_scr[h, pl.ds(k0, tk), LANE:2 * LANE] = key_lanes
            return carry
        lax.fori_loop(0, n_kv, prep, 0)
        for j in range(n_kv):
            for h in range(H):
                vt_scr[h, j, 0:DV, :] = vt_ref[0, h * DV:(h + 1) * DV, j * tk:(j + 1) * tk]
                vt_scr[h, j, DV:DV + ONES_ROWS, :] = ones_rows
        key_idx = lax.broadcasted_iota(jnp.int32, (tk, 2 * tq), 0)
        qcol = lax.broadcasted_iota(jnp.int32, (tk, 2 * tq), 1)
        visible = key_idx <= jnp.where(qcol >= tq, qcol - tq, qcol)
        causal_scr[...] = jnp.where(visible, 0.0, -jnp.inf)

    def load_queries(ref):
        row = lax.broadcasted_iota(jnp.int32, (LANE, tq), 0)
        zero = jnp.zeros((LANE, tq), BF16)
        for h in range(H):
            qt = ref[0, h * LANE:(h + 1) * LANE, :]
            qs_scr[h, 0:LANE, 0:tq] = jnp.where(row < DA_HEAD_DIM, qt, zero)
            qs_scr[h, 0:LANE, tq:2 * tq] = jnp.where(row >= DA_HEAD_DIM, qt, zero)

    def load_slopes():
        row = lax.broadcasted_iota(jnp.int32, (LANE, 2 * tq), 0)
        for h in range(H):
            rows = jnp.zeros((LANE, 2 * tq), F32)
            for r, part in enumerate(slope_parts[h]):
                rows = jnp.where(row == r, part, rows)
            qs_scr[h, LANE:2 * LANE, :] = rows.astype(BF16)

    def produce(j, buf, masked=False):
        k0 = pl.multiple_of(j * tk, tk)
        for h in range(H):
            st = jnp.dot(ks_scr[h, pl.ds(k0, tk), :], qs_scr[h], preferred_element_type=F32)
            if masked:
                st = st + causal_scr[...]
            st_scr[buf, h] = st
            mx_scr[buf, h] = jnp.max(st, axis=0, keepdims=True)

    def softmax(j, buf):
        tile_dist = (j * tk - i * tq).astype(F32)
        for h in range(H):
            off = slopes[h] * tile_dist
            m_prev = m_scr[h]
            m_next = jnp.maximum(m_prev, mx_scr[buf, h] + off)
            alpha_scr[buf, h] = jnp.exp2(m_prev - m_next)
            p_scr[buf, h] = jnp.exp2(st_scr[buf, h] - (m_next - off)).astype(BF16)
            m_scr[h] = m_next

    def weighted_values(j, buf):
        for h in range(H):
            acc_scr[h] = alpha_scr[buf, h] * acc_scr[h] + jnp.dot(vt_scr[h, j], p_scr[buf, h],
                                                                  preferred_element_type=F32)

    def run(stage, base, offset):
        pos, buf = base + offset, offset % 2
        if offset > 0:
            tile = pos - 1
        else:
            tile = jnp.where(pos == 0, i, pos - 1)
        if stage == "produce":
            produce(tile, buf)
        elif stage == "softmax":
            softmax(tile, buf)
        else:
            weighted_values(tile, buf)

    def steady(base):
        for t in range(1, POSITIONS_PER_TRIP + 1):
            run("produce", base, t + 1)
            run("weighted_values", base, t - 1)
            run("softmax", base, t)

    def drain(base, r):
        for t in range(1, r + 1):
            if t + 1 <= r:
                run("produce", base, t + 1)
            run("weighted_values", base, t - 1)
            run("softmax", base, t)
        run("weighted_values", base, r)

    @pl.when(i == 0)
    def _():
        prepare_keys_values()
        load_slopes()
        load_queries(q_ref)
        produce(0, 0, masked=True)

    m_scr[...] = jnp.full(m_scr.shape, -jnp.inf, F32)
    acc_scr[...] = jnp.zeros(acc_scr.shape, F32)

    @pl.when(i == 0)
    def _():
        softmax(0, 0)
        weighted_values(0, 0)

    @pl.when(i > 0)
    def _():
        run("produce", 0, 1)
        run("softmax", 0, 0)

        def trip(t, carry):
            steady(POSITIONS_PER_TRIP * t)
            return carry
        n_trips = (i - 1) // POSITIONS_PER_TRIP
        lax.fori_loop(0, n_trips, trip, 0)

        done = POSITIONS_PER_TRIP * n_trips
        for r in range(1, POSITIONS_PER_TRIP + 1):
            pl.when(i - done == r)(functools.partial(drain, done, r))

    n_q = pl.num_programs(1)
    load_queries(qn_ref)
    produce(jnp.minimum(i + 1, n_q - 1), 0, masked=True)

    f = lambda r: r[...].astype(F32)
    lam = (jnp.exp(jnp.sum(f(lq1_ref) * f(lk1_ref), axis=1, keepdims=True))
           - jnp.exp(jnp.sum(f(lq2_ref) * f(lk2_ref), axis=1, keepdims=True)) + LAM_INIT)
    gain = gsub_ref[...] * (1.0 - LAM_INIT)
    for h in range(H):
        on = acc_scr[h, 0:DV, :] / acc_scr[h, DV:DV + 1, :]
        ot = on[:, 0:tq] - lam * on[:, tq:2 * tq]
        ot = ot * lax.rsqrt(jnp.mean(ot * ot, axis=0, keepdims=True) + RMS_EPS)
        o = ot.T * gain
        gate = za_ref[0, :, h * DV:(h + 1) * DV].astype(F32)
        o_ref[0, :, h * DV:(h + 1) * DV] = (o * gate).astype(o_ref.dtype)


def _diff_attention(qa_t, ka, va_t, za, lam_q1, lam_k1, lam_q2, lam_k2, g_sub_a, *, tq):
    B, S, W = ka.shape
    tk = tq
    assert W == DA_HEADS * LANE and S % tq == 0 and qa_t.shape == (B, W, S) and va_t.shape == (B, W, S)
    vec = lambda v: v.reshape(1, DA_HEAD_DIM)
    vec_spec = pl.BlockSpec((1, DA_HEAD_DIM), lambda b, i: (0, 0))
    tile_spec = pl.BlockSpec((1, tq, W), lambda b, i: (b, i, 0))
    seq_spec = pl.BlockSpec((1, S, W), lambda b, i: (b, 0, 0))
    n_q = S // tq
    q_spec = pl.BlockSpec((1, W, tq), lambda b, i: (b, 0, i))
    next_q_spec = pl.BlockSpec((1, W, tq), lambda b, i: (b, 0, jnp.minimum(i + 1, n_q - 1)))
    seq_t_spec = pl.BlockSpec((1, W, S), lambda b, i: (b, 0, 0))
    kern = functools.partial(_attn_kernel, tq=tq, tk=tk)
    return pl.pallas_call(
        kern,
        grid=(B, n_q),
        in_specs=[vec_spec, vec_spec, vec_spec, vec_spec, q_spec, next_q_spec, seq_spec, seq_t_spec,
                  tile_spec,
                  pl.BlockSpec((1, DA_V_DIM), lambda b, i: (0, 0))],
        out_specs=tile_spec,
        out_shape=jax.ShapeDtypeStruct((B, S, W), BF16),
        scratch_shapes=[pltpu.VMEM((DA_HEADS, 2 * LANE, 2 * tq), BF16),
                        pltpu.VMEM((DA_HEADS, S, 2 * LANE), BF16),
                        pltpu.VMEM((DA_HEADS, S // tk, DA_V_DIM + ONES_ROWS, tk), BF16),
                        pltpu.VMEM((tk, 2 * tq), F32),
                        pltpu.VMEM((2, DA_HEADS, tk, 2 * tq), F32),
                        pltpu.VMEM((2, DA_HEADS, 1, 2 * tq), F32),
                        pltpu.VMEM((2, DA_HEADS, tk, 2 * tq), BF16),
                        pltpu.VMEM((2, DA_HEADS, 1, 2 * tq), F32),
                        pltpu.VMEM((DA_HEADS, 1, 2 * tq), F32),
                        pltpu.VMEM((DA_HEADS, DA_V_DIM + ONES_ROWS, 2 * tq), F32)],
        compiler_params=pltpu.CompilerParams(
            dimension_semantics=("parallel", "arbitrary"), vmem_limit_bytes=VMEM_LIMIT),
        name="diffattn",
    )(vec(lam_q1), vec(lam_k1), vec(lam_q2), vec(lam_k2), qa_t, qa_t, ka, va_t, za,
      g_sub_a.reshape(1, DA_V_DIM))


def _gla_tables(C, n_heads, dk):
    t = np.arange(C)[:, None]
    u = np.arange(C)[None, :]
    sums = [(u <= t), (u > t)]
    halves = []
    m = C // 2
    while m >= 1:
        halves.append(m)
        m //= 2
    upper_rows, pair_masks = [], []
    for m in halves:
        blk = 2 * m
        mid = (t // blk) * blk + m
        upper = (t % blk) >= m
        sums.append(np.where(upper, (u >= mid) & (u <= t), (u > t) & (u <= mid - 1)))
        upper_rows.append(np.broadcast_to(upper, (C, n_heads * dk)))
        tt, ss = np.arange(C)[:, None], np.arange(C)[None, :]
        pair_masks.append(((tt // blk) == (ss // blk)) & ((tt % blk) >= m) & ((ss % blk) < m))
    pair_masks.append(np.eye(C, dtype=bool))
    sum_sel = np.concatenate(sums, axis=0).astype(np.float32)
    upper_rows = np.stack(upper_rows).astype(np.float32)
    eye_h = np.eye(n_heads, dtype=bool)
    pair_bd = np.stack([np.kron(eye_h, pm) for pm in pair_masks]).astype(np.float32)
    head_of_lane = np.arange(n_heads * dk)[None, :] // dk
    head_of_row = np.repeat(np.arange(n_heads), C)[:, None]
    head_mask = (head_of_lane == head_of_row).astype(np.float32)
    return sum_sel, upper_rows, pair_bd, head_mask


CHUNKS_PER_TRIP = 8


def _gla_kernel(q_ref, k_ref, g_ref, v_ref, z_ref, gsub_ref, sumsel_ref, upper_ref, pair_ref, hmask_ref,
                o_ref, state_scr, decay_scr, mixed_scr, *, chunk, n_chunks, n_levels):
    C = chunk
    H, DV = GLA_HEADS, GLA_DV

    @pl.when(pl.program_id(1) == 0)
    def _():
        state_scr[...] = jnp.zeros(state_scr.shape, F32)

    hmask = hmask_ref[...]

    def stack_heads(a):
        return jnp.concatenate([a.astype(BF16)] * H, axis=0) * hmask

    def decays(c, buf):
        r0 = pl.multiple_of(c * C, C)
        g = g_ref[0, pl.ds(r0, C), :]
        g_hi = g.astype(BF16)
        g_lo = (g - g_hi.astype(F32)).astype(BF16)
        g2 = jnp.concatenate([g_hi, g_lo], axis=0)
        d_all = jnp.dot(sumsel_ref[...], g2, preferred_element_type=F32)
        decay_scr[buf] = jnp.exp(d_all)

    def mix(c, buf, state_t):
        r0 = pl.multiple_of(c * C, C)
        q = q_ref[0, pl.ds(r0, C), :].astype(F32)
        k = k_ref[0, pl.ds(r0, C), :].astype(F32)
        v = v_ref[0, pl.ds(r0, C), :]
        v_st = jnp.concatenate([v[:, hh * DV:(hh + 1) * DV] for hh in range(H)], axis=0)
        v_st_t = v_st.astype(F32).T.astype(BF16)
        e_all = decay_scr.at[buf]

        k_out = stack_heads(k * e_all[C:2 * C])
        new_state_t = (e_all[C - 1:C] * state_t
                       + jnp.dot(v_st_t, k_out, preferred_element_type=F32))

        a_bd = None
        for lvl in range(n_levels + 1):
            if lvl < n_levels:
                e = e_all[(2 + lvl) * C:(3 + lvl) * C]
                xs = stack_heads(jnp.where(upper_ref[lvl] > 0.5, q, k) * e)
                prod = lax.dot_general(xs, xs, NT_DIMS, preferred_element_type=F32)
            else:
                prod = lax.dot_general(stack_heads(q), stack_heads(k), NT_DIMS, preferred_element_type=F32)
            term = prod * pair_ref[lvl]
            a_bd = term if a_bd is None else a_bd + term

        q_in = stack_heads(q * e_all[0:C])
        mixed_scr[buf] = (lax.dot_general(q_in, state_t.astype(BF16), NT_DIMS, preferred_element_type=F32)
                          + jnp.dot(a_bd.astype(BF16), v_st, preferred_element_type=F32))
        return new_state_t

    def finish(c, buf):
        r0 = pl.multiple_of(c * C, C)
        for hh in range(H):
            o_h = _rms(mixed_scr[buf, hh * C:(hh + 1) * C, :], gsub_ref[...])
            gate = z_ref[0, pl.ds(r0, C), hh * DV:(hh + 1) * DV].astype(F32)
            o_ref[0, pl.ds(r0, C), hh * DV:(hh + 1) * DV] = (o_h * gate).astype(o_ref.dtype)

    decays(0, 0)
    mixed_scr[1] = jnp.zeros(mixed_scr.shape[1:], F32)

    def trip(cc, carry):
        c0 = CHUNKS_PER_TRIP * cc
        state = state_scr[...]
        for u in range(CHUNKS_PER_TRIP):
            c = c0 + u
            decays(jnp.minimum(c + 1, n_chunks - 1), (u + 1) % 2)
            state = mix(c, u % 2, state)
            finish(jnp.maximum(c - 1, 0), (u + 1) % 2)
        state_scr[...] = state
        return carry

    lax.fori_loop(0, n_chunks // CHUNKS_PER_TRIP, trip, 0)
    finish(n_chunks - 1, 1)


def _gated_linear_attention(qb, kb, glog, vb, zb, g_sub_b, *, chunk, rows_per_step):
    B, S, b_qk = qb.shape
    b_v = vb.shape[-1]
    sum_sel, upper_rows, pair_bd, head_mask = _gla_tables(chunk, GLA_HEADS, GLA_DK)
    n_levels = upper_rows.shape[0]
    T = rows_per_step

    def row_spec(width):
        return pl.BlockSpec((1, T, width), lambda b, i: (b, i, 0))

    def const_spec(shape):
        return pl.BlockSpec(shape, lambda b, i: (0,) * len(shape))

    sum_sel2 = np.concatenate([sum_sel, sum_sel], axis=1)
    assert CHUNKS_PER_TRIP % 2 == 0 and (T // chunk) % CHUNKS_PER_TRIP == 0
    kern = functools.partial(_gla_kernel, chunk=chunk, n_chunks=T // chunk, n_levels=n_levels)
    return pl.pallas_call(
        kern,
        grid=(B, S // T),
        in_specs=[row_spec(b_qk), row_spec(b_qk), row_spec(b_qk), row_spec(b_v), row_spec(b_v),
                  const_spec((1, GLA_DV)), const_spec(sum_sel2.shape), const_spec(upper_rows.shape),
                  const_spec(pair_bd.shape), const_spec(head_mask.shape)],
        out_specs=row_spec(b_v),
        out_shape=jax.ShapeDtypeStruct((B, S, b_v), BF16),
        scratch_shapes=[pltpu.VMEM((GLA_DV, b_qk), F32),
                        pltpu.VMEM((2,) + (sum_sel.shape[0], b_qk), F32),
                        pltpu.VMEM((2, GLA_HEADS * chunk, GLA_DV), F32)],
        compiler_params=pltpu.CompilerParams(
            dimension_semantics=("parallel", "arbitrary"), vmem_limit_bytes=VMEM_LIMIT),
        name="gla",
    )(qb, kb, glog, vb, zb, g_sub_b.reshape(1, GLA_DV), jnp.asarray(sum_sel2, BF16),
      jnp.asarray(upper_rows), jnp.asarray(pair_bd), jnp.asarray(head_mask, BF16))


def _merge_kernel(x_ref, ua_ref, ub_ref, ga_ref, gb_ref, wua_ref, wub_ref, wout_ref, gpost_ref, o_ref):
    ya = jnp.dot(ua_ref[0], wua_ref[...], preferred_element_type=F32)
    yb = jnp.dot(ub_ref[0], wub_ref[...], preferred_element_type=F32)
    y = jax.nn.sigmoid(ga_ref[0].astype(F32)) * ya + jax.nn.sigmoid(gb_ref[0].astype(F32)) * yb
    out = jnp.dot(y.astype(BF16), wout_ref[...], preferred_element_type=F32)
    o_ref[0] = x_ref[0] + _rms(out, gpost_ref[...])


def _merge(x, ua, ub, ga, gb, w_up_a, w_up_b, w_out, g_post, *, tm):
    B, S, D = x.shape

    def row_spec(width):
        return pl.BlockSpec((1, tm, width), lambda b, i: (b, i, 0))

    def const_spec(shape):
        return pl.BlockSpec(shape, lambda b, i: (0,) * len(shape))

    return pl.pallas_call(
        _merge_kernel,
        grid=(B, S // tm),
        in_specs=[row_spec(D), row_spec(ua.shape[-1]), row_spec(ub.shape[-1]), row_spec(D), row_spec(D),
                  const_spec(w_up_a.shape), const_spec(w_up_b.shape), const_spec(w_out.shape),
                  const_spec((1, D))],
        out_specs=row_spec(D),
        out_shape=jax.ShapeDtypeStruct((B, S, D), x.dtype),
        compiler_params=pltpu.CompilerParams(
            dimension_semantics=("parallel", "parallel"), vmem_limit_bytes=VMEM_LIMIT),
        name="merge",
    )(x, ua, ub, ga, gb, w_up_a, w_up_b, w_out, g_post.reshape(1, D))


ROW_TILE = 512
ATTN_TILE = 256
GLA_CHUNK = 64
GLA_STEP_ROWS = 2048


def _layer(x, g_pre, w_in, lam_q1, lam_k1, lam_q2, lam_k2, g_sub_a, w_alpha, b_alpha, g_sub_b,
           w_up_a, w_up_b, w_out, g_post):
    n_mix = 2 * (DA_HEADS * 2 * DA_HEAD_DIM) + 2 * (DA_HEADS * DA_V_DIM) + 2 * (GLA_HEADS * GLA_DK) \
        + 2 * (GLA_HEADS * GLA_DV)
    gate0 = n_mix + GLA_GATE_RANK
    qa_t, ka, va_t, za, qb, kb, vb, zb, ga, gb, glog = _input_projection(
        x, g_pre, w_in.astype(BF16), w_in[:, gate0:].astype(BF16), w_alpha.astype(BF16), b_alpha,
        tm=ROW_TILE)
    ua = _diff_attention(qa_t, ka, va_t, za, lam_q1, lam_k1, lam_q2, lam_k2, g_sub_a, tq=ATTN_TILE)
    ub = _gated_linear_attention(qb, kb, glog, vb, zb, g_sub_b, chunk=GLA_CHUNK, rows_per_step=GLA_STEP_ROWS)
    return _merge(x, ua, ub, ga, gb, w_up_a.astype(BF16), w_up_b.astype(BF16), w_out.astype(BF16),
                  g_post, tm=ROW_TILE)


def kernel(x, g_pre, w_in, lam_q1, lam_k1, lam_q2, lam_k2, g_sub_a, w_alpha, b_alpha, g_sub_b, w_up_a, w_up_b, w_out, g_post):
    depth = w_in.shape[0]
    assert depth == 1, "LAM_INIT is specialised to a single layer"
    first = lambda p: p.reshape(p.shape[1:])
    return _layer(x, *(first(p) for p in (g_pre, w_in, lam_q1, lam_k1, lam_q2, lam_k2, g_sub_a, w_alpha,
                                          b_alpha, g_sub_b, w_up_a, w_up_b, w_out, g_post)))
```

```python
import functools
import math

import numpy as np
import jax
import jax.numpy as jnp
from jax import lax
from jax.experimental import pallas as pl
from jax.experimental.pallas import tpu as pltpu

F32 = jnp.float32
BF16 = jnp.bfloat16

DA_HEADS = 4
DA_HEAD_DIM = 64
DA_V_DIM = 128
GLA_HEADS = 4
GLA_DK = 64
GLA_DV = 128
GLA_GATE_RANK = 16
GLA_TAU = 16.0
RMS_EPS = 1e-6
LAYER_IDX = 0
LAM_INIT = 0.8 - 0.6 * math.exp(-0.3 * LAYER_IDX)

LOG2E = math.log2(math.e)
LANE = 128
ONES_ROWS = 16
SLOPE_PARTS = 3
VMEM_LIMIT = 56 * 1024 * 1024

NT_DIMS = (((1,), (1,)), ((), ()))


def _rms(x, g):
    return x * lax.rsqrt(jnp.mean(x * x, axis=-1, keepdims=True) + RMS_EPS) * g


def _silu(z):
    return z * jax.nn.sigmoid(z)


def _inproj_kernel(x_ref, gpre_ref, wmix_ref, wgate_ref, walpha_ref, balpha_ref,
                   qa_ref, ka_ref, va_ref, za_ref, qb_ref, kb_ref, vb_ref, zb_ref,
                   ga_ref, gb_ref, glog_ref, h_scr, t_scr, *, segments, lr_cols, feature_major,
                   q_scale_a, q_scale_b):
    outs = dict(qa=qa_ref, ka=ka_ref, va=va_ref, za=za_ref, qb=qb_ref, kb=kb_ref, vb=vb_ref,
                zb=zb_ref, ga=ga_ref, gb=gb_ref)
    weights = dict(mix=wmix_ref, gate=wgate_ref)
    scales = dict(qa=q_scale_a, qb=q_scale_b)
    h_scr[...] = _rms(x_ref[0], gpre_ref[...]).astype(BF16)
    for name, wname, c0, width, o0 in segments:
        acc = jnp.dot(h_scr[...], weights[wname][:, c0:c0 + width], preferred_element_type=F32)
        if name in scales:
            acc = acc * scales[name]
        if name in ("za", "zb"):
            acc = _silu(acc)
        if name in feature_major:
            t_scr[...] = acc
            outs[name][0, o0:o0 + width, :] = t_scr[...].T.astype(BF16)
        else:
            outs[name][0, :, o0:o0 + width] = acc.astype(BF16)
    lr = jnp.dot(h_scr[...], wmix_ref[:, lr_cols[0]:lr_cols[1]], preferred_element_type=F32)
    z = jnp.dot(lr.astype(BF16), walpha_ref[...], preferred_element_type=F32) + balpha_ref[...]
    log_sig = jnp.minimum(z, 0.0) - jnp.log1p(jnp.exp(-jnp.abs(z)))
    glog_ref[0] = log_sig / GLA_TAU


def _input_projection(x, g_pre, w_mix_lr, w_gate, w_alpha, b_alpha, *, tm):
    B, S, D = x.shape
    a_qk, a_v = DA_HEADS * 2 * DA_HEAD_DIM, DA_HEADS * DA_V_DIM
    b_qk, b_v = GLA_HEADS * GLA_DK, GLA_HEADS * GLA_DV
    widths = [("qa", "mix", a_qk), ("ka", "mix", a_qk), ("va", "mix", a_v), ("za", "mix", a_v),
              ("qb", "mix", b_qk), ("kb", "mix", b_qk), ("vb", "mix", b_v), ("zb", "mix", b_v),
              ("ga", "gate", D), ("gb", "gate", D)]
    max_chunk = 512
    segments = []
    col = dict(mix=0, gate=0)
    for name, wname, width in widths:
        for o0 in range(0, width, max_chunk):
            segments.append((name, wname, col[wname] + o0, min(max_chunk, width - o0), o0))
        col[wname] += width
    lr_cols = (col["mix"], col["mix"] + GLA_GATE_RANK)
    assert lr_cols[1] <= w_mix_lr.shape[1] and col["gate"] == w_gate.shape[1]

    def row_spec(width):
        return pl.BlockSpec((1, tm, width), lambda b, i: (b, i, 0))

    def const_spec(shape):
        return pl.BlockSpec(shape, lambda b, i: (0,) * len(shape), pipeline_mode=pl.Buffered(1))

    feature_major = ("qa", "va")
    out_shape, out_specs = [], []
    for name, _, w in widths:
        if name in feature_major:
            out_shape.append(jax.ShapeDtypeStruct((B, w, S), BF16))
            out_specs.append(pl.BlockSpec((1, w, tm), lambda b, i: (b, 0, i)))
        else:
            out_shape.append(jax.ShapeDtypeStruct((B, S, w), BF16))
            out_specs.append(row_spec(w))
    out_shape.append(jax.ShapeDtypeStruct((B, S, b_qk), F32))
    out_specs.append(row_spec(b_qk))
    kern = functools.partial(_inproj_kernel, segments=tuple(segments), lr_cols=lr_cols,
                             feature_major=feature_major,
                             q_scale_a=DA_HEAD_DIM ** -0.5 * LOG2E,
                             q_scale_b=GLA_DK ** -0.5)
    return pl.pallas_call(
        kern,
        grid=(B, S // tm),
        in_specs=[row_spec(D), const_spec((1, D)), const_spec(w_mix_lr.shape),
                  const_spec(w_gate.shape), const_spec(w_alpha.shape), const_spec((1, b_qk))],
        out_specs=out_specs,
        out_shape=out_shape,
        scratch_shapes=[pltpu.VMEM((tm, D), BF16), pltpu.VMEM((tm, max_chunk), F32)],
        compiler_params=pltpu.CompilerParams(
            dimension_semantics=("parallel", "parallel"), vmem_limit_bytes=VMEM_LIMIT),
        name="inproj",
    )(x, g_pre.reshape(1, D), w_mix_lr, w_gate, w_alpha, b_alpha.reshape(1, b_qk))


POSITIONS_PER_TRIP = 8


def _attn_kernel(lq1_ref, lk1_ref, lq2_ref, lk2_ref, q_ref, qn_ref, k_ref, vt_ref, za_ref, gsub_ref,
                 o_ref, qs_scr, ks_scr, vt_scr, causal_scr, st_scr, mx_scr, p_scr, alpha_scr, m_scr, acc_scr,
                 *, tq, tk):
    i = pl.program_id(1)
    H, DV = DA_HEADS, DA_V_DIM
    n_kv = vt_scr.shape[1]
    slope_parts, slopes = [], []
    for h in range(H):
        rest = np.float32(2.0 ** (-8.0 * (h + 1) / H) * LOG2E)
        slopes.append(float(rest))
        parts = []
        for _ in range(SLOPE_PARTS):
            part = np.asarray(rest, dtype=BF16)
            parts.append(float(part))
            rest = np.float32(rest - np.float32(part))
        assert rest == 0.0
        slope_parts.append(parts)

    def prepare_keys_values():
        lane = lax.broadcasted_iota(jnp.int32, (tk, LANE), 1)
        key = lax.broadcasted_iota(jnp.int32, (tk, LANE), 0).astype(F32)
        key_lanes = jnp.where(lane < SLOPE_PARTS, key, 0.0).astype(BF16)
        ones_rows = jnp.ones((ONES_ROWS, tk), BF16)
        def prep(j, carry):
            k0 = pl.multiple_of(j * tk, tk)
            for h in range(H):
                ks_scr[h, pl.ds(k0, tk), 0:LANE] = k_ref[0, pl.ds(k0, tk), h * LANE:(h + 1) * LANE]
                ks_scr[h, pl.ds(k0, tk), LANE:2 * LANE] = key_lanes
            return carry
        lax.fori_loop(0, n_kv, prep, 0)
        for j in range(n_kv):
            for h in range(H):
                vt_scr[h, j, 0:DV, :] = vt_ref[0, h * DV:(h + 1) * DV, j * tk:(j + 1) * tk]
                vt_scr[h, j, DV:DV + ONES_ROWS, :] = ones_rows
        key_idx = lax.broadcasted_iota(jnp.int32, (tk, 2 * tq), 0)
        qcol = lax.broadcasted_iota(jnp.int32, (tk, 2 * tq), 1)
        visible = key_idx <= jnp.where(qcol >= tq, qcol - tq, qcol)
        causal_scr[...] = jnp.where(visible, 0.0, -jnp.inf)

    def load_queries(ref):
        row = lax.broadcasted_iota(jnp.int32, (LANE, tq), 0)
        zero = jnp.zeros((LANE, tq), BF16)
        for h in range(H):
            qt = ref[0, h * LANE:(h + 1) * LANE, :]
            qs_scr[h, 0:LANE, 0:tq] = jnp.where(row < DA_HEAD_DIM, qt, zero)
            qs_scr[h, 0:LANE, tq:2 * tq] = jnp.where(row >= DA_HEAD_DIM, qt, zero)

    def load_slopes():
        row = lax.broadcasted_iota(jnp.int32, (LANE, 2 * tq), 0)
        for h in range(H):
            rows = jnp.zeros((LANE, 2 * tq), F32)
            for r, part in enumerate(slope_parts[h]):
                rows = jnp.where(row == r, part, rows)
            qs_scr[h, LANE:2 * LANE, :] = rows.astype(BF16)

    def produce(j, buf, masked=False):
        k0 = pl.multiple_of(j * tk, tk)
        for h in range(H):
            st = jnp.dot(ks_scr[h, pl.ds(k0, tk), :], qs_scr[h], preferred_element_type=F32)
            if masked:
                st = st + causal_scr[...]
            st_scr[buf, h] = st
            mx_scr[buf, h] = jnp.max(st, axis=0, keepdims=True)

    def softmax(j, buf):
        tile_dist = (j * tk - i * tq).astype(F32)
        for h in range(H):
            off = slopes[h] * tile_dist
            m_prev = m_scr[h]
            m_next = jnp.maximum(m_prev, mx_scr[buf, h] + off)
            alpha_scr[buf, h] = jnp.exp2(m_prev - m_next)
            p_scr[buf, h] = jnp.exp2(st_scr[buf, h] - (m_next - off)).astype(BF16)
            m_scr[h] = m_next

    def weighted_values(j, buf):
        for h in range(H):
            acc_scr[h] = alpha_scr[buf, h] * acc_scr[h] + jnp.dot(vt_scr[h, j], p_scr[buf, h],
                                                                  preferred_element_type=F32)

    def run(stage, base, offset):
        pos, buf = base + offset, offset % 2
        if offset > 0:
            tile = pos - 1
        else:
            tile = jnp.where(pos == 0, i, pos - 1)
        if stage == "produce":
            produce(tile, buf)
        elif stage == "softmax":
            softmax(tile, buf)
        else:
            weighted_values(tile, buf)

    def steady(base):
        for t in range(1, POSITIONS_PER_TRIP + 1):
            run("produce", base, t + 1)
            run("weighted_values", base, t - 1)
            run("softmax", base, t)

    def drain(base, r):
        for t in range(1, r + 1):
            if t + 1 <= r:
                run("produce", base, t + 1)
            run("weighted_values", base, t - 1)
            run("softmax", base, t)
        run("weighted_values", base, r)

    def reset():
        m_scr[...] = jnp.full(m_scr.shape, -jnp.inf, F32)
        acc_scr[...] = jnp.zeros(acc_scr.shape, F32)

    @pl.when(i == 0)
    def _():
        prepare_keys_values()
        load_slopes()
        load_queries(q_ref)
        produce(0, 0, masked=True)
        reset()
        softmax(0, 0)
        weighted_values(0, 0)

    @pl.when(i > 0)
    def _():
        reset()
        run("produce", 0, 1)
        run("softmax", 0, 0)

        def trip(t, carry):
            steady(POSITIONS_PER_TRIP * t)
            return carry
        n_trips = (i - 1) // POSITIONS_PER_TRIP
        lax.fori_loop(0, n_trips, trip, 0)

        done = POSITIONS_PER_TRIP * n_trips
        for r in range(1, POSITIONS_PER_TRIP + 1):
            pl.when(i - done == r)(functools.partial(drain, done, r))

    n_q = pl.num_programs(1)
    load_queries(qn_ref)
    produce(jnp.minimum(i + 1, n_q - 1), 0, masked=True)

    f = lambda r: r[...].astype(F32)
    lam = (jnp.exp(jnp.sum(f(lq1_ref) * f(lk1_ref), axis=1, keepdims=True))
           - jnp.exp(jnp.sum(f(lq2_ref) * f(lk2_ref), axis=1, keepdims=True)) + LAM_INIT)
    gain = gsub_ref[...] * (1.0 - LAM_INIT)
    for h in range(H):
        on = acc_scr[h, 0:DV, :] / acc_scr[h, DV:DV + 1, :]
        ot = on[:, 0:tq] - lam * on[:, tq:2 * tq]
        ot = ot * lax.rsqrt(jnp.mean(ot * ot, axis=0, keepdims=True) + RMS_EPS)
        o = ot.T * gain
        gate = za_ref[0, :, h * DV:(h + 1) * DV].astype(F32)
        o_ref[0, :, h * DV:(h + 1) * DV] = (o * gate).astype(o_ref.dtype)


def _diff_attention(qa_t, ka, va_t, za, lam_q1, lam_k1, lam_q2, lam_k2, g_sub_a, *, tq):
    B, S, W = ka.shape
    tk = tq
    assert W == DA_HEADS * LANE and S % tq == 0 and qa_t.shape == (B, W, S) and va_t.shape == (B, W, S)
    vec = lambda v: v.reshape(1, DA_HEAD_DIM)
    vec_spec = pl.BlockSpec((1, DA_HEAD_DIM), lambda b, i: (0, 0))
    tile_spec = pl.BlockSpec((1, tq, W), lambda b, i: (b, i, 0))
    seq_spec = pl.BlockSpec((1, S, W), lambda b, i: (b, 0, 0))
    n_q = S // tq
    q_spec = pl.BlockSpec((1, W, tq), lambda b, i: (b, 0, i))
    next_q_spec = pl.BlockSpec((1, W, tq), lambda b, i: (b, 0, jnp.minimum(i + 1, n_q - 1)))
    seq_t_spec = pl.BlockSpec((1, W, S), lambda b, i: (b, 0, 0))
    kern = functools.partial(_attn_kernel, tq=tq, tk=tk)
    return pl.pallas_call(
        kern,
        grid=(B, n_q),
        in_specs=[vec_spec, vec_spec, vec_spec, vec_spec, q_spec, next_q_spec, seq_spec, seq_t_spec,
                  tile_spec,
                  pl.BlockSpec((1, DA_V_DIM), lambda b, i: (0, 0))],
        out_specs=tile_spec,
        out_shape=jax.ShapeDtypeStruct((B, S, W), BF16),
        scratch_shapes=[pltpu.VMEM((DA_HEADS, 2 * LANE, 2 * tq), BF16),
                        pltpu.VMEM((DA_HEADS, S, 2 * LANE), BF16),
                        pltpu.VMEM((DA_HEADS, S // tk, DA_V_DIM + ONES_ROWS, tk), BF16),
                        pltpu.VMEM((tk, 2 * tq), F32),
                        pltpu.VMEM((2, DA_HEADS, tk, 2 * tq), F32),
                        pltpu.VMEM((2, DA_HEADS, 1, 2 * tq), F32),
                        pltpu.VMEM((2, DA_HEADS, tk, 2 * tq), BF16),
                        pltpu.VMEM((2, DA_HEADS, 1, 2 * tq), F32),
                        pltpu.VMEM((DA_HEADS, 1, 2 * tq), F32),
                        pltpu.VMEM((DA_HEADS, DA_V_DIM + ONES_ROWS, 2 * tq), F32)],
        compiler_params=pltpu.CompilerParams(
            dimension_semantics=("parallel", "arbitrary"), vmem_limit_bytes=VMEM_LIMIT),
        name="diffattn",
    )(vec(lam_q1), vec(lam_k1), vec(lam_q2), vec(lam_k2), qa_t, qa_t, ka, va_t, za,
      g_sub_a.reshape(1, DA_V_DIM))


def _gla_tables(C, n_heads, dk):
    t = np.arange(C)[:, None]
    u = np.arange(C)[None, :]
    sums = [(u <= t), (u > t)]
    halves = []
    m = C // 2
    while m >= 1:
        halves.append(m)
        m //= 2
    upper_rows, pair_masks = [], []
    for m in halves:
        blk = 2 * m
        mid = (t // blk) * blk + m
        upper = (t % blk) >= m
        sums.append(np.where(upper, (u >= mid) & (u <= t), (u > t) & (u <= mid - 1)))
        upper_rows.append(np.broadcast_to(upper, (C, n_heads * dk)))
        tt, ss = np.arange(C)[:, None], np.arange(C)[None, :]
        pair_masks.append(((tt // blk) == (ss // blk)) & ((tt % blk) >= m) & ((ss % blk) < m))
    pair_masks.append(np.eye(C, dtype=bool))
    sum_sel = np.concatenate(sums, axis=0).astype(np.float32)
    upper_rows = np.stack(upper_rows).astype(np.float32)
    eye_h = np.eye(n_heads, dtype=bool)
    pair_bd = np.stack([np.kron(eye_h, pm) for pm in pair_masks]).astype(np.float32)
    head_of_lane = np.arange(n_heads * dk)[None, :] // dk
    head_of_row = np.repeat(np.arange(n_heads), C)[:, None]
    head_mask = (head_of_lane == head_of_row).astype(np.float32)
    return sum_sel, upper_rows, pair_bd, head_mask


CHUNKS_PER_TRIP = 16


def _gla_kernel(q_ref, k_ref, g_ref, v_ref, z_ref, gsub_ref, sumsel_ref, upper_ref, pair_ref, hmask_ref,
                o_ref, state_scr, decay_scr, mixed_scr, *, chunk, n_chunks, n_levels):
    C = chunk
    H, DV = GLA_HEADS, GLA_DV

    @pl.when(pl.program_id(1) == 0)
    def _():
        state_scr[...] = jnp.zeros(state_scr.shape, F32)

    hmask = hmask_ref[...]

    def stack_heads(a):
        return jnp.concatenate([a.astype(BF16)] * H, axis=0) * hmask

    def decays(c, buf):
        r0 = pl.multiple_of(c * C, C)
        g = g_ref[0, pl.ds(r0, C), :]
        g_hi = g.astype(BF16)
        g_lo = (g - g_hi.astype(F32)).astype(BF16)
        g2 = jnp.concatenate([g_hi, g_lo], axis=0)
        d_all = jnp.dot(sumsel_ref[...], g2, preferred_element_type=F32)
        decay_scr[buf] = jnp.exp(d_all)

    def mix(c, buf, state_t):
        r0 = pl.multiple_of(c * C, C)
        q = q_ref[0, pl.ds(r0, C), :].astype(F32)
        k = k_ref[0, pl.ds(r0, C), :].astype(F32)
        v = v_ref[0, pl.ds(r0, C), :]
        v_st = jnp.concatenate([v[:, hh * DV:(hh + 1) * DV] for hh in range(H)], axis=0)
        v_st_t = v_st.astype(F32).T.astype(BF16)
        e_all = decay_scr.at[buf]

        k_out = stack_heads(k * e_all[C:2 * C])
        new_state_t = (e_all[C - 1:C] * state_t
                       + jnp.dot(v_st_t, k_out, preferred_element_type=F32))

        a_bd = None
        for lvl in range(n_levels + 1):
            if lvl < n_levels:
                e = e_all[(2 + lvl) * C:(3 + lvl) * C]
                xs = stack_heads(jnp.where(upper_ref[lvl] > 0.5, q, k) * e)
                prod = lax.dot_general(xs, xs, NT_DIMS, preferred_element_type=F32)
            else:
                prod = lax.dot_general(stack_heads(q), stack_heads(k), NT_DIMS, preferred_element_type=F32)
            term = prod * pair_ref[lvl]
            a_bd = term if a_bd is None else a_bd + term

        q_in = stack_heads(q * e_all[0:C])
        mixed_scr[buf] = (lax.dot_general(q_in, state_t.astype(BF16), NT_DIMS, preferred_element_type=F32)
                          + jnp.dot(a_bd.astype(BF16), v_st, preferred_element_type=F32))
        return new_state_t

    def finish(c, buf):
        r0 = pl.multiple_of(c * C, C)
        for hh in range(H):
            o_h = _rms(mixed_scr[buf, hh * C:(hh + 1) * C, :], gsub_ref[...])
            gate = z_ref[0, pl.ds(r0, C), hh * DV:(hh + 1) * DV].astype(F32)
            o_ref[0, pl.ds(r0, C), hh * DV:(hh + 1) * DV] = (o_h * gate).astype(o_ref.dtype)

    decays(0, 0)
    mixed_scr[1] = jnp.zeros(mixed_scr.shape[1:], F32)

    def trip(cc, carry):
        c0 = CHUNKS_PER_TRIP * cc
        state = state_scr[...]
        for u in range(CHUNKS_PER_TRIP):
            c = c0 + u
            decays(jnp.minimum(c + 1, n_chunks - 1), (u + 1) % 2)
            state = mix(c, u % 2, state)
            finish(jnp.maximum(c - 1, 0), (u + 1) % 2)
        state_scr[...] = state
        return carry

    lax.fori_loop(0, n_chunks // CHUNKS_PER_TRIP, trip, 0)
    finish(n_chunks - 1, 1)


def _gated_linear_attention(qb, kb, glog, vb, zb, g_sub_b, *, chunk, rows_per_step):
    B, S, b_qk = qb.shape
    b_v = vb.shape[-1]
    sum_sel, upper_rows, pair_bd, head_mask = _gla_tables(chunk, GLA_HEADS, GLA_DK)
    n_levels = upper_rows.shape[0]
    T = rows_per_step

    def row_spec(width):
        return pl.BlockSpec((1, T, width), lambda b, i: (b, i, 0))

    def const_spec(shape):
        return pl.BlockSpec(shape, lambda b, i: (0,) * len(shape))

    sum_sel2 = np.concatenate([sum_sel, sum_sel], axis=1)
    assert CHUNKS_PER_TRIP % 2 == 0 and (T // chunk) % CHUNKS_PER_TRIP == 0
    kern = functools.partial(_gla_kernel, chunk=chunk, n_chunks=T // chunk, n_levels=n_levels)
    return pl.pallas_call(
        kern,
        grid=(B, S // T),
        in_specs=[row_spec(b_qk), row_spec(b_qk), row_spec(b_qk), row_spec(b_v), row_spec(b_v),
                  const_spec((1, GLA_DV)), const_spec(sum_sel2.shape), const_spec(upper_rows.shape),
                  const_spec(pair_bd.shape), const_spec(head_mask.shape)],
        out_specs=row_spec(b_v),
        out_shape=jax.ShapeDtypeStruct((B, S, b_v), BF16),
        scratch_shapes=[pltpu.VMEM((GLA_DV, b_qk), F32),
                        pltpu.VMEM((2,) + (sum_sel.shape[0], b_qk), F32),
                        pltpu.VMEM((2, GLA_HEADS * chunk, GLA_DV), F32)],
        compiler_params=pltpu.CompilerParams(
            dimension_semantics=("parallel", "arbitrary"), vmem_limit_bytes=VMEM_LIMIT),
        name="gla",
    )(qb, kb, glog, vb, zb, g_sub_b.reshape(1, GLA_DV), jnp.asarray(sum_sel2, BF16),
      jnp.asarray(upper_rows), jnp.asarray(pair_bd), jnp.asarray(head_mask, BF16))


def _merge_kernel(x_ref, ua_ref, ub_ref, ga_ref, gb_ref, wua_ref, wub_ref, wout_ref, gpost_ref, o_ref):
    ya = jnp.dot(ua_ref[0], wua_ref[...], preferred_element_type=F32)
    yb = jnp.dot(ub_ref[0], wub_ref[...], preferred_element_type=F32)
    y = jax.nn.sigmoid(ga_ref[0].astype(F32)) * ya + jax.nn.sigmoid(gb_ref[0].astype(F32)) * yb
    out = jnp.dot(y.astype(BF16), wout_ref[...], preferred_element_type=F32)
    o_ref[0] = x_ref[0] + _rms(out, gpost_ref[...])


def _merge(x, ua, ub, ga, gb, w_up_a, w_up_b, w_out, g_post, *, tm):
    B, S, D = x.shape

    def row_spec(width):
        return pl.BlockSpec((1, tm, width), lambda b, i: (b, i, 0))

    def const_spec(shape):
        return pl.BlockSpec(shape, lambda b, i: (0,) * len(shape))

    return pl.pallas_call(
        _merge_kernel,
        grid=(B, S // tm),
        in_specs=[row_spec(D), row_spec(ua.shape[-1]), row_spec(ub.shape[-1]), row_spec(D), row_spec(D),
                  const_spec(w_up_a.shape), const_spec(w_up_b.shape), const_spec(w_out.shape),
                  const_spec((1, D))],
        out_specs=row_spec(D),
        out_shape=jax.ShapeDtypeStruct((B, S, D), x.dtype),
        compiler_params=pltpu.CompilerParams(
            dimension_semantics=("parallel", "parallel"), vmem_limit_bytes=VMEM_LIMIT),
        name="merge",
    )(x, ua, ub, ga, gb, w_up_a, w_up_b, w_out, g_post.reshape(1, D))


ROW_TILE = 512
ATTN_TILE = 256
GLA_CHUNK = 64
GLA_STEP_ROWS = 2048


def _layer(x, g_pre, w_in, lam_q1, lam_k1, lam_q2, lam_k2, g_sub_a, w_alpha, b_alpha, g_sub_b,
           w_up_a, w_up_b, w_out, g_post):
    n_mix = 2 * (DA_HEADS * 2 * DA_HEAD_DIM) + 2 * (DA_HEADS * DA_V_DIM) + 2 * (GLA_HEADS * GLA_DK) \
        + 2 * (GLA_HEADS * GLA_DV)
    gate0 = n_mix + GLA_GATE_RANK
    qa_t, ka, va_t, za, qb, kb, vb, zb, ga, gb, glog = _input_projection(
        x, g_pre, w_in.astype(BF16), w_in[:, gate0:].astype(BF16), w_alpha.astype(BF16), b_alpha,
        tm=ROW_TILE)
    ua = _diff_attention(qa_t, ka, va_t, za, lam_q1, lam_k1, lam_q2, lam_k2, g_sub_a, tq=ATTN_TILE)
    ub = _gated_linear_attention(qb, kb, glog, vb, zb, g_sub_b, chunk=GLA_CHUNK, rows_per_step=GLA_STEP_ROWS)
    return _merge(x, ua, ub, ga, gb, w_up_a.astype(BF16), w_up_b.astype(BF16), w_out.astype(BF16),
                  g_post, tm=ROW_TILE)


def kernel(x, g_pre, w_in, lam_q1, lam_k1, lam_q2, lam_k2, g_sub_a, w_alpha, b_alpha, g_sub_b, w_up_a, w_up_b, w_out, g_post):
    depth = w_in.shape[0]
    assert depth == 1, "LAM_INIT is specialised to a single layer"
    first = lambda p: p.reshape(p.shape[1:])
    return _layer(x, *(first(p) for p in (g_pre, w_in, lam_q1, lam_k1, lam_q2, lam_k2, g_sub_a, w_alpha,
                                          b_alpha, g_sub_b, w_up_a, w_up_b, w_out, g_post)))
```

```python
import functools
import math

import numpy as np
import jax
import jax.numpy as jnp
from jax import lax
from jax.experimental import pallas as pl
from jax.experimental.pallas import tpu as pltpu

F32 = jnp.float32
BF16 = jnp.bfloat16

DA_HEADS = 4
DA_HEAD_DIM = 64
DA_V_DIM = 128
GLA_HEADS = 4
GLA_DK = 64
GLA_DV = 128
GLA_GATE_RANK = 16
GLA_TAU = 16.0
RMS_EPS = 1e-6
LAYER_IDX = 0
LAM_INIT = 0.8 - 0.6 * math.exp(-0.3 * LAYER_IDX)

LOG2E = math.log2(math.e)
LANE = 128
ONES_ROWS = 16
SLOPE_PARTS = 3
VMEM_LIMIT = 56 * 1024 * 1024

NT_DIMS = (((1,), (1,)), ((), ()))


def _rms(x, g):
    return x * lax.rsqrt(jnp.mean(x * x, axis=-1, keepdims=True) + RMS_EPS) * g


def _silu(z):
    return z * jax.nn.sigmoid(z)


def _inproj_kernel(x_ref, gpre_ref, wmix_ref, wgate_ref, walpha_ref, balpha_ref,
                   qa_ref, ka_ref, va_ref, za_ref, qb_ref, kb_ref, vb_ref, zb_ref,
                   ga_ref, gb_ref, glog_ref, h_scr, t_scr, *, segments, lr_cols, feature_major,
                   q_scale_a, q_scale_b):
    outs = dict(qa=qa_ref, ka=ka_ref, va=va_ref, za=za_ref, qb=qb_ref, kb=kb_ref, vb=vb_ref,
                zb=zb_ref, ga=ga_ref, gb=gb_ref)
    weights = dict(mix=wmix_ref, gate=wgate_ref)
    scales = dict(qa=q_scale_a, qb=q_scale_b)
    h_scr[...] = _rms(x_ref[0], gpre_ref[...]).astype(BF16)
    for name, wname, c0, width, o0 in segments:
        acc = jnp.dot(h_scr[...], weights[wname][:, c0:c0 + width], preferred_element_type=F32)
        if name in scales:
            acc = acc * scales[name]
        if name in ("za", "zb"):
            acc = _silu(acc)
        if name in feature_major:
            t_scr[...] = acc
            outs[name][0, o0:o0 + width, :] = t_scr[...].T.astype(BF16)
        else:
            outs[name][0, :, o0:o0 + width] = acc.astype(BF16)
    lr = jnp.dot(h_scr[...], wmix_ref[:, lr_cols[0]:lr_cols[1]], preferred_element_type=F32)
    z = jnp.dot(lr.astype(BF16), walpha_ref[...], preferred_element_type=F32) + balpha_ref[...]
    log_sig = jnp.minimum(z, 0.0) - jnp.log1p(jnp.exp(-jnp.abs(z)))
    glog_ref[0] = log_sig / GLA_TAU


def _input_projection(x, g_pre, w_mix_lr, w_gate, w_alpha, b_alpha, *, tm):
    B, S, D = x.shape
    a_qk, a_v = DA_HEADS * 2 * DA_HEAD_DIM, DA_HEADS * DA_V_DIM
    b_qk, b_v = GLA_HEADS * GLA_DK, GLA_HEADS * GLA_DV
    widths = [("qa", "mix", a_qk), ("ka", "mix", a_qk), ("va", "mix", a_v), ("za", "mix", a_v),
              ("qb", "mix", b_qk), ("kb", "mix", b_qk), ("vb", "mix", b_v), ("zb", "mix", b_v),
              ("ga", "gate", D), ("gb", "gate", D)]
    max_chunk = 512
    segments = []
    col = dict(mix=0, gate=0)
    for name, wname, width in widths:
        for o0 in range(0, width, max_chunk):
            segments.append((name, wname, col[wname] + o0, min(max_chunk, width - o0), o0))
        col[wname] += width
    lr_cols = (col["mix"], col["mix"] + GLA_GATE_RANK)
    assert lr_cols[1] <= w_mix_lr.shape[1] and col["gate"] == w_gate.shape[1]

    def row_spec(width):
        return pl.BlockSpec((1, tm, width), lambda b, i: (b, i, 0))

    def const_spec(shape):
        return pl.BlockSpec(shape, lambda b, i: (0,) * len(shape), pipeline_mode=pl.Buffered(1))

    feature_major = ("qa", "va")
    out_shape, out_specs = [], []
    for name, _, w in widths:
        if name in feature_major:
            out_shape.append(jax.ShapeDtypeStruct((B, w, S), BF16))
            out_specs.append(pl.BlockSpec((1, w, tm), lambda b, i: (b, 0, i)))
        else:
            out_shape.append(jax.ShapeDtypeStruct((B, S, w), BF16))
            out_specs.append(row_spec(w))
    out_shape.append(jax.ShapeDtypeStruct((B, S, b_qk), F32))
    out_specs.append(row_spec(b_qk))
    kern = functools.partial(_inproj_kernel, segments=tuple(segments), lr_cols=lr_cols,
                             feature_major=feature_major,
                             q_scale_a=DA_HEAD_DIM ** -0.5 * LOG2E,
                             q_scale_b=GLA_DK ** -0.5)
    return pl.pallas_call(
        kern,
        grid=(B, S // tm),
        in_specs=[row_spec(D), const_spec((1, D)), const_spec(w_mix_lr.shape),
                  const_spec(w_gate.shape), const_spec(w_alpha.shape), const_spec((1, b_qk))],
        out_specs=out_specs,
        out_shape=out_shape,
        scratch_shapes=[pltpu.VMEM((tm, D), BF16), pltpu.VMEM((tm, max_chunk), F32)],
        compiler_params=pltpu.CompilerParams(
            dimension_semantics=("parallel", "parallel"), vmem_limit_bytes=VMEM_LIMIT),
        name="inproj",
    )(x, g_pre.reshape(1, D), w_mix_lr, w_gate, w_alpha, b_alpha.reshape(1, b_qk))


POSITIONS_PER_TRIP = 8


def _attn_kernel(lq1_ref, lk1_ref, lq2_ref, lk2_ref, q_ref, qn_ref, k_ref, vt_ref, za_ref, gsub_ref,
                 o_ref, qs_scr, ks_scr, vt_scr, causal_scr, st_scr, mx_scr, p_scr, alpha_scr, m_scr, acc_scr,
                 *, tq, tk):
    i = pl.program_id(1)
    H, DV = DA_HEADS, DA_V_DIM
    n_kv = vt_scr.shape[1]
    slope_parts, slopes = [], []
    for h in range(H):
        rest = np.float32(2.0 ** (-8.0 * (h + 1) / H) * LOG2E)
        slopes.append(float(rest))
        parts = []
        for _ in range(SLOPE_PARTS):
            part = np.asarray(rest, dtype=BF16)
            parts.append(float(part))
            rest = np.float32(rest - np.float32(part))
        assert rest == 0.0
        slope_parts.append(parts)

    def prepare_keys_values():
        lane = lax.broadcasted_iota(jnp.int32, (tk, LANE), 1)
        key = lax.broadcasted_iota(jnp.int32, (tk, LANE), 0).astype(F32)
        key_lanes = jnp.where(lane < SLOPE_PARTS, key, 0.0).astype(BF16)
        ones_rows = jnp.ones((ONES_ROWS, tk), BF16)
        def prep(j, carry):
            k0 = pl.multiple_of(j * tk, tk)
            for h in range(H):
                ks_scr[h, pl.ds(k0, tk), 0:LANE] = k_ref[0, pl.ds(k0, tk), h * LANE:(h + 1) * LANE]
                ks_scr[h, pl.ds(k0, tk), LANE:2 * LANE] = key_lanes
            return carry
        lax.fori_loop(0, n_kv, prep, 0)
        for j in range(n_kv):
            for h in range(H):
                vt_scr[h, j, 0:DV, :] = vt_ref[0, h * DV:(h + 1) * DV, j * tk:(j + 1) * tk]
                vt_scr[h, j, DV:DV + ONES_ROWS, :] = ones_rows
        key_idx = lax.broadcasted_iota(jnp.int32, (tk, 2 * tq), 0)
        qcol = lax.broadcasted_iota(jnp.int32, (tk, 2 * tq), 1)
        visible = key_idx <= jnp.where(qcol >= tq, qcol - tq, qcol)
        causal_scr[...] = jnp.where(visible, 0.0, -jnp.inf)

    def load_queries(ref):
        row = lax.broadcasted_iota(jnp.int32, (LANE, tq), 0)
        zero = jnp.zeros((LANE, tq), BF16)
        for h in range(H):
            qt = ref[0, h * LANE:(h + 1) * LANE, :]
            qs_scr[h, 0:LANE, 0:tq] = jnp.where(row < DA_HEAD_DIM, qt, zero)
            qs_scr[h, 0:LANE, tq:2 * tq] = jnp.where(row >= DA_HEAD_DIM, qt, zero)

    def load_slopes():
        row = lax.broadcasted_iota(jnp.int32, (LANE, 2 * tq), 0)
        for h in range(H):
            rows = jnp.zeros((LANE, 2 * tq), F32)
            for r, part in enumerate(slope_parts[h]):
                rows = jnp.where(row == r, part, rows)
            qs_scr[h, LANE:2 * LANE, :] = rows.astype(BF16)

    def produce(j, buf, masked=False):
        k0 = pl.multiple_of(j * tk, tk)
        for h in range(H):
            st = jnp.dot(ks_scr[h, pl.ds(k0, tk), :], qs_scr[h], preferred_element_type=F32)
            if masked:
                st = st + causal_scr[...]
            st_scr[buf, h] = st
            mx_scr[buf, h] = jnp.max(st, axis=0, keepdims=True)

    def softmax(j, buf):
        tile_dist = (j * tk - i * tq).astype(F32)
        for h in range(H):
            off = slopes[h] * tile_dist
            m_prev = m_scr[h]
            m_next = jnp.maximum(m_prev, mx_scr[buf, h] + off)
            alpha_scr[buf, h] = jnp.exp2(m_prev - m_next)
            p_scr[buf, h] = jnp.exp2(st_scr[buf, h] - (m_next - off)).astype(BF16)
            m_scr[h] = m_next

    def weighted_values(j, buf):
        for h in range(H):
            acc_scr[h] = alpha_scr[buf, h] * acc_scr[h] + jnp.dot(vt_scr[h, j], p_scr[buf, h],
                                                                  preferred_element_type=F32)

    def run(stage, base, offset):
        pos, buf = base + offset, offset % 2
        if offset > 0:
            tile = pos - 1
        else:
            tile = jnp.where(pos == 0, i, pos - 1)
        if stage == "produce":
            produce(tile, buf)
        elif stage == "softmax":
            softmax(tile, buf)
        else:
            weighted_values(tile, buf)

    def steady(base):
        for t in range(1, POSITIONS_PER_TRIP + 1):
            run("produce", base, t + 1)
            run("weighted_values", base, t - 1)
            run("softmax", base, t)

    def drain(base, r):
        for t in range(1, r + 1):
            if t + 1 <= r:
                run("produce", base, t + 1)
            run("weighted_values", base, t - 1)
            run("softmax", base, t)
        run("weighted_values", base, r)

    def reset():
        m_scr[...] = jnp.full(m_scr.shape, -jnp.inf, F32)
        acc_scr[...] = jnp.zeros(acc_scr.shape, F32)

    @pl.when(i == 0)
    def _():
        prepare_keys_values()
        load_slopes()
        load_queries(q_ref)
        produce(0, 0, masked=True)
        reset()
        softmax(0, 0)
        weighted_values(0, 0)

    @pl.when(i > 0)
    def _():
        reset()
        run("produce", 0, 1)
        run("softmax", 0, 0)

        def trip(t, carry):
            steady(POSITIONS_PER_TRIP * t)
            return carry
        n_trips = (i - 1) // POSITIONS_PER_TRIP
        lax.fori_loop(0, n_trips, trip, 0)

        done = POSITIONS_PER_TRIP * n_trips
        for r in range(1, POSITIONS_PER_TRIP + 1):
            pl.when(i - done == r)(functools.partial(drain, done, r))

    n_q = pl.num_programs(1)
    load_queries(qn_ref)
    produce(jnp.minimum(i + 1, n_q - 1), 0, masked=True)

    f = lambda r: r[...].astype(F32)
    lam = (jnp.exp(jnp.sum(f(lq1_ref) * f(lk1_ref), axis=1, keepdims=True))
           - jnp.exp(jnp.sum(f(lq2_ref) * f(lk2_ref), axis=1, keepdims=True)) + LAM_INIT)
    gain = gsub_ref[...] * (1.0 - LAM_INIT)
    for h in range(H):
        on = acc_scr[h, 0:DV, :] / acc_scr[h, DV:DV + 1, :]
        ot = on[:, 0:tq] - lam * on[:, tq:2 * tq]
        ot = ot * lax.rsqrt(jnp.mean(ot * ot, axis=0, keepdims=True) + RMS_EPS)
        o = ot.T * gain
        gate = za_ref[0, :, h * DV:(h + 1) * DV].astype(F32)
        o_ref[0, :, h * DV:(h + 1) * DV] = (o * gate).astype(o_ref.dtype)


def _diff_attention(qa_t, ka, va_t, za, lam_q1, lam_k1, lam_q2, lam_k2, g_sub_a, *, tq):
    B, S, W = ka.shape
    tk = tq
    assert W == DA_HEADS * LANE and S % tq == 0 and qa_t.shape == (B, W, S) and va_t.shape == (B, W, S)
    vec = lambda v: v.reshape(1, DA_HEAD_DIM)
    vec_spec = pl.BlockSpec((1, DA_HEAD_DIM), lambda b, i: (0, 0))
    tile_spec = pl.BlockSpec((1, tq, W), lambda b, i: (b, i, 0))
    seq_spec = pl.BlockSpec((1, S, W), lambda b, i: (b, 0, 0))
    n_q = S // tq
    q_spec = pl.BlockSpec((1, W, tq), lambda b, i: (b, 0, i))
    next_q_spec = pl.BlockSpec((1, W, tq), lambda b, i: (b, 0, jnp.minimum(i + 1, n_q - 1)))
    seq_t_spec = pl.BlockSpec((1, W, S), lambda b, i: (b, 0, 0))
    kern = functools.partial(_attn_kernel, tq=tq, tk=tk)
    return pl.pallas_call(
        kern,
        grid=(B, n_q),
        in_specs=[vec_spec, vec_spec, vec_spec, vec_spec, q_spec, next_q_spec, seq_spec, seq_t_spec,
                  tile_spec,
                  pl.BlockSpec((1, DA_V_DIM), lambda b, i: (0, 0))],
        out_specs=tile_spec,
        out_shape=jax.ShapeDtypeStruct((B, S, W), BF16),
        scratch_shapes=[pltpu.VMEM((DA_HEADS, 2 * LANE, 2 * tq), BF16),
                        pltpu.VMEM((DA_HEADS, S, 2 * LANE), BF16),
                        pltpu.VMEM((DA_HEADS, S // tk, DA_V_DIM + ONES_ROWS, tk), BF16),
                        pltpu.VMEM((tk, 2 * tq), F32),
                        pltpu.VMEM((2, DA_HEADS, tk, 2 * tq), F32),
                        pltpu.VMEM((2, DA_HEADS, 1, 2 * tq), F32),
                        pltpu.VMEM((2, DA_HEADS, tk, 2 * tq), BF16),
                        pltpu.VMEM((2, DA_HEADS, 1, 2 * tq), F32),
                        pltpu.VMEM((DA_HEADS, 1, 2 * tq), F32),
                        pltpu.VMEM((DA_HEADS, DA_V_DIM + ONES_ROWS, 2 * tq), F32)],
        compiler_params=pltpu.CompilerParams(
            dimension_semantics=("parallel", "arbitrary"), vmem_limit_bytes=VMEM_LIMIT),
        name="diffattn",
    )(vec(lam_q1), vec(lam_k1), vec(lam_q2), vec(lam_k2), qa_t, qa_t, ka, va_t, za,
      g_sub_a.reshape(1, DA_V_DIM))


def _gla_tables(C, n_heads, dk):
    t = np.arange(C)[:, None]
    u = np.arange(C)[None, :]
    sums = [(u <= t), (u > t)]
    halves = []
    m = C // 2
    while m >= 1:
        halves.append(m)
        m //= 2
    upper_rows, pair_masks = [], []
    for m in halves:
        blk = 2 * m
        mid = (t // blk) * blk + m
        upper = (t % blk) >= m
        sums.append(np.where(upper, (u >= mid) & (u <= t), (u > t) & (u <= mid - 1)))
        upper_rows.append(np.broadcast_to(upper, (C, n_heads * dk)))
        tt, ss = np.arange(C)[:, None], np.arange(C)[None, :]
        pair_masks.append(((tt // blk) == (ss // blk)) & ((tt % blk) >= m) & ((ss % blk) < m))
    pair_masks.append(np.eye(C, dtype=bool))
    sum_sel = np.concatenate(sums, axis=0).astype(np.float32)
    upper_rows = np.stack(upper_rows).astype(np.float32)
    eye_h = np.eye(n_heads, dtype=bool)
    pair_bd = np.stack([np.kron(eye_h, pm) for pm in pair_masks]).astype(np.float32)
    head_of_lane = np.arange(n_heads * dk)[None, :] // dk
    head_of_row = np.repeat(np.arange(n_heads), C)[:, None]
    head_mask = (head_of_lane == head_of_row).astype(np.float32)
    return sum_sel, upper_rows, pair_bd, head_mask


CHUNKS_PER_TRIP = 16


def _gla_kernel(q_ref, k_ref, g_ref, v_ref, z_ref, gsub_ref, sumsel_ref, upper_ref, pair_ref, hmask_ref,
                o_ref, state_scr, decay_scr, mixed_scr, *, chunk, n_chunks, n_levels):
    C = chunk
    H, DV = GLA_HEADS, GLA_DV

    @pl.when(pl.program_id(1) == 0)
    def _():
        state_scr[...] = jnp.zeros(state_scr.shape, F32)

    hmask = hmask_ref[...]

    def stack_heads(a):
        return jnp.concatenate([a.astype(BF16)] * H, axis=0) * hmask

    def decays(c, buf):
        r0 = pl.multiple_of(c * C, C)
        g = g_ref[0, pl.ds(r0, C), :]
        g_hi = g.astype(BF16)
        g_lo = (g - g_hi.astype(F32)).astype(BF16)
        g2 = jnp.concatenate([g_hi, g_lo], axis=0)
        d_all = jnp.dot(sumsel_ref[...], g2, preferred_element_type=F32)
        decay_scr[buf] = jnp.exp(d_all)

    def mix(c, buf, state_t):
        r0 = pl.multiple_of(c * C, C)
        q = q_ref[0, pl.ds(r0, C), :].astype(F32)
        k = k_ref[0, pl.ds(r0, C), :].astype(F32)
        v = v_ref[0, pl.ds(r0, C), :]
        v_st = jnp.concatenate([v[:, hh * DV:(hh + 1) * DV] for hh in range(H)], axis=0)
        v_st_t = v_st.astype(F32).T.astype(BF16)
        e_all = decay_scr.at[buf]

        k_out = stack_heads(k * e_all[C:2 * C])
        new_state_t = (e_all[C - 1:C] * state_t
                       + jnp.dot(v_st_t, k_out, preferred_element_type=F32))

        a_bd = None
        for lvl in range(n_levels + 1):
            if lvl < n_levels:
                e = e_all[(2 + lvl) * C:(3 + lvl) * C]
                xs = stack_heads(jnp.where(upper_ref[lvl] > 0.5, q, k) * e)
                prod = lax.dot_general(xs, xs, NT_DIMS, preferred_element_type=F32)
            else:
                prod = lax.dot_general(stack_heads(q), stack_heads(k), NT_DIMS, preferred_element_type=F32)
            term = prod * pair_ref[lvl]
            a_bd = term if a_bd is None else a_bd + term

        q_in = stack_heads(q * e_all[0:C])
        mixed_scr[buf] = (lax.dot_general(q_in, state_t.astype(BF16), NT_DIMS, preferred_element_type=F32)
                          + jnp.dot(a_bd.astype(BF16), v_st, preferred_element_type=F32))
        return new_state_t

    def finish(c, buf):
        r0 = pl.multiple_of(c * C, C)
        for hh in range(H):
            o_h = _rms(mixed_scr[buf, hh * C:(hh + 1) * C, :], gsub_ref[...])
            gate = z_ref[0, pl.ds(r0, C), hh * DV:(hh + 1) * DV].astype(F32)
            o_ref[0, pl.ds(r0, C), hh * DV:(hh + 1) * DV] = (o_h * gate).astype(o_ref.dtype)

    decays(0, 0)
    mixed_scr[1] = jnp.zeros(mixed_scr.shape[1:], F32)

    def trip(cc, carry):
        c0 = CHUNKS_PER_TRIP * cc
        state = state_scr[...]
        for u in range(CHUNKS_PER_TRIP):
            c = c0 + u
            decays(jnp.minimum(c + 1, n_chunks - 1), (u + 1) % 2)
            state = mix(c, u % 2, state)
            finish(jnp.maximum(c - 1, 0), (u + 1) % 2)
        state_scr[...] = state
        return carry

    lax.fori_loop(0, n_chunks // CHUNKS_PER_TRIP, trip, 0)
    finish(n_chunks - 1, 1)


def _gated_linear_attention(qb, kb, glog, vb, zb, g_sub_b, *, chunk, rows_per_step):
    B, S, b_qk = qb.shape
    b_v = vb.shape[-1]
    sum_sel, upper_rows, pair_bd, head_mask = _gla_tables(chunk, GLA_HEADS, GLA_DK)
    n_levels = upper_rows.shape[0]
    T = rows_per_step

    def row_spec(width):
        return pl.BlockSpec((1, T, width), lambda b, i: (b, i, 0))

    def const_spec(shape):
        return pl.BlockSpec(shape, lambda b, i: (0,) * len(shape))

    sum_sel2 = np.concatenate([sum_sel, sum_sel], axis=1)
    assert CHUNKS_PER_TRIP % 2 == 0 and (T // chunk) % CHUNKS_PER_TRIP == 0
    kern = functools.partial(_gla_kernel, chunk=chunk, n_chunks=T // chunk, n_levels=n_levels)
    return pl.pallas_call(
        kern,
        grid=(B, S // T),
        in_specs=[row_spec(b_qk), row_spec(b_qk), row_spec(b_qk), row_spec(b_v), row_spec(b_v),
                  const_spec((1, GLA_DV)), const_spec(sum_sel2.shape), const_spec(upper_rows.shape),
                  const_spec(pair_bd.shape), const_spec(head_mask.shape)],
        out_specs=row_spec(b_v),
        out_shape=jax.ShapeDtypeStruct((B, S, b_v), BF16),
        scratch_shapes=[pltpu.VMEM((GLA_DV, b_qk), F32),
                        pltpu.VMEM((2,) + (sum_sel.shape[0], b_qk), F32),
                        pltpu.VMEM((2, GLA_HEADS * chunk, GLA_DV), F32)],
        compiler_params=pltpu.CompilerParams(
            dimension_semantics=("parallel", "arbitrary"), vmem_limit_bytes=VMEM_LIMIT),
        name="gla",
    )(qb, kb, glog, vb, zb, g_sub_b.reshape(1, GLA_DV), jnp.asarray(sum_sel2, BF16),
      jnp.asarray(upper_rows), jnp.asarray(pair_bd), jnp.asarray(head_mask, BF16))


def _merge_kernel(x_ref, ua_ref, ub_ref, ga_ref, gb_ref, wua_ref, wub_ref, wout_ref, gpost_ref, o_ref, *,
                  sub_rows):
    for r0 in range(0, x_ref.shape[1], sub_rows):
        rows = slice(r0, r0 + sub_rows)
        ya = jnp.dot(ua_ref[0, rows, :], wua_ref[...], preferred_element_type=F32)
        yb = jnp.dot(ub_ref[0, rows, :], wub_ref[...], preferred_element_type=F32)
        y = (jax.nn.sigmoid(ga_ref[0, rows, :].astype(F32)) * ya
             + jax.nn.sigmoid(gb_ref[0, rows, :].astype(F32)) * yb)
        out = jnp.dot(y.astype(BF16), wout_ref[...], preferred_element_type=F32)
        o_ref[0, rows, :] = x_ref[0, rows, :] + _rms(out, gpost_ref[...])


def _merge(x, ua, ub, ga, gb, w_up_a, w_up_b, w_out, g_post, *, tm, sub_rows):
    B, S, D = x.shape
    assert tm % sub_rows == 0

    def row_spec(width):
        return pl.BlockSpec((1, tm, width), lambda b, i: (b, i, 0))

    def const_spec(shape):
        return pl.BlockSpec(shape, lambda b, i: (0,) * len(shape), pipeline_mode=pl.Buffered(1))

    return pl.pallas_call(
        functools.partial(_merge_kernel, sub_rows=sub_rows),
        grid=(B, S // tm),
        in_specs=[row_spec(D), row_spec(ua.shape[-1]), row_spec(ub.shape[-1]), row_spec(D), row_spec(D),
                  const_spec(w_up_a.shape), const_spec(w_up_b.shape), const_spec(w_out.shape),
                  const_spec((1, D))],
        out_specs=row_spec(D),
        out_shape=jax.ShapeDtypeStruct((B, S, D), x.dtype),
        compiler_params=pltpu.CompilerParams(
            dimension_semantics=("parallel", "parallel"), vmem_limit_bytes=VMEM_LIMIT),
        name="merge",
    )(x, ua, ub, ga, gb, w_up_a, w_up_b, w_out, g_post.reshape(1, D))


ROW_TILE = 512
ATTN_TILE = 256
GLA_CHUNK = 64
GLA_STEP_ROWS = 2048


def _layer(x, g_pre, w_in, lam_q1, lam_k1, lam_q2, lam_k2, g_sub_a, w_alpha, b_alpha, g_sub_b,
           w_up_a, w_up_b, w_out, g_post):
    n_mix = 2 * (DA_HEADS * 2 * DA_HEAD_DIM) + 2 * (DA_HEADS * DA_V_DIM) + 2 * (GLA_HEADS * GLA_DK) \
        + 2 * (GLA_HEADS * GLA_DV)
    gate0 = n_mix + GLA_GATE_RANK
    qa_t, ka, va_t, za, qb, kb, vb, zb, ga, gb, glog = _input_projection(
        x, g_pre, w_in.astype(BF16), w_in[:, gate0:].astype(BF16), w_alpha.astype(BF16), b_alpha,
        tm=ROW_TILE)
    ua = _diff_attention(qa_t, ka, va_t, za, lam_q1, lam_k1, lam_q2, lam_k2, g_sub_a, tq=ATTN_TILE)
    ub = _gated_linear_attention(qb, kb, glog, vb, zb, g_sub_b, chunk=GLA_CHUNK, rows_per_step=GLA_STEP_ROWS)
    return _merge(x, ua, ub, ga, gb, w_up_a.astype(BF16), w_up_b.astype(BF16), w_out.astype(BF16),
                  g_post, tm=2 * ROW_TILE, sub_rows=ROW_TILE)


def kernel(x, g_pre, w_in, lam_q1, lam_k1, lam_q2, lam_k2, g_sub_a, w_alpha, b_alpha, g_sub_b, w_up_a, w_up_b, w_out, g_post):
    depth = w_in.shape[0]
    assert depth == 1, "LAM_INIT is specialised to a single layer"
    first = lambda p: p.reshape(p.shape[1:])
    return _layer(x, *(first(p) for p in (g_pre, w_in, lam_q1, lam_k1, lam_q2, lam_k2, g_sub_a, w_alpha,
                                          b_alpha, g_sub_b, w_up_a, w_up_b, w_out, g_post)))
```

```python
import functools
import math

import numpy as np
import jax
import jax.numpy as jnp
from jax import lax
from jax.experimental import pallas as pl
from jax.experimental.pallas import tpu as pltpu

F32 = jnp.float32
BF16 = jnp.bfloat16

DA_HEADS = 4
DA_HEAD_DIM = 64
DA_V_DIM = 128
GLA_HEADS = 4
GLA_DK = 64
GLA_DV = 128
GLA_GATE_RANK = 16
GLA_TAU = 16.0
RMS_EPS = 1e-6
LAYER_IDX = 0
LAM_INIT = 0.8 - 0.6 * math.exp(-0.3 * LAYER_IDX)

LOG2E = math.log2(math.e)
LANE = 128
ONES_ROWS = 16
SLOPE_PARTS = 3
VMEM_LIMIT = 56 * 1024 * 1024

NT_DIMS = (((1,), (1,)), ((), ()))


def _rms(x, g):
    return x * lax.rsqrt(jnp.mean(x * x, axis=-1, keepdims=True) + RMS_EPS) * g


def _silu(z):
    return z * jax.nn.sigmoid(z)


def _inproj_kernel(x_ref, gpre_ref, wmix_ref, wgate_ref, walpha_ref, balpha_ref,
                   qa_ref, ka_ref, va_ref, za_ref, qb_ref, kb_ref, vb_ref, zb_ref,
                   ga_ref, gb_ref, glog_ref, h_scr, t_scr, *, segments, lr_cols, feature_major,
                   q_scale_a, q_scale_b):
    outs = dict(qa=qa_ref, ka=ka_ref, va=va_ref, za=za_ref, qb=qb_ref, kb=kb_ref, vb=vb_ref,
                zb=zb_ref, ga=ga_ref, gb=gb_ref)
    weights = dict(mix=wmix_ref, gate=wgate_ref)
    scales = dict(qa=q_scale_a, qb=q_scale_b)
    sub = t_scr.shape[0]
    for r0 in range(0, h_scr.shape[0], sub):
        rows = slice(r0, r0 + sub)
        h_scr[rows, :] = _rms(x_ref[0, rows, :], gpre_ref[...]).astype(BF16)
        for name, wname, c0, width, o0 in segments:
            acc = jnp.dot(h_scr[rows, :], weights[wname][:, c0:c0 + width], preferred_element_type=F32)
            if name in scales:
                acc = acc * scales[name]
            if name in ("za", "zb"):
                acc = _silu(acc)
            if name in feature_major:
                t_scr[...] = acc
                outs[name][0, o0:o0 + width, rows] = t_scr[...].T.astype(BF16)
            else:
                outs[name][0, rows, o0:o0 + width] = acc.astype(BF16)
        lr = jnp.dot(h_scr[rows, :], wmix_ref[:, lr_cols[0]:lr_cols[1]], preferred_element_type=F32)
        z = jnp.dot(lr.astype(BF16), walpha_ref[...], preferred_element_type=F32) + balpha_ref[...]
        log_sig = jnp.minimum(z, 0.0) - jnp.log1p(jnp.exp(-jnp.abs(z)))
        glog_ref[0, rows, :] = log_sig / GLA_TAU


def _input_projection(x, g_pre, w_mix_lr, w_gate, w_alpha, b_alpha, *, tm, sub_rows):
    B, S, D = x.shape
    a_qk, a_v = DA_HEADS * 2 * DA_HEAD_DIM, DA_HEADS * DA_V_DIM
    b_qk, b_v = GLA_HEADS * GLA_DK, GLA_HEADS * GLA_DV
    widths = [("qa", "mix", a_qk), ("ka", "mix", a_qk), ("va", "mix", a_v), ("za", "mix", a_v),
              ("qb", "mix", b_qk), ("kb", "mix", b_qk), ("vb", "mix", b_v), ("zb", "mix", b_v),
              ("ga", "gate", D), ("gb", "gate", D)]
    max_chunk = 512
    segments = []
    col = dict(mix=0, gate=0)
    for name, wname, width in widths:
        for o0 in range(0, width, max_chunk):
            segments.append((name, wname, col[wname] + o0, min(max_chunk, width - o0), o0))
        col[wname] += width
    lr_cols = (col["mix"], col["mix"] + GLA_GATE_RANK)
    assert lr_cols[1] <= w_mix_lr.shape[1] and col["gate"] == w_gate.shape[1]

    def row_spec(width):
        return pl.BlockSpec((1, tm, width), lambda b, i: (b, i, 0))

    def const_spec(shape):
        return pl.BlockSpec(shape, lambda b, i: (0,) * len(shape), pipeline_mode=pl.Buffered(1))

    feature_major = ("qa", "va")
    out_shape, out_specs = [], []
    for name, _, w in widths:
        if name in feature_major:
            out_shape.append(jax.ShapeDtypeStruct((B, w, S), BF16))
            out_specs.append(pl.BlockSpec((1, w, tm), lambda b, i: (b, 0, i)))
        else:
            out_shape.append(jax.ShapeDtypeStruct((B, S, w), BF16))
            out_specs.append(row_spec(w))
    out_shape.append(jax.ShapeDtypeStruct((B, S, b_qk), F32))
    out_specs.append(row_spec(b_qk))
    kern = functools.partial(_inproj_kernel, segments=tuple(segments), lr_cols=lr_cols,
                             feature_major=feature_major,
                             q_scale_a=DA_HEAD_DIM ** -0.5 * LOG2E,
                             q_scale_b=GLA_DK ** -0.5)
    return pl.pallas_call(
        kern,
        grid=(B, S // tm),
        in_specs=[row_spec(D), const_spec((1, D)), const_spec(w_mix_lr.shape),
                  const_spec(w_gate.shape), const_spec(w_alpha.shape), const_spec((1, b_qk))],
        out_specs=out_specs,
        out_shape=out_shape,
        scratch_shapes=[pltpu.VMEM((tm, D), BF16), pltpu.VMEM((sub_rows, max_chunk), F32)],
        compiler_params=pltpu.CompilerParams(
            dimension_semantics=("parallel", "parallel"), vmem_limit_bytes=VMEM_LIMIT),
        name="inproj",
    )(x, g_pre.reshape(1, D), w_mix_lr, w_gate, w_alpha, b_alpha.reshape(1, b_qk))


POSITIONS_PER_TRIP = 8


def _attn_kernel(lq1_ref, lk1_ref, lq2_ref, lk2_ref, q_ref, qn_ref, k_ref, vt_ref, za_ref, gsub_ref,
                 o_ref, qs_scr, ks_scr, vt_scr, causal_scr, st_scr, mx_scr, p_scr, alpha_scr, m_scr, acc_scr,
                 *, tq, tk):
    i = pl.program_id(1)
    H, DV = DA_HEADS, DA_V_DIM
    n_kv = vt_scr.shape[1]
    slope_parts, slopes = [], []
    for h in range(H):
        rest = np.float32(2.0 ** (-8.0 * (h + 1) / H) * LOG2E)
        slopes.append(float(rest))
        parts = []
        for _ in range(SLOPE_PARTS):
            part = np.asarray(rest, dtype=BF16)
            parts.append(float(part))
            rest = np.float32(rest - np.float32(part))
        assert rest == 0.0
        slope_parts.append(parts)

    def prepare_keys_values():
        lane = lax.broadcasted_iota(jnp.int32, (tk, LANE), 1)
        key = lax.broadcasted_iota(jnp.int32, (tk, LANE), 0).astype(F32)
        key_lanes = jnp.where(lane < SLOPE_PARTS, key, 0.0).astype(BF16)
        ones_rows = jnp.ones((ONES_ROWS, tk), BF16)
        def prep(j, carry):
            k0 = pl.multiple_of(j * tk, tk)
            for h in range(H):
                ks_scr[h, pl.ds(k0, tk), 0:LANE] = k_ref[0, pl.ds(k0, tk), h * LANE:(h + 1) * LANE]
                ks_scr[h, pl.ds(k0, tk), LANE:2 * LANE] = key_lanes
            return carry
        lax.fori_loop(0, n_kv, prep, 0)
        for j in range(n_kv):
            for h in range(H):
                vt_scr[h, j, 0:DV, :] = vt_ref[0, h * DV:(h + 1) * DV, j * tk:(j + 1) * tk]
                vt_scr[h, j, DV:DV + ONES_ROWS, :] = ones_rows
        key_idx = lax.broadcasted_iota(jnp.int32, (tk, 2 * tq), 0)
        qcol = lax.broadcasted_iota(jnp.int32, (tk, 2 * tq), 1)
        visible = key_idx <= jnp.where(qcol >= tq, qcol - tq, qcol)
        causal_scr[...] = jnp.where(visible, 0.0, -jnp.inf)

    def load_queries(ref):
        row = lax.broadcasted_iota(jnp.int32, (LANE, tq), 0)
        zero = jnp.zeros((LANE, tq), BF16)
        for h in range(H):
            qt = ref[0, h * LANE:(h + 1) * LANE, :]
            qs_scr[h, 0:LANE, 0:tq] = jnp.where(row < DA_HEAD_DIM, qt, zero)
            qs_scr[h, 0:LANE, tq:2 * tq] = jnp.where(row >= DA_HEAD_DIM, qt, zero)

    def load_slopes():
        row = lax.broadcasted_iota(jnp.int32, (LANE, 2 * tq), 0)
        for h in range(H):
            rows = jnp.zeros((LANE, 2 * tq), F32)
            for r, part in enumerate(slope_parts[h]):
                rows = jnp.where(row == r, part, rows)
            qs_scr[h, LANE:2 * LANE, :] = rows.astype(BF16)

    def produce(j, buf, masked=False):
        k0 = pl.multiple_of(j * tk, tk)
        for h in range(H):
            st = jnp.dot(ks_scr[h, pl.ds(k0, tk), :], qs_scr[h], preferred_element_type=F32)
            if masked:
                st = st + causal_scr[...]
            st_scr[buf, h] = st
            mx_scr[buf, h] = jnp.max(st, axis=0, keepdims=True)

    def softmax(j, buf):
        tile_dist = (j * tk - i * tq).astype(F32)
        for h in range(H):
            off = slopes[h] * tile_dist
            m_prev = m_scr[h]
            m_next = jnp.maximum(m_prev, mx_scr[buf, h] + off)
            alpha_scr[buf, h] = jnp.exp2(m_prev - m_next)
            p_scr[buf, h] = jnp.exp2(st_scr[buf, h] - (m_next - off)).astype(BF16)
            m_scr[h] = m_next

    def weighted_values(j, buf):
        for h in range(H):
            acc_scr[h] = alpha_scr[buf, h] * acc_scr[h] + jnp.dot(vt_scr[h, j], p_scr[buf, h],
                                                                  preferred_element_type=F32)

    def run(stage, base, offset):
        pos, buf = base + offset, offset % 2
        if offset > 0:
            tile = pos - 1
        else:
            tile = jnp.where(pos == 0, i, pos - 1)
        if stage == "produce":
            produce(tile, buf)
        elif stage == "softmax":
            softmax(tile, buf)
        else:
            weighted_values(tile, buf)

    def steady(base):
        for t in range(1, POSITIONS_PER_TRIP + 1):
            run("produce", base, t + 1)
            run("weighted_values", base, t - 1)
            run("softmax", base, t)

    def drain(base, r):
        for t in range(1, r + 1):
            if t + 1 <= r:
                run("produce", base, t + 1)
            run("weighted_values", base, t - 1)
            run("softmax", base, t)
        run("weighted_values", base, r)

    def reset():
        m_scr[...] = jnp.full(m_scr.shape, -jnp.inf, F32)
        acc_scr[...] = jnp.zeros(acc_scr.shape, F32)

    @pl.when(i == 0)
    def _():
        prepare_keys_values()
        load_slopes()
        load_queries(q_ref)
        produce(0, 0, masked=True)
        reset()
        softmax(0, 0)
        weighted_values(0, 0)

    @pl.when(i > 0)
    def _():
        reset()
        run("produce", 0, 1)
        run("softmax", 0, 0)

        def trip(t, carry):
            steady(POSITIONS_PER_TRIP * t)
            return carry
        n_trips = (i - 1) // POSITIONS_PER_TRIP
        lax.fori_loop(0, n_trips, trip, 0)

        done = POSITIONS_PER_TRIP * n_trips
        for r in range(1, POSITIONS_PER_TRIP + 1):
            pl.when(i - done == r)(functools.partial(drain, done, r))

    n_q = pl.num_programs(1)
    load_queries(qn_ref)
    produce(jnp.minimum(i + 1, n_q - 1), 0, masked=True)

    f = lambda r: r[...].astype(F32)
    lam = (jnp.exp(jnp.sum(f(lq1_ref) * f(lk1_ref), axis=1, keepdims=True))
           - jnp.exp(jnp.sum(f(lq2_ref) * f(lk2_ref), axis=1, keepdims=True)) + LAM_INIT)
    gain = gsub_ref[...] * (1.0 - LAM_INIT)
    for h in range(H):
        on = acc_scr[h, 0:DV, :] / acc_scr[h, DV:DV + 1, :]
        ot = on[:, 0:tq] - lam * on[:, tq:2 * tq]
        ot = ot * lax.rsqrt(jnp.mean(ot * ot, axis=0, keepdims=True) + RMS_EPS)
        o = ot.T * gain
        gate = za_ref[0, :, h * DV:(h + 1) * DV].astype(F32)
        o_ref[0, :, h * DV:(h + 1) * DV] = (o * gate).astype(o_ref.dtype)


def _diff_attention(qa_t, ka, va_t, za, lam_q1, lam_k1, lam_q2, lam_k2, g_sub_a, *, tq):
    B, S, W = ka.shape
    tk = tq
    assert W == DA_HEADS * LANE and S % tq == 0 and qa_t.shape == (B, W, S) and va_t.shape == (B, W, S)
    vec = lambda v: v.reshape(1, DA_HEAD_DIM)
    vec_spec = pl.BlockSpec((1, DA_HEAD_DIM), lambda b, i: (0, 0))
    tile_spec = pl.BlockSpec((1, tq, W), lambda b, i: (b, i, 0))
    seq_spec = pl.BlockSpec((1, S, W), lambda b, i: (b, 0, 0))
    n_q = S // tq
    q_spec = pl.BlockSpec((1, W, tq), lambda b, i: (b, 0, i))
    next_q_spec = pl.BlockSpec((1, W, tq), lambda b, i: (b, 0, jnp.minimum(i + 1, n_q - 1)))
    seq_t_spec = pl.BlockSpec((1, W, S), lambda b, i: (b, 0, 0))
    kern = functools.partial(_attn_kernel, tq=tq, tk=tk)
    return pl.pallas_call(
        kern,
        grid=(B, n_q),
        in_specs=[vec_spec, vec_spec, vec_spec, vec_spec, q_spec, next_q_spec, seq_spec, seq_t_spec,
                  tile_spec,
                  pl.BlockSpec((1, DA_V_DIM), lambda b, i: (0, 0))],
        out_specs=tile_spec,
        out_shape=jax.ShapeDtypeStruct((B, S, W), BF16),
        scratch_shapes=[pltpu.VMEM((DA_HEADS, 2 * LANE, 2 * tq), BF16),
                        pltpu.VMEM((DA_HEADS, S, 2 * LANE), BF16),
                        pltpu.VMEM((DA_HEADS, S // tk, DA_V_DIM + ONES_ROWS, tk), BF16),
                        pltpu.VMEM((tk, 2 * tq), F32),
                        pltpu.VMEM((2, DA_HEADS, tk, 2 * tq), F32),
                        pltpu.VMEM((2, DA_HEADS, 1, 2 * tq), F32),
                        pltpu.VMEM((2, DA_HEADS, tk, 2 * tq), BF16),
                        pltpu.VMEM((2, DA_HEADS, 1, 2 * tq), F32),
                        pltpu.VMEM((DA_HEADS, 1, 2 * tq), F32),
                        pltpu.VMEM((DA_HEADS, DA_V_DIM + ONES_ROWS, 2 * tq), F32)],
        compiler_params=pltpu.CompilerParams(
            dimension_semantics=("parallel", "arbitrary"), vmem_limit_bytes=VMEM_LIMIT),
        name="diffattn",
    )(vec(lam_q1), vec(lam_k1), vec(lam_q2), vec(lam_k2), qa_t, qa_t, ka, va_t, za,
      g_sub_a.reshape(1, DA_V_DIM))


def _gla_tables(C, n_heads, dk):
    t = np.arange(C)[:, None]
    u = np.arange(C)[None, :]
    sums = [(u <= t), (u > t)]
    halves = []
    m = C // 2
    while m >= 1:
        halves.append(m)
        m //= 2
    upper_rows, pair_masks = [], []
    for m in halves:
        blk = 2 * m
        mid = (t // blk) * blk + m
        upper = (t % blk) >= m
        sums.append(np.where(upper, (u >= mid) & (u <= t), (u > t) & (u <= mid - 1)))
        upper_rows.append(np.broadcast_to(upper, (C, n_heads * dk)))
        tt, ss = np.arange(C)[:, None], np.arange(C)[None, :]
        pair_masks.append(((tt // blk) == (ss // blk)) & ((tt % blk) >= m) & ((ss % blk) < m))
    pair_masks.append(np.eye(C, dtype=bool))
    sum_sel = np.concatenate(sums, axis=0).astype(np.float32)
    upper_rows = np.stack(upper_rows).astype(np.float32)
    eye_h = np.eye(n_heads, dtype=bool)
    pair_bd = np.stack([np.kron(eye_h, pm) for pm in pair_masks]).astype(np.float32)
    head_of_lane = np.arange(n_heads * dk)[None, :] // dk
    head_of_row = np.repeat(np.arange(n_heads), C)[:, None]
    head_mask = (head_of_lane == head_of_row).astype(np.float32)
    return sum_sel, upper_rows, pair_bd, head_mask


CHUNKS_PER_TRIP = 32


def _gla_kernel(q_ref, k_ref, g_ref, v_ref, z_ref, gsub_ref, sumsel_ref, upper_ref, pair_ref, hmask_ref,
                o_ref, state_scr, decay_scr, mixed_scr, *, chunk, n_chunks, n_levels):
    C = chunk
    H, DV = GLA_HEADS, GLA_DV

    @pl.when(pl.program_id(1) == 0)
    def _():
        state_scr[...] = jnp.zeros(state_scr.shape, F32)

    hmask = hmask_ref[...]

    def stack_heads(a):
        return jnp.concatenate([a.astype(BF16)] * H, axis=0) * hmask

    def decays(c, buf):
        r0 = pl.multiple_of(c * C, C)
        g = g_ref[0, pl.ds(r0, C), :]
        g_hi = g.astype(BF16)
        g_lo = (g - g_hi.astype(F32)).astype(BF16)
        g2 = jnp.concatenate([g_hi, g_lo], axis=0)
        d_all = jnp.dot(sumsel_ref[...], g2, preferred_element_type=F32)
        decay_scr[buf] = jnp.exp(d_all)

    def mix(c, buf, state_t):
        r0 = pl.multiple_of(c * C, C)
        q = q_ref[0, pl.ds(r0, C), :].astype(F32)
        k = k_ref[0, pl.ds(r0, C), :].astype(F32)
        v = v_ref[0, pl.ds(r0, C), :]
        v_st = jnp.concatenate([v[:, hh * DV:(hh + 1) * DV] for hh in range(H)], axis=0)
        v_st_t = v_st.astype(F32).T.astype(BF16)
        e_all = decay_scr.at[buf]

        k_out = stack_heads(k * e_all[C:2 * C])
        new_state_t = (e_all[C - 1:C] * state_t
                       + jnp.dot(v_st_t, k_out, preferred_element_type=F32))

        a_bd = None
        for lvl in range(n_levels + 1):
            if lvl < n_levels:
                e = e_all[(2 + lvl) * C:(3 + lvl) * C]
                xs = stack_heads(jnp.where(upper_ref[lvl] > 0.5, q, k) * e)
                prod = lax.dot_general(xs, xs, NT_DIMS, preferred_element_type=F32)
            else:
                prod = lax.dot_general(stack_heads(q), stack_heads(k), NT_DIMS, preferred_element_type=F32)
            term = prod * pair_ref[lvl]
            a_bd = term if a_bd is None else a_bd + term

        q_in = stack_heads(q * e_all[0:C])
        mixed_scr[buf] = (lax.dot_general(q_in, state_t.astype(BF16), NT_DIMS, preferred_element_type=F32)
                          + jnp.dot(a_bd.astype(BF16), v_st, preferred_element_type=F32))
        return new_state_t

    def finish(c, buf):
        r0 = pl.multiple_of(c * C, C)
        for hh in range(H):
            o_h = _rms(mixed_scr[buf, hh * C:(hh + 1) * C, :], gsub_ref[...])
            gate = z_ref[0, pl.ds(r0, C), hh * DV:(hh + 1) * DV].astype(F32)
            o_ref[0, pl.ds(r0, C), hh * DV:(hh + 1) * DV] = (o_h * gate).astype(o_ref.dtype)

    decays(0, 0)
    mixed_scr[1] = jnp.zeros(mixed_scr.shape[1:], F32)

    def trip(cc, carry):
        c0 = CHUNKS_PER_TRIP * cc
        state = state_scr[...]
        for u in range(CHUNKS_PER_TRIP):
            c = c0 + u
            decays(jnp.minimum(c + 1, n_chunks - 1), (u + 1) % 2)
            state = mix(c, u % 2, state)
            finish(jnp.maximum(c - 1, 0), (u + 1) % 2)
        state_scr[...] = state
        return carry

    lax.fori_loop(0, n_chunks // CHUNKS_PER_TRIP, trip, 0)
    finish(n_chunks - 1, 1)


def _gated_linear_attention(qb, kb, glog, vb, zb, g_sub_b, *, chunk, rows_per_step):
    B, S, b_qk = qb.shape
    b_v = vb.shape[-1]
    sum_sel, upper_rows, pair_bd, head_mask = _gla_tables(chunk, GLA_HEADS, GLA_DK)
    n_levels = upper_rows.shape[0]
    T = rows_per_step

    def row_spec(width):
        return pl.BlockSpec((1, T, width), lambda b, i: (b, i, 0))

    def const_spec(shape):
        return pl.BlockSpec(shape, lambda b, i: (0,) * len(shape))

    sum_sel2 = np.concatenate([sum_sel, sum_sel], axis=1)
    assert CHUNKS_PER_TRIP % 2 == 0 and (T // chunk) % CHUNKS_PER_TRIP == 0
    kern = functools.partial(_gla_kernel, chunk=chunk, n_chunks=T // chunk, n_levels=n_levels)
    return pl.pallas_call(
        kern,
        grid=(B, S // T),
        in_specs=[row_spec(b_qk), row_spec(b_qk), row_spec(b_qk), row_spec(b_v), row_spec(b_v),
                  const_spec((1, GLA_DV)), const_spec(sum_sel2.shape), const_spec(upper_rows.shape),
                  const_spec(pair_bd.shape), const_spec(head_mask.shape)],
        out_specs=row_spec(b_v),
        out_shape=jax.ShapeDtypeStruct((B, S, b_v), BF16),
        scratch_shapes=[pltpu.VMEM((GLA_DV, b_qk), F32),
                        pltpu.VMEM((2,) + (sum_sel.shape[0], b_qk), F32),
                        pltpu.VMEM((2, GLA_HEADS * chunk, GLA_DV), F32)],
        compiler_params=pltpu.CompilerParams(
            dimension_semantics=("parallel", "arbitrary"), vmem_limit_bytes=VMEM_LIMIT),
        name="gla",
    )(qb, kb, glog, vb, zb, g_sub_b.reshape(1, GLA_DV), jnp.asarray(sum_sel2, BF16),
      jnp.asarray(upper_rows), jnp.asarray(pair_bd), jnp.asarray(head_mask, BF16))


def _merge_kernel(x_ref, ua_ref, ub_ref, ga_ref, gb_ref, wua_ref, wub_ref, wout_ref, gpost_ref, o_ref, *,
                  sub_rows):
    for r0 in range(0, x_ref.shape[1], sub_rows):
        rows = slice(r0, r0 + sub_rows)
        ya = jnp.dot(ua_ref[0, rows, :], wua_ref[...], preferred_element_type=F32)
        yb = jnp.dot(ub_ref[0, rows, :], wub_ref[...], preferred_element_type=F32)
        y = (jax.nn.sigmoid(ga_ref[0, rows, :].astype(F32)) * ya
             + jax.nn.sigmoid(gb_ref[0, rows, :].astype(F32)) * yb)
        out = jnp.dot(y.astype(BF16), wout_ref[...], preferred_element_type=F32)
        o_ref[0, rows, :] = x_ref[0, rows, :] + _rms(out, gpost_ref[...])


def _merge(x, ua, ub, ga, gb, w_up_a, w_up_b, w_out, g_post, *, tm, sub_rows):
    B, S, D = x.shape
    assert tm % sub_rows == 0

    def row_spec(width):
        return pl.BlockSpec((1, tm, width), lambda b, i: (b, i, 0))

    def const_spec(shape):
        return pl.BlockSpec(shape, lambda b, i: (0,) * len(shape), pipeline_mode=pl.Buffered(1))

    return pl.pallas_call(
        functools.partial(_merge_kernel, sub_rows=sub_rows),
        grid=(B, S // tm),
        in_specs=[row_spec(D), row_spec(ua.shape[-1]), row_spec(ub.shape[-1]), row_spec(D), row_spec(D),
                  const_spec(w_up_a.shape), const_spec(w_up_b.shape), const_spec(w_out.shape),
                  const_spec((1, D))],
        out_specs=row_spec(D),
        out_shape=jax.ShapeDtypeStruct((B, S, D), x.dtype),
        compiler_params=pltpu.CompilerParams(
            dimension_semantics=("parallel", "parallel"), vmem_limit_bytes=VMEM_LIMIT),
        name="merge",
    )(x, ua, ub, ga, gb, w_up_a, w_up_b, w_out, g_post.reshape(1, D))


ROW_TILE = 512
ATTN_TILE = 256
GLA_CHUNK = 64
GLA_STEP_ROWS = 2048


def _layer(x, g_pre, w_in, lam_q1, lam_k1, lam_q2, lam_k2, g_sub_a, w_alpha, b_alpha, g_sub_b,
           w_up_a, w_up_b, w_out, g_post):
    n_mix = 2 * (DA_HEADS * 2 * DA_HEAD_DIM) + 2 * (DA_HEADS * DA_V_DIM) + 2 * (GLA_HEADS * GLA_DK) \
        + 2 * (GLA_HEADS * GLA_DV)
    gate0 = n_mix + GLA_GATE_RANK
    qa_t, ka, va_t, za, qb, kb, vb, zb, ga, gb, glog = _input_projection(
        x, g_pre, w_in.astype(BF16), w_in[:, gate0:].astype(BF16), w_alpha.astype(BF16), b_alpha,
        tm=2 * ROW_TILE, sub_rows=ROW_TILE)
    ua = _diff_attention(qa_t, ka, va_t, za, lam_q1, lam_k1, lam_q2, lam_k2, g_sub_a, tq=ATTN_TILE)
    ub = _gated_linear_attention(qb, kb, glog, vb, zb, g_sub_b, chunk=GLA_CHUNK, rows_per_step=GLA_STEP_ROWS)
    return _merge(x, ua, ub, ga, gb, w_up_a.astype(BF16), w_up_b.astype(BF16), w_out.astype(BF16),
                  g_post, tm=2 * ROW_TILE, sub_rows=ROW_TILE)


def kernel(x, g_pre, w_in, lam_q1, lam_k1, lam_q2, lam_k2, g_sub_a, w_alpha, b_alpha, g_sub_b, w_up_a, w_up_b, w_out, g_post):
    depth = w_in.shape[0]
    assert depth == 1, "LAM_INIT is specialised to a single layer"
    first = lambda p: p.reshape(p.shape[1:])
    return _layer(x, *(first(p) for p in (g_pre, w_in, lam_q1, lam_k1, lam_q2, lam_k2, g_sub_a, w_alpha,
                                          b_alpha, g_sub_b, w_up_a, w_up_b, w_out, g_post)))
```

```python
import functools
import math

import numpy as np
import jax
import jax.numpy as jnp
from jax import lax
from jax.experimental import pallas as pl
from jax.experimental.pallas import tpu as pltpu

F32 = jnp.float32
BF16 = jnp.bfloat16

DA_HEADS = 4
DA_HEAD_DIM = 64
DA_V_DIM = 128
GLA_HEADS = 4
GLA_DK = 64
GLA_DV = 128
GLA_GATE_RANK = 16
GLA_TAU = 16.0
RMS_EPS = 1e-6
LAYER_IDX = 0
LAM_INIT = 0.8 - 0.6 * math.exp(-0.3 * LAYER_IDX)

LOG2E = math.log2(math.e)
LANE = 128
ONES_ROWS = 16
SLOPE_PARTS = 3
VMEM_LIMIT = 56 * 1024 * 1024

NT_DIMS = (((1,), (1,)), ((), ()))


def _rms(x, g):
    return x * lax.rsqrt(jnp.mean(x * x, axis=-1, keepdims=True) + RMS_EPS) * g


def _silu(z):
    return z * jax.nn.sigmoid(z)


def _inproj_kernel(x_ref, gpre_ref, wmix_ref, wgate_ref, walpha_ref, balpha_ref,
                   qa_ref, ka_ref, va_ref, za_ref, qb_ref, kb_ref, vb_ref, zb_ref,
                   ga_ref, gb_ref, glog_ref, h_scr, t_scr, *, segments, lr_cols, feature_major,
                   q_scale_a, q_scale_b):
    outs = dict(qa=qa_ref, ka=ka_ref, va=va_ref, za=za_ref, qb=qb_ref, kb=kb_ref, vb=vb_ref,
                zb=zb_ref, ga=ga_ref, gb=gb_ref)
    weights = dict(mix=wmix_ref, gate=wgate_ref)
    scales = dict(qa=q_scale_a, qb=q_scale_b)
    h_scr[...] = _rms(x_ref[0], gpre_ref[...]).astype(BF16)
    for name, wname, c0, width, o0 in segments:
        acc = jnp.dot(h_scr[...], weights[wname][:, c0:c0 + width], preferred_element_type=F32)
        if name in scales:
            acc = acc * scales[name]
        if name in ("za", "zb"):
            acc = _silu(acc)
        if name in feature_major:
            t_scr[...] = acc
            outs[name][0, o0:o0 + width, :] = t_scr[...].T.astype(BF16)
        else:
            outs[name][0, :, o0:o0 + width] = acc.astype(BF16)
    lr = jnp.dot(h_scr[...], wmix_ref[:, lr_cols[0]:lr_cols[1]], preferred_element_type=F32)
    z = jnp.dot(lr.astype(BF16), walpha_ref[...], preferred_element_type=F32) + balpha_ref[...]
    log_sig = jnp.minimum(z, 0.0) - jnp.log1p(jnp.exp(-jnp.abs(z)))
    glog_ref[0] = log_sig / GLA_TAU


def _input_projection(x, g_pre, w_mix_lr, w_gate, w_alpha, b_alpha, *, tm):
    B, S, D = x.shape
    a_qk, a_v = DA_HEADS * 2 * DA_HEAD_DIM, DA_HEADS * DA_V_DIM
    b_qk, b_v = GLA_HEADS * GLA_DK, GLA_HEADS * GLA_DV
    widths = [("qa", "mix", a_qk), ("ka", "mix", a_qk), ("va", "mix", a_v), ("za", "mix", a_v),
              ("qb", "mix", b_qk), ("kb", "mix", b_qk), ("vb", "mix", b_v), ("zb", "mix", b_v),
              ("ga", "gate", D), ("gb", "gate", D)]
    max_chunk = 512
    segments = []
    col = dict(mix=0, gate=0)
    for name, wname, width in widths:
        for o0 in range(0, width, max_chunk):
            segments.append((name, wname, col[wname] + o0, min(max_chunk, width - o0), o0))
        col[wname] += width
    lr_cols = (col["mix"], col["mix"] + GLA_GATE_RANK)
    assert lr_cols[1] <= w_mix_lr.shape[1] and col["gate"] == w_gate.shape[1]

    def row_spec(width):
        return pl.BlockSpec((1, tm, width), lambda b, i: (b, i, 0))

    def const_spec(shape):
        return pl.BlockSpec(shape, lambda b, i: (0,) * len(shape), pipeline_mode=pl.Buffered(1))

    feature_major = ("qa", "va")
    out_shape, out_specs = [], []
    for name, _, w in widths:
        if name in feature_major:
            out_shape.append(jax.ShapeDtypeStruct((B, w, S), BF16))
            out_specs.append(pl.BlockSpec((1, w, tm), lambda b, i: (b, 0, i)))
        else:
            out_shape.append(jax.ShapeDtypeStruct((B, S, w), BF16))
            out_specs.append(row_spec(w))
    out_shape.append(jax.ShapeDtypeStruct((B, S, b_qk), F32))
    out_specs.append(row_spec(b_qk))
    kern = functools.partial(_inproj_kernel, segments=tuple(segments), lr_cols=lr_cols,
                             feature_major=feature_major,
                             q_scale_a=DA_HEAD_DIM ** -0.5 * LOG2E,
                             q_scale_b=GLA_DK ** -0.5)
    return pl.pallas_call(
        kern,
        grid=(B, S // tm),
        in_specs=[row_spec(D), const_spec((1, D)), const_spec(w_mix_lr.shape),
                  const_spec(w_gate.shape), const_spec(w_alpha.shape), const_spec((1, b_qk))],
        out_specs=out_specs,
        out_shape=out_shape,
        scratch_shapes=[pltpu.VMEM((tm, D), BF16), pltpu.VMEM((tm, max_chunk), F32)],
        compiler_params=pltpu.CompilerParams(
            dimension_semantics=("parallel", "parallel"), vmem_limit_bytes=VMEM_LIMIT),
        name="inproj",
    )(x, g_pre.reshape(1, D), w_mix_lr, w_gate, w_alpha, b_alpha.reshape(1, b_qk))


POSITIONS_PER_TRIP = 8


def _attn_kernel(lq1_ref, lk1_ref, lq2_ref, lk2_ref, q_ref, qn_ref, k_ref, vt_ref, za_ref, gsub_ref,
                 o_ref, qs_scr, ks_scr, vt_scr, causal_scr, st_scr, mx_scr, p_scr, alpha_scr, m_scr, acc_scr,
                 *, tq, tk):
    i = pl.program_id(1)
    H, DV = DA_HEADS, DA_V_DIM
    n_kv = vt_scr.shape[1]
    slope_parts, slopes = [], []
    for h in range(H):
        rest = np.float32(2.0 ** (-8.0 * (h + 1) / H) * LOG2E)
        slopes.append(float(rest))
        parts = []
        for _ in range(SLOPE_PARTS):
            part = np.asarray(rest, dtype=BF16)
            parts.append(float(part))
            rest = np.float32(rest - np.float32(part))
        assert rest == 0.0
        slope_parts.append(parts)

    def prepare_keys_values():
        lane = lax.broadcasted_iota(jnp.int32, (tk, LANE), 1)
        key = lax.broadcasted_iota(jnp.int32, (tk, LANE), 0).astype(F32)
        key_lanes = jnp.where(lane < SLOPE_PARTS, key, 0.0).astype(BF16)
        ones_rows = jnp.ones((ONES_ROWS, tk), BF16)
        def prep(j, carry):
            k0 = pl.multiple_of(j * tk, tk)
            for h in range(H):
                ks_scr[h, pl.ds(k0, tk), 0:LANE] = k_ref[0, pl.ds(k0, tk), h * LANE:(h + 1) * LANE]
                ks_scr[h, pl.ds(k0, tk), LANE:2 * LANE] = key_lanes
            return carry
        lax.fori_loop(0, n_kv, prep, 0)
        for j in range(n_kv):
            for h in range(H):
                vt_scr[h, j, 0:DV, :] = vt_ref[0, h * DV:(h + 1) * DV, j * tk:(j + 1) * tk]
                vt_scr[h, j, DV:DV + ONES_ROWS, :] = ones_rows
        key_idx = lax.broadcasted_iota(jnp.int32, (tk, 2 * tq), 0)
        qcol = lax.broadcasted_iota(jnp.int32, (tk, 2 * tq), 1)
        visible = key_idx <= jnp.where(qcol >= tq, qcol - tq, qcol)
        causal_scr[...] = jnp.where(visible, 0.0, -jnp.inf)

    def load_queries(ref):
        row = lax.broadcasted_iota(jnp.int32, (LANE, tq), 0)
        zero = jnp.zeros((LANE, tq), BF16)
        for h in range(H):
            qt = ref[0, h * LANE:(h + 1) * LANE, :]
            qs_scr[h, 0:LANE, 0:tq] = jnp.where(row < DA_HEAD_DIM, qt, zero)
            qs_scr[h, 0:LANE, tq:2 * tq] = jnp.where(row >= DA_HEAD_DIM, qt, zero)

    def load_slopes():
        row = lax.broadcasted_iota(jnp.int32, (LANE, 2 * tq), 0)
        for h in range(H):
            rows = jnp.zeros((LANE, 2 * tq), F32)
            for r, part in enumerate(slope_parts[h]):
                rows = jnp.where(row == r, part, rows)
            qs_scr[h, LANE:2 * LANE, :] = rows.astype(BF16)

    def produce(j, buf, masked=False):
        k0 = pl.multiple_of(j * tk, tk)
        for h in range(H):
            st = jnp.dot(ks_scr[h, pl.ds(k0, tk), :], qs_scr[h], preferred_element_type=F32)
            if masked:
                st = st + causal_scr[...]
            st_scr[buf, h] = st
            mx_scr[buf, h] = jnp.max(st, axis=0, keepdims=True)

    def softmax(j, buf):
        tile_dist = (j * tk - i * tq).astype(F32)
        for h in range(H):
            off = slopes[h] * tile_dist
            m_prev = m_scr[h]
            m_next = jnp.maximum(m_prev, mx_scr[buf, h] + off)
            alpha_scr[buf, h] = jnp.exp2(m_prev - m_next)
            p_scr[buf, h] = jnp.exp2(st_scr[buf, h] - (m_next - off)).astype(BF16)
            m_scr[h] = m_next

    def weighted_values(j, buf):
        for h in range(H):
            acc_scr[h] = alpha_scr[buf, h] * acc_scr[h] + jnp.dot(vt_scr[h, j], p_scr[buf, h],
                                                                  preferred_element_type=F32)

    def run(stage, base, offset):
        pos, buf = base + offset, offset % 2
        if offset > 0:
            tile = pos - 1
        else:
            tile = jnp.where(pos == 0, i, pos - 1)
        if stage == "produce":
            produce(tile, buf)
        elif stage == "softmax":
            softmax(tile, buf)
        else:
            weighted_values(tile, buf)

    def steady(base):
        for t in range(1, POSITIONS_PER_TRIP + 1):
            run("produce", base, t + 1)
            run("weighted_values", base, t - 1)
            run("softmax", base, t)

    def drain(base, r):
        for t in range(1, r + 1):
            if t + 1 <= r:
                run("produce", base, t + 1)
            run("weighted_values", base, t - 1)
            run("softmax", base, t)
        run("weighted_values", base, r)

    def reset():
        m_scr[...] = jnp.full(m_scr.shape, -jnp.inf, F32)
        acc_scr[...] = jnp.zeros(acc_scr.shape, F32)

    @pl.when(i == 0)
    def _():
        prepare_keys_values()
        load_slopes()
        load_queries(q_ref)
        produce(0, 0, masked=True)
        reset()
        softmax(0, 0)
        weighted_values(0, 0)

    @pl.when(i > 0)
    def _():
        reset()
        run("produce", 0, 1)
        run("softmax", 0, 0)

        def trip(t, carry):
            steady(POSITIONS_PER_TRIP * t)
            return carry
        n_trips = (i - 1) // POSITIONS_PER_TRIP
        lax.fori_loop(0, n_trips, trip, 0)

        done = POSITIONS_PER_TRIP * n_trips
        for r in range(1, POSITIONS_PER_TRIP + 1):
            pl.when(i - done == r)(functools.partial(drain, done, r))

    n_q = pl.num_programs(1)
    load_queries(qn_ref)
    produce(jnp.minimum(i + 1, n_q - 1), 0, masked=True)

    f = lambda r: r[...].astype(F32)
    lam = (jnp.exp(jnp.sum(f(lq1_ref) * f(lk1_ref), axis=1, keepdims=True))
           - jnp.exp(jnp.sum(f(lq2_ref) * f(lk2_ref), axis=1, keepdims=True)) + LAM_INIT)
    gain = gsub_ref[...] * (1.0 - LAM_INIT)
    for h in range(H):
        on = acc_scr[h, 0:DV, :] / acc_scr[h, DV:DV + 1, :]
        ot = on[:, 0:tq] - lam * on[:, tq:2 * tq]
        ot = ot * lax.rsqrt(jnp.mean(ot * ot, axis=0, keepdims=True) + RMS_EPS)
        o = ot.T * gain
        gate = za_ref[0, :, h * DV:(h + 1) * DV].astype(F32)
        o_ref[0, :, h * DV:(h + 1) * DV] = (o * gate).astype(o_ref.dtype)


def _diff_attention(qa_t, ka, va_t, za, lam_q1, lam_k1, lam_q2, lam_k2, g_sub_a, *, tq):
    B, S, W = ka.shape
    tk = tq
    assert W == DA_HEADS * LANE and S % tq == 0 and qa_t.shape == (B, W, S) and va_t.shape == (B, W, S)
    vec = lambda v: v.reshape(1, DA_HEAD_DIM)
    vec_spec = pl.BlockSpec((1, DA_HEAD_DIM), lambda b, i: (0, 0))
    tile_spec = pl.BlockSpec((1, tq, W), lambda b, i: (b, i, 0))
    seq_spec = pl.BlockSpec((1, S, W), lambda b, i: (b, 0, 0))
    n_q = S // tq
    q_spec = pl.BlockSpec((1, W, tq), lambda b, i: (b, 0, i))
    next_q_spec = pl.BlockSpec((1, W, tq), lambda b, i: (b, 0, jnp.minimum(i + 1, n_q - 1)))
    seq_t_spec = pl.BlockSpec((1, W, S), lambda b, i: (b, 0, 0))
    kern = functools.partial(_attn_kernel, tq=tq, tk=tk)
    return pl.pallas_call(
        kern,
        grid=(B, n_q),
        in_specs=[vec_spec, vec_spec, vec_spec, vec_spec, q_spec, next_q_spec, seq_spec, seq_t_spec,
                  tile_spec,
                  pl.BlockSpec((1, DA_V_DIM), lambda b, i: (0, 0))],
        out_specs=tile_spec,
        out_shape=jax.ShapeDtypeStruct((B, S, W), BF16),
        scratch_shapes=[pltpu.VMEM((DA_HEADS, 2 * LANE, 2 * tq), BF16),
                        pltpu.VMEM((DA_HEADS, S, 2 * LANE), BF16),
                        pltpu.VMEM((DA_HEADS, S // tk, DA_V_DIM + ONES_ROWS, tk), BF16),
                        pltpu.VMEM((tk, 2 * tq), F32),
                        pltpu.VMEM((2, DA_HEADS, tk, 2 * tq), F32),
                        pltpu.VMEM((2, DA_HEADS, 1, 2 * tq), F32),
                        pltpu.VMEM((2, DA_HEADS, tk, 2 * tq), BF16),
                        pltpu.VMEM((2, DA_HEADS, 1, 2 * tq), F32),
                        pltpu.VMEM((DA_HEADS, 1, 2 * tq), F32),
                        pltpu.VMEM((DA_HEADS, DA_V_DIM + ONES_ROWS, 2 * tq), F32)],
        compiler_params=pltpu.CompilerParams(
            dimension_semantics=("parallel", "arbitrary"), vmem_limit_bytes=VMEM_LIMIT),
        name="diffattn",
    )(vec(lam_q1), vec(lam_k1), vec(lam_q2), vec(lam_k2), qa_t, qa_t, ka, va_t, za,
      g_sub_a.reshape(1, DA_V_DIM))


def _gla_tables(C, n_heads, dk):
    t = np.arange(C)[:, None]
    u = np.arange(C)[None, :]
    sums = [(u <= t), (u > t)]
    halves = []
    m = C // 2
    while m >= 1:
        halves.append(m)
        m //= 2
    upper_rows, pair_masks = [], []
    for m in halves:
        blk = 2 * m
        mid = (t // blk) * blk + m
        upper = (t % blk) >= m
        sums.append(np.where(upper, (u >= mid) & (u <= t), (u > t) & (u <= mid - 1)))
        upper_rows.append(np.broadcast_to(upper, (C, n_heads * dk)))
        tt, ss = np.arange(C)[:, None], np.arange(C)[None, :]
        pair_masks.append(((tt // blk) == (ss // blk)) & ((tt % blk) >= m) & ((ss % blk) < m))
    pair_masks.append(np.eye(C, dtype=bool))
    sum_sel = np.concatenate(sums, axis=0).astype(np.float32)
    upper_rows = np.stack(upper_rows).astype(np.float32)
    eye_h = np.eye(n_heads, dtype=bool)
    pair_bd = np.stack([np.kron(eye_h, pm) for pm in pair_masks]).astype(np.float32)
    head_of_lane = np.arange(n_heads * dk)[None, :] // dk
    head_of_row = np.repeat(np.arange(n_heads), C)[:, None]
    head_mask = (head_of_lane == head_of_row).astype(np.float32)
    return sum_sel, upper_rows, pair_bd, head_mask


CHUNKS_PER_TRIP = 32


def _gla_kernel(q_ref, k_ref, g_ref, v_ref, z_ref, gsub_ref, sumsel_ref, upper_ref, pair_ref, hmask_ref,
                o_ref, state_scr, decay_scr, mixed_scr, *, chunk, n_chunks, n_levels):
    C = chunk
    H, DV = GLA_HEADS, GLA_DV

    @pl.when(pl.program_id(1) == 0)
    def _():
        state_scr[...] = jnp.zeros(state_scr.shape, F32)

    hmask = hmask_ref[...]

    def stack_heads(a):
        return jnp.concatenate([a.astype(BF16)] * H, axis=0) * hmask

    def decays(c, buf):
        r0 = pl.multiple_of(c * C, C)
        g = g_ref[0, pl.ds(r0, C), :]
        g_hi = g.astype(BF16)
        g_lo = (g - g_hi.astype(F32)).astype(BF16)
        g2 = jnp.concatenate([g_hi, g_lo], axis=0)
        d_all = jnp.dot(sumsel_ref[...], g2, preferred_element_type=F32)
        decay_scr[buf] = jnp.exp(d_all)

    def mix(c, buf, state_t):
        r0 = pl.multiple_of(c * C, C)
        q = q_ref[0, pl.ds(r0, C), :].astype(F32)
        k = k_ref[0, pl.ds(r0, C), :].astype(F32)
        v = v_ref[0, pl.ds(r0, C), :]
        v_st = jnp.concatenate([v[:, hh * DV:(hh + 1) * DV] for hh in range(H)], axis=0)
        v_st_t = v_st.astype(F32).T.astype(BF16)
        e_all = decay_scr.at[buf]

        k_out = stack_heads(k * e_all[C:2 * C])
        new_state_t = (e_all[C - 1:C] * state_t
                       + jnp.dot(v_st_t, k_out, preferred_element_type=F32))

        a_bd = None
        for lvl in range(n_levels + 1):
            if lvl < n_levels:
                e = e_all[(2 + lvl) * C:(3 + lvl) * C]
                xs = stack_heads(jnp.where(upper_ref[lvl] > 0.5, q, k) * e)
                prod = lax.dot_general(xs, xs, NT_DIMS, preferred_element_type=F32)
            else:
                prod = lax.dot_general(stack_heads(q), stack_heads(k), NT_DIMS, preferred_element_type=F32)
            term = prod * pair_ref[lvl]
            a_bd = term if a_bd is None else a_bd + term

        q_in = stack_heads(q * e_all[0:C])
        mixed_scr[buf] = (lax.dot_general(q_in, state_t.astype(BF16), NT_DIMS, preferred_element_type=F32)
                          + jnp.dot(a_bd.astype(BF16), v_st, preferred_element_type=F32))
        return new_state_t

    def finish(c, buf):
        r0 = pl.multiple_of(c * C, C)
        for hh in range(H):
            o_h = _rms(mixed_scr[buf, hh * C:(hh + 1) * C, :], gsub_ref[...])
            gate = z_ref[0, pl.ds(r0, C), hh * DV:(hh + 1) * DV].astype(F32)
            o_ref[0, pl.ds(r0, C), hh * DV:(hh + 1) * DV] = (o_h * gate).astype(o_ref.dtype)

    decays(0, 0)
    mixed_scr[1] = jnp.zeros(mixed_scr.shape[1:], F32)

    def trip(cc, carry):
        c0 = CHUNKS_PER_TRIP * cc
        state = state_scr[...]
        for u in range(CHUNKS_PER_TRIP):
            c = c0 + u
            decays(jnp.minimum(c + 1, n_chunks - 1), (u + 1) % 2)
            state = mix(c, u % 2, state)
            finish(jnp.maximum(c - 1, 0), (u + 1) % 2)
        state_scr[...] = state
        return carry

    lax.fori_loop(0, n_chunks // CHUNKS_PER_TRIP, trip, 0)
    finish(n_chunks - 1, 1)


def _gated_linear_attention(qb, kb, glog, vb, zb, g_sub_b, *, chunk, rows_per_step):
    B, S, b_qk = qb.shape
    b_v = vb.shape[-1]
    sum_sel, upper_rows, pair_bd, head_mask = _gla_tables(chunk, GLA_HEADS, GLA_DK)
    n_levels = upper_rows.shape[0]
    T = rows_per_step

    def row_spec(width):
        return pl.BlockSpec((1, T, width), lambda b, i: (b, i, 0))

    def const_spec(shape):
        return pl.BlockSpec(shape, lambda b, i: (0,) * len(shape))

    sum_sel2 = np.concatenate([sum_sel, sum_sel], axis=1)
    assert CHUNKS_PER_TRIP % 2 == 0 and (T // chunk) % CHUNKS_PER_TRIP == 0
    kern = functools.partial(_gla_kernel, chunk=chunk, n_chunks=T // chunk, n_levels=n_levels)
    return pl.pallas_call(
        kern,
        grid=(B, S // T),
        in_specs=[row_spec(b_qk), row_spec(b_qk), row_spec(b_qk), row_spec(b_v), row_spec(b_v),
                  const_spec((1, GLA_DV)), const_spec(sum_sel2.shape), const_spec(upper_rows.shape),
                  const_spec(pair_bd.shape), const_spec(head_mask.shape)],
        out_specs=row_spec(b_v),
        out_shape=jax.ShapeDtypeStruct((B, S, b_v), BF16),
        scratch_shapes=[pltpu.VMEM((GLA_DV, b_qk), F32),
                        pltpu.VMEM((2,) + (sum_sel.shape[0], b_qk), F32),
                        pltpu.VMEM((2, GLA_HEADS * chunk, GLA_DV), F32)],
        compiler_params=pltpu.CompilerParams(
            dimension_semantics=("parallel", "arbitrary"), vmem_limit_bytes=VMEM_LIMIT),
        name="gla",
    )(qb, kb, glog, vb, zb, g_sub_b.reshape(1, GLA_DV), jnp.asarray(sum_sel2, BF16),
      jnp.asarray(upper_rows), jnp.asarray(pair_bd), jnp.asarray(head_mask, BF16))


def _merge_kernel(x_ref, ua_ref, ub_ref, ga_ref, gb_ref, wua_ref, wub_ref, wout_ref, gpost_ref, o_ref, *,
                  sub_rows):
    for r0 in range(0, x_ref.shape[1], sub_rows):
        rows = slice(r0, r0 + sub_rows)
        ya = jnp.dot(ua_ref[0, rows, :], wua_ref[...], preferred_element_type=F32)
        yb = jnp.dot(ub_ref[0, rows, :], wub_ref[...], preferred_element_type=F32)
        y = (jax.nn.sigmoid(ga_ref[0, rows, :].astype(F32)) * ya
             + jax.nn.sigmoid(gb_ref[0, rows, :].astype(F32)) * yb)
        out = jnp.dot(y.astype(BF16), wout_ref[...], preferred_element_type=F32)
        o_ref[0, rows, :] = x_ref[0, rows, :] + _rms(out, gpost_ref[...])


def _merge(x, ua, ub, ga, gb, w_up_a, w_up_b, w_out, g_post, *, tm, sub_rows):
    B, S, D = x.shape
    assert tm % sub_rows == 0

    def row_spec(width):
        return pl.BlockSpec((1, tm, width), lambda b, i: (b, i, 0))

    def const_spec(shape):
        return pl.BlockSpec(shape, lambda b, i: (0,) * len(shape), pipeline_mode=pl.Buffered(1))

    return pl.pallas_call(
        functools.partial(_merge_kernel, sub_rows=sub_rows),
        grid=(B, S // tm),
        in_specs=[row_spec(D), row_spec(ua.shape[-1]), row_spec(ub.shape[-1]), row_spec(D), row_spec(D),
                  const_spec(w_up_a.shape), const_spec(w_up_b.shape), const_spec(w_out.shape),
                  const_spec((1, D))],
        out_specs=row_spec(D),
        out_shape=jax.ShapeDtypeStruct((B, S, D), x.dtype),
        compiler_params=pltpu.CompilerParams(
            dimension_semantics=("parallel", "parallel"), vmem_limit_bytes=VMEM_LIMIT),
        name="merge",
    )(x, ua, ub, ga, gb, w_up_a, w_up_b, w_out, g_post.reshape(1, D))


ROW_TILE = 512
ATTN_TILE = 256
GLA_CHUNK = 64
GLA_STEP_ROWS = 2048


def _layer(x, g_pre, w_in, lam_q1, lam_k1, lam_q2, lam_k2, g_sub_a, w_alpha, b_alpha, g_sub_b,
           w_up_a, w_up_b, w_out, g_post):
    n_mix = 2 * (DA_HEADS * 2 * DA_HEAD_DIM) + 2 * (DA_HEADS * DA_V_DIM) + 2 * (GLA_HEADS * GLA_DK) \
        + 2 * (GLA_HEADS * GLA_DV)
    gate0 = n_mix + GLA_GATE_RANK
    qa_t, ka, va_t, za, qb, kb, vb, zb, ga, gb, glog = _input_projection(
        x, g_pre, w_in.astype(BF16), w_in[:, gate0:].astype(BF16), w_alpha.astype(BF16), b_alpha,
        tm=ROW_TILE)
    ua = _diff_attention(qa_t, ka, va_t, za, lam_q1, lam_k1, lam_q2, lam_k2, g_sub_a, tq=ATTN_TILE)
    ub = _gated_linear_attention(qb, kb, glog, vb, zb, g_sub_b, chunk=GLA_CHUNK, rows_per_step=GLA_STEP_ROWS)
    return _merge(x, ua, ub, ga, gb, w_up_a.astype(BF16), w_up_b.astype(BF16), w_out.astype(BF16),
                  g_post, tm=2 * ROW_TILE, sub_rows=ROW_TILE)


def kernel(x, g_pre, w_in, lam_q1, lam_k1, lam_q2, lam_k2, g_sub_a, w_alpha, b_alpha, g_sub_b, w_up_a, w_up_b, w_out, g_post):
    depth = w_in.shape[0]
    assert depth == 1, "LAM_INIT is specialised to a single layer"
    first = lambda p: p.reshape(p.shape[1:])
    return _layer(x, *(first(p) for p in (g_pre, w_in, lam_q1, lam_k1, lam_q2, lam_k2, g_sub_a, w_alpha,
                                          b_alpha, g_sub_b, w_up_a, w_up_b, w_out, g_post)))
```

```python
import functools
import math

import numpy as np
import jax
import jax.numpy as jnp
from jax import lax
from jax.experimental import pallas as pl
from jax.experimental.pallas import tpu as pltpu

F32 = jnp.float32
BF16 = jnp.bfloat16

DA_HEADS = 4
DA_HEAD_DIM = 64
DA_V_DIM = 128
GLA_HEADS = 4
GLA_DK = 64
GLA_DV = 128
GLA_GATE_RANK = 16
GLA_TAU = 16.0
RMS_EPS = 1e-6
LAYER_IDX = 0
LAM_INIT = 0.8 - 0.6 * math.exp(-0.3 * LAYER_IDX)

LOG2E = math.log2(math.e)
LANE = 128
ONES_ROWS = 16
SLOPE_PARTS = 3
VMEM_LIMIT = 56 * 1024 * 1024

NT_DIMS = (((1,), (1,)), ((), ()))


def _rms(x, g):
    return x * lax.rsqrt(jnp.mean(x * x, axis=-1, keepdims=True) + RMS_EPS) * g


def _silu(z):
    return z * jax.nn.sigmoid(z)


def _inproj_kernel(x_ref, gpre_ref, wmix_ref, wgate_ref, walpha_ref, balpha_ref,
                   qa_ref, ka_ref, va_ref, za_ref, qb_ref, kb_ref, vb_ref, zb_ref,
                   ga_ref, gb_ref, glog_ref, h_scr, t_scr, *, segments, lr_cols, feature_major,
                   q_scale_a, q_scale_b):
    outs = dict(qa=qa_ref, ka=ka_ref, va=va_ref, za=za_ref, qb=qb_ref, kb=kb_ref, vb=vb_ref,
                zb=zb_ref, ga=ga_ref, gb=gb_ref)
    weights = dict(mix=wmix_ref, gate=wgate_ref)
    scales = dict(qa=q_scale_a, qb=q_scale_b)
    xf = x_ref[0]
    h_scr[...] = (xf * gpre_ref[...]).astype(BF16)
    inv_rms = lax.rsqrt(jnp.mean(xf * xf, axis=-1, keepdims=True) + RMS_EPS)
    for name, wname, c0, width, o0 in segments:
        acc = jnp.dot(h_scr[...], weights[wname][:, c0:c0 + width], preferred_element_type=F32)
        acc = acc * (inv_rms * scales[name] if name in scales else inv_rms)
        if name in ("za", "zb"):
            acc = _silu(acc)
        if name in feature_major:
            t_scr[...] = acc
            outs[name][0, o0:o0 + width, :] = t_scr[...].T.astype(BF16)
        else:
            outs[name][0, :, o0:o0 + width] = acc.astype(BF16)
    lr = jnp.dot(h_scr[...], wmix_ref[:, lr_cols[0]:lr_cols[1]], preferred_element_type=F32) * inv_rms
    z = jnp.dot(lr.astype(BF16), walpha_ref[...], preferred_element_type=F32) + balpha_ref[...]
    log_sig = jnp.minimum(z, 0.0) - jnp.log1p(jnp.exp(-jnp.abs(z)))
    glog_ref[0] = log_sig / GLA_TAU


def _input_projection(x, g_pre, w_mix_lr, w_gate, w_alpha, b_alpha, *, tm):
    B, S, D = x.shape
    a_qk, a_v = DA_HEADS * 2 * DA_HEAD_DIM, DA_HEADS * DA_V_DIM
    b_qk, b_v = GLA_HEADS * GLA_DK, GLA_HEADS * GLA_DV
    widths = [("qa", "mix", a_qk), ("ka", "mix", a_qk), ("va", "mix", a_v), ("za", "mix", a_v),
              ("qb", "mix", b_qk), ("kb", "mix", b_qk), ("vb", "mix", b_v), ("zb", "mix", b_v),
              ("ga", "gate", D), ("gb", "gate", D)]
    max_chunk = 512
    segments = []
    col = dict(mix=0, gate=0)
    for name, wname, width in widths:
        for o0 in range(0, width, max_chunk):
            segments.append((name, wname, col[wname] + o0, min(max_chunk, width - o0), o0))
        col[wname] += width
    lr_cols = (col["mix"], col["mix"] + GLA_GATE_RANK)
    assert lr_cols[1] <= w_mix_lr.shape[1] and col["gate"] == w_gate.shape[1]

    def row_spec(width):
        return pl.BlockSpec((1, tm, width), lambda b, i: (b, i, 0))

    def const_spec(shape):
        return pl.BlockSpec(shape, lambda b, i: (0,) * len(shape), pipeline_mode=pl.Buffered(1))

    feature_major = ("qa", "va")
    out_shape, out_specs = [], []
    for name, _, w in widths:
        if name in feature_major:
            out_shape.append(jax.ShapeDtypeStruct((B, w, S), BF16))
            out_specs.append(pl.BlockSpec((1, w, tm), lambda b, i: (b, 0, i)))
        else:
            out_shape.append(jax.ShapeDtypeStruct((B, S, w), BF16))
            out_specs.append(row_spec(w))
    out_shape.append(jax.ShapeDtypeStruct((B, S, b_qk), F32))
    out_specs.append(row_spec(b_qk))
    kern = functools.partial(_inproj_kernel, segments=tuple(segments), lr_cols=lr_cols,
                             feature_major=feature_major,
                             q_scale_a=DA_HEAD_DIM ** -0.5 * LOG2E,
                             q_scale_b=GLA_DK ** -0.5)
    return pl.pallas_call(
        kern,
        grid=(B, S // tm),
        in_specs=[row_spec(D), const_spec((1, D)), const_spec(w_mix_lr.shape),
                  const_spec(w_gate.shape), const_spec(w_alpha.shape), const_spec((1, b_qk))],
        out_specs=out_specs,
        out_shape=out_shape,
        scratch_shapes=[pltpu.VMEM((tm, D), BF16), pltpu.VMEM((tm, max_chunk), F32)],
        compiler_params=pltpu.CompilerParams(
            dimension_semantics=("parallel", "parallel"), vmem_limit_bytes=VMEM_LIMIT),
        name="inproj",
    )(x, g_pre.reshape(1, D), w_mix_lr, w_gate, w_alpha, b_alpha.reshape(1, b_qk))


POSITIONS_PER_TRIP = 8


def _attn_kernel(lq1_ref, lk1_ref, lq2_ref, lk2_ref, q_ref, qn_ref, k_ref, vt_ref, za_ref, gsub_ref,
                 o_ref, qs_scr, ks_scr, vt_scr, causal_scr, st_scr, mx_scr, p_scr, alpha_scr, m_scr, acc_scr,
                 *, tq, tk):
    i = pl.program_id(1)
    H, DV = DA_HEADS, DA_V_DIM
    n_kv = vt_scr.shape[1]
    slope_parts, slopes = [], []
    for h in range(H):
        rest = np.float32(2.0 ** (-8.0 * (h + 1) / H) * LOG2E)
        slopes.append(float(rest))
        parts = []
        for _ in range(SLOPE_PARTS):
            part = np.asarray(rest, dtype=BF16)
            parts.append(float(part))
            rest = np.float32(rest - np.float32(part))
        assert rest == 0.0
        slope_parts.append(parts)

    def prepare_keys_values():
        lane = lax.broadcasted_iota(jnp.int32, (tk, LANE), 1)
        key = lax.broadcasted_iota(jnp.int32, (tk, LANE), 0).astype(F32)
        key_lanes = jnp.where(lane < SLOPE_PARTS, key, 0.0).astype(BF16)
        ones_rows = jnp.ones((ONES_ROWS, tk), BF16)
        def prep(j, carry):
            k0 = pl.multiple_of(j * tk, tk)
            for h in range(H):
                ks_scr[h, pl.ds(k0, tk), 0:LANE] = k_ref[0, pl.ds(k0, tk), h * LANE:(h + 1) * LANE]
                ks_scr[h, pl.ds(k0, tk), LANE:2 * LANE] = key_lanes
            return carry
        lax.fori_loop(0, n_kv, prep, 0)
        for j in range(n_kv):
            for h in range(H):
                vt_scr[h, j, 0:DV, :] = vt_ref[0, h * DV:(h + 1) * DV, j * tk:(j + 1) * tk]
                vt_scr[h, j, DV:DV + ONES_ROWS, :] = ones_rows
        key_idx = lax.broadcasted_iota(jnp.int32, (tk, 2 * tq), 0)
        qcol = lax.broadcasted_iota(jnp.int32, (tk, 2 * tq), 1)
        visible = key_idx <= jnp.where(qcol >= tq, qcol - tq, qcol)
        causal_scr[...] = jnp.where(visible, 0.0, -jnp.inf)

    def load_queries(ref):
        row = lax.broadcasted_iota(jnp.int32, (LANE, tq), 0)
        zero = jnp.zeros((LANE, tq), BF16)
        for h in range(H):
            qt = ref[0, h * LANE:(h + 1) * LANE, :]
            qs_scr[h, 0:LANE, 0:tq] = jnp.where(row < DA_HEAD_DIM, qt, zero)
            qs_scr[h, 0:LANE, tq:2 * tq] = jnp.where(row >= DA_HEAD_DIM, qt, zero)

    def load_slopes():
        row = lax.broadcasted_iota(jnp.int32, (LANE, 2 * tq), 0)
        for h in range(H):
            rows = jnp.zeros((LANE, 2 * tq), F32)
            for r, part in enumerate(slope_parts[h]):
                rows = jnp.where(row == r, part, rows)
            qs_scr[h, LANE:2 * LANE, :] = rows.astype(BF16)

    def produce(j, buf, masked=False):
        k0 = pl.multiple_of(j * tk, tk)
        for h in range(H):
            st = jnp.dot(ks_scr[h, pl.ds(k0, tk), :], qs_scr[h], preferred_element_type=F32)
            if masked:
                st = st + causal_scr[...]
            st_scr[buf, h] = st
            mx_scr[buf, h] = jnp.max(st, axis=0, keepdims=True)

    def softmax(j, buf):
        tile_dist = (j * tk - i * tq).astype(F32)
        for h in range(H):
            off = slopes[h] * tile_dist
            m_prev = m_scr[h]
            m_next = jnp.maximum(m_prev, mx_scr[buf, h] + off)
            alpha_scr[buf, h] = jnp.exp2(m_prev - m_next)
            p_scr[buf, h] = jnp.exp2(st_scr[buf, h] - (m_next - off)).astype(BF16)
            m_scr[h] = m_next

    def weighted_values(j, buf):
        for h in range(H):
            acc_scr[h] = alpha_scr[buf, h] * acc_scr[h] + jnp.dot(vt_scr[h, j], p_scr[buf, h],
                                                                  preferred_element_type=F32)

    def run(stage, base, offset):
        pos, buf = base + offset, offset % 2
        if offset > 0:
            tile = pos - 1
        else:
            tile = jnp.where(pos == 0, i, pos - 1)
        if stage == "produce":
            produce(tile, buf)
        elif stage == "softmax":
            softmax(tile, buf)
        else:
            weighted_values(tile, buf)

    def steady(base):
        for t in range(1, POSITIONS_PER_TRIP + 1):
            run("produce", base, t + 1)
            run("weighted_values", base, t - 1)
            run("softmax", base, t)

    def drain(base, r):
        for t in range(1, r + 1):
            if t + 1 <= r:
                run("produce", base, t + 1)
            run("weighted_values", base, t - 1)
            run("softmax", base, t)
        run("weighted_values", base, r)

    def reset():
        m_scr[...] = jnp.full(m_scr.shape, -jnp.inf, F32)
        acc_scr[...] = jnp.zeros(acc_scr.shape, F32)

    @pl.when(i == 0)
    def _():
        prepare_keys_values()
        load_slopes()
        load_queries(q_ref)
        produce(0, 0, masked=True)
        reset()
        softmax(0, 0)
        weighted_values(0, 0)

    @pl.when(i > 0)
    def _():
        reset()
        run("produce", 0, 1)
        run("softmax", 0, 0)

        def trip(t, carry):
            steady(POSITIONS_PER_TRIP * t)
            return carry
        n_trips = (i - 1) // POSITIONS_PER_TRIP
        lax.fori_loop(0, n_trips, trip, 0)

        done = POSITIONS_PER_TRIP * n_trips
        for r in range(1, POSITIONS_PER_TRIP + 1):
            pl.when(i - done == r)(functools.partial(drain, done, r))

    n_q = pl.num_programs(1)
    load_queries(qn_ref)
    produce(jnp.minimum(i + 1, n_q - 1), 0, masked=True)

    f = lambda r: r[...].astype(F32)
    lam = (jnp.exp(jnp.sum(f(lq1_ref) * f(lk1_ref), axis=1, keepdims=True))
           - jnp.exp(jnp.sum(f(lq2_ref) * f(lk2_ref), axis=1, keepdims=True)) + LAM_INIT)
    gain = gsub_ref[...] * (1.0 - LAM_INIT)
    for h in range(H):
        on = acc_scr[h, 0:DV, :] / acc_scr[h, DV:DV + 1, :]
        ot = on[:, 0:tq] - lam * on[:, tq:2 * tq]
        ot = ot * lax.rsqrt(jnp.mean(ot * ot, axis=0, keepdims=True) + RMS_EPS)
        o = ot.T * gain
        gate = za_ref[0, :, h * DV:(h + 1) * DV].astype(F32)
        o_ref[0, :, h * DV:(h + 1) * DV] = (o * gate).astype(o_ref.dtype)


def _diff_attention(qa_t, ka, va_t, za, lam_q1, lam_k1, lam_q2, lam_k2, g_sub_a, *, tq):
    B, S, W = ka.shape
    tk = tq
    assert W == DA_HEADS * LANE and S % tq == 0 and qa_t.shape == (B, W, S) and va_t.shape == (B, W, S)
    vec = lambda v: v.reshape(1, DA_HEAD_DIM)
    vec_spec = pl.BlockSpec((1, DA_HEAD_DIM), lambda b, i: (0, 0))
    tile_spec = pl.BlockSpec((1, tq, W), lambda b, i: (b, i, 0))
    seq_spec = pl.BlockSpec((1, S, W), lambda b, i: (b, 0, 0))
    n_q = S // tq
    q_spec = pl.BlockSpec((1, W, tq), lambda b, i: (b, 0, i))
    next_q_spec = pl.BlockSpec((1, W, tq), lambda b, i: (b, 0, jnp.minimum(i + 1, n_q - 1)))
    seq_t_spec = pl.BlockSpec((1, W, S), lambda b, i: (b, 0, 0))
    kern = functools.partial(_attn_kernel, tq=tq, tk=tk)
    return pl.pallas_call(
        kern,
        grid=(B, n_q),
        in_specs=[vec_spec, vec_spec, vec_spec, vec_spec, q_spec, next_q_spec, seq_spec, seq_t_spec,
                  tile_spec,
                  pl.BlockSpec((1, DA_V_DIM), lambda b, i: (0, 0))],
        out_specs=tile_spec,
        out_shape=jax.ShapeDtypeStruct((B, S, W), BF16),
        scratch_shapes=[pltpu.VMEM((DA_HEADS, 2 * LANE, 2 * tq), BF16),
                        pltpu.VMEM((DA_HEADS, S, 2 * LANE), BF16),
                        pltpu.VMEM((DA_HEADS, S // tk, DA_V_DIM + ONES_ROWS, tk), BF16),
                        pltpu.VMEM((tk, 2 * tq), F32),
                        pltpu.VMEM((2, DA_HEADS, tk, 2 * tq), F32),
                        pltpu.VMEM((2, DA_HEADS, 1, 2 * tq), F32),
                        pltpu.VMEM((2, DA_HEADS, tk, 2 * tq), BF16),
                        pltpu.VMEM((2, DA_HEADS, 1, 2 * tq), F32),
                        pltpu.VMEM((DA_HEADS, 1, 2 * tq), F32),
                        pltpu.VMEM((DA_HEADS, DA_V_DIM + ONES_ROWS, 2 * tq), F32)],
        compiler_params=pltpu.CompilerParams(
            dimension_semantics=("parallel", "arbitrary"), vmem_limit_bytes=VMEM_LIMIT),
        name="diffattn",
    )(vec(lam_q1), vec(lam_k1), vec(lam_q2), vec(lam_k2), qa_t, qa_t, ka, va_t, za,
      g_sub_a.reshape(1, DA_V_DIM))


def _gla_tables(C, n_heads, dk):
    t = np.arange(C)[:, None]
    u = np.arange(C)[None, :]
    sums = [(u <= t), (u > t)]
    halves = []
    m = C // 2
    while m >= 1:
        halves.append(m)
        m //= 2
    upper_rows, pair_masks = [], []
    for m in halves:
        blk = 2 * m
        mid = (t // blk) * blk + m
        upper = (t % blk) >= m
        sums.append(np.where(upper, (u >= mid) & (u <= t), (u > t) & (u <= mid - 1)))
        upper_rows.append(np.broadcast_to(upper, (C, n_heads * dk)))
        tt, ss = np.arange(C)[:, None], np.arange(C)[None, :]
        pair_masks.append(((tt // blk) == (ss // blk)) & ((tt % blk) >= m) & ((ss % blk) < m))
    pair_masks.append(np.eye(C, dtype=bool))
    sum_sel = np.concatenate(sums, axis=0).astype(np.float32)
    upper_rows = np.stack(upper_rows).astype(np.float32)
    eye_h = np.eye(n_heads, dtype=bool)
    pair_bd = np.stack([np.kron(eye_h, pm) for pm in pair_masks]).astype(np.float32)
    head_of_lane = np.arange(n_heads * dk)[None, :] // dk
    head_of_row = np.repeat(np.arange(n_heads), C)[:, None]
    head_mask = (head_of_lane == head_of_row).astype(np.float32)
    return sum_sel, upper_rows, pair_bd, head_mask


CHUNKS_PER_TRIP = 32


def _gla_kernel(q_ref, k_ref, g_ref, v_ref, z_ref, gsub_ref, sumsel_ref, upper_ref, pair_ref, hmask_ref,
                o_ref, state_scr, decay_scr, mixed_scr, *, chunk, n_chunks, n_levels):
    C = chunk
    H, DV = GLA_HEADS, GLA_DV

    @pl.when(pl.program_id(1) == 0)
    def _():
        state_scr[...] = jnp.zeros(state_scr.shape, F32)

    hmask = hmask_ref[...]

    def stack_heads(a):
        return jnp.concatenate([a.astype(BF16)] * H, axis=0) * hmask

    def decays(c, buf):
        r0 = pl.multiple_of(c * C, C)
        g = g_ref[0, pl.ds(r0, C), :]
        g_hi = g.astype(BF16)
        g_lo = (g - g_hi.astype(F32)).astype(BF16)
        g2 = jnp.concatenate([g_hi, g_lo], axis=0)
        d_all = jnp.dot(sumsel_ref[...], g2, preferred_element_type=F32)
        decay_scr[buf] = jnp.exp(d_all)

    def mix(c, buf, state_t):
        r0 = pl.multiple_of(c * C, C)
        q = q_ref[0, pl.ds(r0, C), :].astype(F32)
        k = k_ref[0, pl.ds(r0, C), :].astype(F32)
        v = v_ref[0, pl.ds(r0, C), :]
        v_st = jnp.concatenate([v[:, hh * DV:(hh + 1) * DV] for hh in range(H)], axis=0)
        v_st_t = v_st.astype(F32).T.astype(BF16)
        e_all = decay_scr.at[buf]

        k_out = stack_heads(k * e_all[C:2 * C])
        new_state_t = (e_all[C - 1:C] * state_t
                       + jnp.dot(v_st_t, k_out, preferred_element_type=F32))

        a_bd = None
        for lvl in range(n_levels + 1):
            if lvl < n_levels:
                e = e_all[(2 + lvl) * C:(3 + lvl) * C]
                xs = stack_heads(jnp.where(upper_ref[lvl] > 0.5, q, k) * e)
                prod = lax.dot_general(xs, xs, NT_DIMS, preferred_element_type=F32)
            else:
                prod = lax.dot_general(stack_heads(q), stack_heads(k), NT_DIMS, preferred_element_type=F32)
            term = prod * pair_ref[lvl]
            a_bd = term if a_bd is None else a_bd + term

        q_in = stack_heads(q * e_all[0:C])
        mixed_scr[buf] = (lax.dot_general(q_in, state_t.astype(BF16), NT_DIMS, preferred_element_type=F32)
                          + jnp.dot(a_bd.astype(BF16), v_st, preferred_element_type=F32))
        return new_state_t

    def finish(c, buf):
        r0 = pl.multiple_of(c * C, C)
        for hh in range(H):
            o_h = _rms(mixed_scr[buf, hh * C:(hh + 1) * C, :], gsub_ref[...])
            gate = z_ref[0, pl.ds(r0, C), hh * DV:(hh + 1) * DV].astype(F32)
            o_ref[0, pl.ds(r0, C), hh * DV:(hh + 1) * DV] = (o_h * gate).astype(o_ref.dtype)

    decays(0, 0)
    mixed_scr[1] = jnp.zeros(mixed_scr.shape[1:], F32)

    def trip(cc, carry):
        c0 = CHUNKS_PER_TRIP * cc
        state = state_scr[...]
        for u in range(CHUNKS_PER_TRIP):
            c = c0 + u
            decays(jnp.minimum(c + 1, n_chunks - 1), (u + 1) % 2)
            state = mix(c, u % 2, state)
            finish(jnp.maximum(c - 1, 0), (u + 1) % 2)
        state_scr[...] = state
        return carry

    lax.fori_loop(0, n_chunks // CHUNKS_PER_TRIP, trip, 0)
    finish(n_chunks - 1, 1)


def _gated_linear_attention(qb, kb, glog, vb, zb, g_sub_b, *, chunk, rows_per_step):
    B, S, b_qk = qb.shape
    b_v = vb.shape[-1]
    sum_sel, upper_rows, pair_bd, head_mask = _gla_tables(chunk, GLA_HEADS, GLA_DK)
    n_levels = upper_rows.shape[0]
    T = rows_per_step

    def row_spec(width):
        return pl.BlockSpec((1, T, width), lambda b, i: (b, i, 0))

    def const_spec(shape):
        return pl.BlockSpec(shape, lambda b, i: (0,) * len(shape))

    sum_sel2 = np.concatenate([sum_sel, sum_sel], axis=1)
    assert CHUNKS_PER_TRIP % 2 == 0 and (T // chunk) % CHUNKS_PER_TRIP == 0
    kern = functools.partial(_gla_kernel, chunk=chunk, n_chunks=T // chunk, n_levels=n_levels)
    return pl.pallas_call(
        kern,
        grid=(B, S // T),
        in_specs=[row_spec(b_qk), row_spec(b_qk), row_spec(b_qk), row_spec(b_v), row_spec(b_v),
                  const_spec((1, GLA_DV)), const_spec(sum_sel2.shape), const_spec(upper_rows.shape),
                  const_spec(pair_bd.shape), const_spec(head_mask.shape)],
        out_specs=row_spec(b_v),
        out_shape=jax.ShapeDtypeStruct((B, S, b_v), BF16),
        scratch_shapes=[pltpu.VMEM((GLA_DV, b_qk), F32),
                        pltpu.VMEM((2,) + (sum_sel.shape[0], b_qk), F32),
                        pltpu.VMEM((2, GLA_HEADS * chunk, GLA_DV), F32)],
        compiler_params=pltpu.CompilerParams(
            dimension_semantics=("parallel", "arbitrary"), vmem_limit_bytes=VMEM_LIMIT),
        name="gla",
    )(qb, kb, glog, vb, zb, g_sub_b.reshape(1, GLA_DV), jnp.asarray(sum_sel2, BF16),
      jnp.asarray(upper_rows), jnp.asarray(pair_bd), jnp.asarray(head_mask, BF16))


def _merge_kernel(x_ref, ua_ref, ub_ref, ga_ref, gb_ref, wua_ref, wub_ref, wout_ref, gpost_ref, o_ref, *,
                  sub_rows):
    for r0 in range(0, x_ref.shape[1], sub_rows):
        rows = slice(r0, r0 + sub_rows)
        ya = jnp.dot(ua_ref[0, rows, :], wua_ref[...], preferred_element_type=F32)
        yb = jnp.dot(ub_ref[0, rows, :], wub_ref[...], preferred_element_type=F32)
        y = (jax.nn.sigmoid(ga_ref[0, rows, :].astype(F32)) * ya
             + jax.nn.sigmoid(gb_ref[0, rows, :].astype(F32)) * yb)
        out = jnp.dot(y.astype(BF16), wout_ref[...], preferred_element_type=F32)
        o_ref[0, rows, :] = x_ref[0, rows, :] + _rms(out, gpost_ref[...])


def _merge(x, ua, ub, ga, gb, w_up_a, w_up_b, w_out, g_post, *, tm, sub_rows):
    B, S, D = x.shape
    assert tm % sub_rows == 0

    def row_spec(width):
        return pl.BlockSpec((1, tm, width), lambda b, i: (b, i, 0))

    def const_spec(shape):
        return pl.BlockSpec(shape, lambda b, i: (0,) * len(shape), pipeline_mode=pl.Buffered(1))

    return pl.pallas_call(
        functools.partial(_merge_kernel, sub_rows=sub_rows),
        grid=(B, S // tm),
        in_specs=[row_spec(D), row_spec(ua.shape[-1]), row_spec(ub.shape[-1]), row_spec(D), row_spec(D),
                  const_spec(w_up_a.shape), const_spec(w_up_b.shape), const_spec(w_out.shape),
                  const_spec((1, D))],
        out_specs=row_spec(D),
        out_shape=jax.ShapeDtypeStruct((B, S, D), x.dtype),
        compiler_params=pltpu.CompilerParams(
            dimension_semantics=("parallel", "parallel"), vmem_limit_bytes=VMEM_LIMIT),
        name="merge",
    )(x, ua, ub, ga, gb, w_up_a, w_up_b, w_out, g_post.reshape(1, D))


ROW_TILE = 512
ATTN_TILE = 256
GLA_CHUNK = 64
GLA_STEP_ROWS = 2048


def _layer(x, g_pre, w_in, lam_q1, lam_k1, lam_q2, lam_k2, g_sub_a, w_alpha, b_alpha, g_sub_b,
           w_up_a, w_up_b, w_out, g_post):
    n_mix = 2 * (DA_HEADS * 2 * DA_HEAD_DIM) + 2 * (DA_HEADS * DA_V_DIM) + 2 * (GLA_HEADS * GLA_DK) \
        + 2 * (GLA_HEADS * GLA_DV)
    gate0 = n_mix + GLA_GATE_RANK
    qa_t, ka, va_t, za, qb, kb, vb, zb, ga, gb, glog = _input_projection(
        x, g_pre, w_in.astype(BF16), w_in[:, gate0:].astype(BF16), w_alpha.astype(BF16), b_alpha,
        tm=ROW_TILE)
    ua = _diff_attention(qa_t, ka, va_t, za, lam_q1, lam_k1, lam_q2, lam_k2, g_sub_a, tq=ATTN_TILE)
    ub = _gated_linear_attention(qb, kb, glog, vb, zb, g_sub_b, chunk=GLA_CHUNK, rows_per_step=GLA_STEP_ROWS)
    return _merge(x, ua, ub, ga, gb, w_up_a.astype(BF16), w_up_b.astype(BF16), w_out.astype(BF16),
                  g_post, tm=2 * ROW_TILE, sub_rows=ROW_TILE)


def kernel(x, g_pre, w_in, lam_q1, lam_k1, lam_q2, lam_k2, g_sub_a, w_alpha, b_alpha, g_sub_b, w_up_a, w_up_b, w_out, g_post):
    depth = w_in.shape[0]
    assert depth == 1, "LAM_INIT is specialised to a single layer"
    first = lambda p: p.reshape(p.shape[1:])
    return _layer(x, *(first(p) for p in (g_pre, w_in, lam_q1, lam_k1, lam_q2, lam_k2, g_sub_a, w_alpha,
                                          b_alpha, g_sub_b, w_up_a, w_up_b, w_out, g_post)))
```

```python
import functools
import math

import numpy as np
import jax
import jax.numpy as jnp
from jax import lax
from jax.experimental import pallas as pl
from jax.experimental.pallas import tpu as pltpu

F32 = jnp.float32
BF16 = jnp.bfloat16

DA_HEADS = 4
DA_HEAD_DIM = 64
DA_V_DIM = 128
GLA_HEADS = 4
GLA_DK = 64
GLA_DV = 128
GLA_GATE_RANK = 16
GLA_TAU = 16.0
RMS_EPS = 1e-6
LAYER_IDX = 0
LAM_INIT = 0.8 - 0.6 * math.exp(-0.3 * LAYER_IDX)

LOG2E = math.log2(math.e)
LANE = 128
ONES_ROWS = 16
SLOPE_PARTS = 3
VMEM_LIMIT = 56 * 1024 * 1024

NT_DIMS = (((1,), (1,)), ((), ()))


def _rms(x, g):
    return x * lax.rsqrt(jnp.mean(x * x, axis=-1, keepdims=True) + RMS_EPS) * g


def _silu(z):
    return z * jax.nn.sigmoid(z)


def _inproj_kernel(x_ref, gpre_ref, wmix_ref, wgate_ref, walpha_ref, balpha_ref,
                   qa_ref, ka_ref, va_ref, za_ref, qb_ref, kb_ref, vb_ref, zb_ref,
                   ga_ref, gb_ref, glog_ref, h_scr, t_scr, *, segments, lr_cols, feature_major,
                   q_scale_a, q_scale_b):
    outs = dict(qa=qa_ref, ka=ka_ref, va=va_ref, za=za_ref, qb=qb_ref, kb=kb_ref, vb=vb_ref,
                zb=zb_ref, ga=ga_ref, gb=gb_ref)
    weights = dict(mix=wmix_ref, gate=wgate_ref)
    scales = dict(qa=q_scale_a, qb=q_scale_b)
    xf = x_ref[0]
    h_scr[...] = (xf * gpre_ref[...]).astype(BF16)
    inv_rms = lax.rsqrt(jnp.mean(xf * xf, axis=-1, keepdims=True) + RMS_EPS)
    for name, wname, c0, width, o0 in segments:
        acc = jnp.dot(h_scr[...], weights[wname][:, c0:c0 + width], preferred_element_type=F32)
        acc = acc * (inv_rms * scales[name] if name in scales else inv_rms)
        if name in ("za", "zb"):
            acc = _silu(acc)
        if name in ("ga", "gb"):
            acc = jax.nn.sigmoid(acc)
        if name in feature_major:
            t_scr[...] = acc
            outs[name][0, o0:o0 + width, :] = t_scr[...].T.astype(BF16)
        else:
            outs[name][0, :, o0:o0 + width] = acc.astype(BF16)
    lr = jnp.dot(h_scr[...], wmix_ref[:, lr_cols[0]:lr_cols[1]], preferred_element_type=F32) * inv_rms
    z = jnp.dot(lr.astype(BF16), walpha_ref[...], preferred_element_type=F32) + balpha_ref[...]
    log_sig = jnp.minimum(z, 0.0) - jnp.log1p(jnp.exp(-jnp.abs(z)))
    glog_ref[0] = log_sig / GLA_TAU


def _input_projection(x, g_pre, w_mix_lr, w_gate, w_alpha, b_alpha, *, tm):
    B, S, D = x.shape
    a_qk, a_v = DA_HEADS * 2 * DA_HEAD_DIM, DA_HEADS * DA_V_DIM
    b_qk, b_v = GLA_HEADS * GLA_DK, GLA_HEADS * GLA_DV
    widths = [("qa", "mix", a_qk), ("ka", "mix", a_qk), ("va", "mix", a_v), ("za", "mix", a_v),
              ("qb", "mix", b_qk), ("kb", "mix", b_qk), ("vb", "mix", b_v), ("zb", "mix", b_v),
              ("ga", "gate", D), ("gb", "gate", D)]
    max_chunk = 512
    segments = []
    col = dict(mix=0, gate=0)
    for name, wname, width in widths:
        for o0 in range(0, width, max_chunk):
            segments.append((name, wname, col[wname] + o0, min(max_chunk, width - o0), o0))
        col[wname] += width
    lr_cols = (col["mix"], col["mix"] + GLA_GATE_RANK)
    assert lr_cols[1] <= w_mix_lr.shape[1] and col["gate"] == w_gate.shape[1]

    def row_spec(width):
        return pl.BlockSpec((1, tm, width), lambda b, i: (b, i, 0))

    def const_spec(shape):
        return pl.BlockSpec(shape, lambda b, i: (0,) * len(shape), pipeline_mode=pl.Buffered(1))

    feature_major = ("qa", "va")
    out_shape, out_specs = [], []
    for name, _, w in widths:
        if name in feature_major:
            out_shape.append(jax.ShapeDtypeStruct((B, w, S), BF16))
            out_specs.append(pl.BlockSpec((1, w, tm), lambda b, i: (b, 0, i)))
        else:
            out_shape.append(jax.ShapeDtypeStruct((B, S, w), BF16))
            out_specs.append(row_spec(w))
    out_shape.append(jax.ShapeDtypeStruct((B, S, b_qk), F32))
    out_specs.append(row_spec(b_qk))
    kern = functools.partial(_inproj_kernel, segments=tuple(segments), lr_cols=lr_cols,
                             feature_major=feature_major,
                             q_scale_a=DA_HEAD_DIM ** -0.5 * LOG2E,
                             q_scale_b=GLA_DK ** -0.5)
    return pl.pallas_call(
        kern,
        grid=(B, S // tm),
        in_specs=[row_spec(D), const_spec((1, D)), const_spec(w_mix_lr.shape),
                  const_spec(w_gate.shape), const_spec(w_alpha.shape), const_spec((1, b_qk))],
        out_specs=out_specs,
        out_shape=out_shape,
        scratch_shapes=[pltpu.VMEM((tm, D), BF16), pltpu.VMEM((tm, max_chunk), F32)],
        compiler_params=pltpu.CompilerParams(
            dimension_semantics=("parallel", "parallel"), vmem_limit_bytes=VMEM_LIMIT),
        name="inproj",
    )(x, g_pre.reshape(1, D), w_mix_lr, w_gate, w_alpha, b_alpha.reshape(1, b_qk))


POSITIONS_PER_TRIP = 8


def _attn_kernel(lq1_ref, lk1_ref, lq2_ref, lk2_ref, q_ref, qn_ref, k_ref, vt_ref, za_ref, gsub_ref,
                 o_ref, qs_scr, ks_scr, vt_scr, causal_scr, st_scr, mx_scr, p_scr, alpha_scr, m_scr, acc_scr,
                 *, tq, tk):
    i = pl.program_id(1)
    H, DV = DA_HEADS, DA_V_DIM
    n_kv = vt_scr.shape[1]
    slope_parts, slopes = [], []
    for h in range(H):
        rest = np.float32(2.0 ** (-8.0 * (h + 1) / H) * LOG2E)
        slopes.append(float(rest))
        parts = []
        for _ in range(SLOPE_PARTS):
            part = np.asarray(rest, dtype=BF16)
            parts.append(float(part))
            rest = np.float32(rest - np.float32(part))
        assert rest == 0.0
        slope_parts.append(parts)

    def prepare_keys_values():
        lane = lax.broadcasted_iota(jnp.int32, (tk, LANE), 1)
        key = lax.broadcasted_iota(jnp.int32, (tk, LANE), 0).astype(F32)
        key_lanes = jnp.where(lane < SLOPE_PARTS, key, 0.0).astype(BF16)
        ones_rows = jnp.ones((ONES_ROWS, tk), BF16)
        def prep(j, carry):
            k0 = pl.multiple_of(j * tk, tk)
            for h in range(H):
                ks_scr[h, pl.ds(k0, tk), 0:LANE] = k_ref[0, pl.ds(k0, tk), h * LANE:(h + 1) * LANE]
                ks_scr[h, pl.ds(k0, tk), LANE:2 * LANE] = key_lanes
            return carry
        lax.fori_loop(0, n_kv, prep, 0)
        for j in range(n_kv):
            for h in range(H):
                vt_scr[h, j, 0:DV, :] = vt_ref[0, h * DV:(h + 1) * DV, j * tk:(j + 1) * tk]
                vt_scr[h, j, DV:DV + ONES_ROWS, :] = ones_rows
        key_idx = lax.broadcasted_iota(jnp.int32, (tk, 2 * tq), 0)
        qcol = lax.broadcasted_iota(jnp.int32, (tk, 2 * tq), 1)
        visible = key_idx <= jnp.where(qcol >= tq, qcol - tq, qcol)
        causal_scr[...] = jnp.where(visible, 0.0, -jnp.inf)

    def load_queries(ref):
        row = lax.broadcasted_iota(jnp.int32, (LANE, tq), 0)
        zero = jnp.zeros((LANE, tq), BF16)
        for h in range(H):
            qt = ref[0, h * LANE:(h + 1) * LANE, :]
            qs_scr[h, 0:LANE, 0:tq] = jnp.where(row < DA_HEAD_DIM, qt, zero)
            qs_scr[h, 0:LANE, tq:2 * tq] = jnp.where(row >= DA_HEAD_DIM, qt, zero)

    def load_slopes():
        row = lax.broadcasted_iota(jnp.int32, (LANE, 2 * tq), 0)
        for h in range(H):
            rows = jnp.zeros((LANE, 2 * tq), F32)
            for r, part in enumerate(slope_parts[h]):
                rows = jnp.where(row == r, part, rows)
            qs_scr[h, LANE:2 * LANE, :] = rows.astype(BF16)

    def produce(j, buf, masked=False):
        k0 = pl.multiple_of(j * tk, tk)
        for h in range(H):
            st = jnp.dot(ks_scr[h, pl.ds(k0, tk), :], qs_scr[h], preferred_element_type=F32)
            if masked:
                st = st + causal_scr[...]
            st_scr[buf, h] = st
            mx_scr[buf, h] = jnp.max(st, axis=0, keepdims=True)

    def softmax(j, buf):
        tile_dist = (j * tk - i * tq).astype(F32)
        for h in range(H):
            off = slopes[h] * tile_dist
            m_prev = m_scr[h]
            m_next = jnp.maximum(m_prev, mx_scr[buf, h] + off)
            alpha_scr[buf, h] = jnp.exp2(m_prev - m_next)
            p_scr[buf, h] = jnp.exp2(st_scr[buf, h] - (m_next - off)).astype(BF16)
            m_scr[h] = m_next

    def weighted_values(j, buf):
        for h in range(H):
            acc_scr[h] = alpha_scr[buf, h] * acc_scr[h] + jnp.dot(vt_scr[h, j], p_scr[buf, h],
                                                                  preferred_element_type=F32)

    def run(stage, base, offset):
        pos, buf = base + offset, offset % 2
        if offset > 0:
            tile = pos - 1
        else:
            tile = jnp.where(pos == 0, i, pos - 1)
        if stage == "produce":
            produce(tile, buf)
        elif stage == "softmax":
            softmax(tile, buf)
        else:
            weighted_values(tile, buf)

    def steady(base):
        for t in range(1, POSITIONS_PER_TRIP + 1):
            run("produce", base, t + 1)
            run("weighted_values", base, t - 1)
            run("softmax", base, t)

    def drain(base, r):
        for t in range(1, r + 1):
            if t + 1 <= r:
                run("produce", base, t + 1)
            run("weighted_values", base, t - 1)
            run("softmax", base, t)
        run("weighted_values", base, r)

    def reset():
        m_scr[...] = jnp.full(m_scr.shape, -jnp.inf, F32)
        acc_scr[...] = jnp.zeros(acc_scr.shape, F32)

    @pl.when(i == 0)
    def _():
        prepare_keys_values()
        load_slopes()
        load_queries(q_ref)
        produce(0, 0, masked=True)
        reset()
        softmax(0, 0)
        weighted_values(0, 0)

    @pl.when(i > 0)
    def _():
        reset()
        run("produce", 0, 1)
        run("softmax", 0, 0)

        def trip(t, carry):
            steady(POSITIONS_PER_TRIP * t)
            return carry
        n_trips = (i - 1) // POSITIONS_PER_TRIP
        lax.fori_loop(0, n_trips, trip, 0)

        done = POSITIONS_PER_TRIP * n_trips
        for r in range(1, POSITIONS_PER_TRIP + 1):
            pl.when(i - done == r)(functools.partial(drain, done, r))

    n_q = pl.num_programs(1)
    load_queries(qn_ref)
    produce(jnp.minimum(i + 1, n_q - 1), 0, masked=True)

    f = lambda r: r[...].astype(F32)
    lam = (jnp.exp(jnp.sum(f(lq1_ref) * f(lk1_ref), axis=1, keepdims=True))
           - jnp.exp(jnp.sum(f(lq2_ref) * f(lk2_ref), axis=1, keepdims=True)) + LAM_INIT)
    gain = gsub_ref[...] * (1.0 - LAM_INIT)
    for h in range(H):
        on = acc_scr[h, 0:DV, :] / acc_scr[h, DV:DV + 1, :]
        ot = on[:, 0:tq] - lam * on[:, tq:2 * tq]
        ot = ot * lax.rsqrt(jnp.mean(ot * ot, axis=0, keepdims=True) + RMS_EPS)
        o = ot.T * gain
        gate = za_ref[0, :, h * DV:(h + 1) * DV].astype(F32)
        o_ref[0, :, h * DV:(h + 1) * DV] = (o * gate).astype(o_ref.dtype)


def _diff_attention(qa_t, ka, va_t, za, lam_q1, lam_k1, lam_q2, lam_k2, g_sub_a, *, tq):
    B, S, W = ka.shape
    tk = tq
    assert W == DA_HEADS * LANE and S % tq == 0 and qa_t.shape == (B, W, S) and va_t.shape == (B, W, S)
    vec = lambda v: v.reshape(1, DA_HEAD_DIM)
    vec_spec = pl.BlockSpec((1, DA_HEAD_DIM), lambda b, i: (0, 0))
    tile_spec = pl.BlockSpec((1, tq, W), lambda b, i: (b, i, 0))
    seq_spec = pl.BlockSpec((1, S, W), lambda b, i: (b, 0, 0))
    n_q = S // tq
    q_spec = pl.BlockSpec((1, W, tq), lambda b, i: (b, 0, i))
    next_q_spec = pl.BlockSpec((1, W, tq), lambda b, i: (b, 0, jnp.minimum(i + 1, n_q - 1)))
    seq_t_spec = pl.BlockSpec((1, W, S), lambda b, i: (b, 0, 0))
    kern = functools.partial(_attn_kernel, tq=tq, tk=tk)
    return pl.pallas_call(
        kern,
        grid=(B, n_q),
        in_specs=[vec_spec, vec_spec, vec_spec, vec_spec, q_spec, next_q_spec, seq_spec, seq_t_spec,
                  tile_spec,
                  pl.BlockSpec((1, DA_V_DIM), lambda b, i: (0, 0))],
        out_specs=tile_spec,
        out_shape=jax.ShapeDtypeStruct((B, S, W), BF16),
        scratch_shapes=[pltpu.VMEM((DA_HEADS, 2 * LANE, 2 * tq), BF16),
                        pltpu.VMEM((DA_HEADS, S, 2 * LANE), BF16),
                        pltpu.VMEM((DA_HEADS, S // tk, DA_V_DIM + ONES_ROWS, tk), BF16),
                        pltpu.VMEM((tk, 2 * tq), F32),
                        pltpu.VMEM((2, DA_HEADS, tk, 2 * tq), F32),
                        pltpu.VMEM((2, DA_HEADS, 1, 2 * tq), F32),
                        pltpu.VMEM((2, DA_HEADS, tk, 2 * tq), BF16),
                        pltpu.VMEM((2, DA_HEADS, 1, 2 * tq), F32),
                        pltpu.VMEM((DA_HEADS, 1, 2 * tq), F32),
                        pltpu.VMEM((DA_HEADS, DA_V_DIM + ONES_ROWS, 2 * tq), F32)],
        compiler_params=pltpu.CompilerParams(
            dimension_semantics=("parallel", "arbitrary"), vmem_limit_bytes=VMEM_LIMIT),
        name="diffattn",
    )(vec(lam_q1), vec(lam_k1), vec(lam_q2), vec(lam_k2), qa_t, qa_t, ka, va_t, za,
      g_sub_a.reshape(1, DA_V_DIM))


def _gla_tables(C, n_heads, dk):
    t = np.arange(C)[:, None]
    u = np.arange(C)[None, :]
    sums = [(u <= t), (u > t)]
    halves = []
    m = C // 2
    while m >= 1:
        halves.append(m)
        m //= 2
    upper_rows, pair_masks = [], []
    for m in halves:
        blk = 2 * m
        mid = (t // blk) * blk + m
        upper = (t % blk) >= m
        sums.append(np.where(upper, (u >= mid) & (u <= t), (u > t) & (u <= mid - 1)))
        upper_rows.append(np.broadcast_to(upper, (C, n_heads * dk)))
        tt, ss = np.arange(C)[:, None], np.arange(C)[None, :]
        pair_masks.append(((tt // blk) == (ss // blk)) & ((tt % blk) >= m) & ((ss % blk) < m))
    pair_masks.append(np.eye(C, dtype=bool))
    sum_sel = np.concatenate(sums, axis=0).astype(np.float32)
    upper_rows = np.stack(upper_rows).astype(np.float32)
    eye_h = np.eye(n_heads, dtype=bool)
    pair_bd = np.stack([np.kron(eye_h, pm) for pm in pair_masks]).astype(np.float32)
    head_of_lane = np.arange(n_heads * dk)[None, :] // dk
    head_of_row = np.repeat(np.arange(n_heads), C)[:, None]
    head_mask = (head_of_lane == head_of_row).astype(np.float32)
    return sum_sel, upper_rows, pair_bd, head_mask


CHUNKS_PER_TRIP = 32


def _gla_kernel(q_ref, k_ref, g_ref, v_ref, z_ref, gsub_ref, sumsel_ref, upper_ref, pair_ref, hmask_ref,
                o_ref, state_scr, decay_scr, mixed_scr, *, chunk, n_chunks, n_levels):
    C = chunk
    H, DV = GLA_HEADS, GLA_DV

    @pl.when(pl.program_id(1) == 0)
    def _():
        state_scr[...] = jnp.zeros(state_scr.shape, F32)

    hmask = hmask_ref[...]

    def stack_heads(a):
        return jnp.concatenate([a.astype(BF16)] * H, axis=0) * hmask

    def decays(c, buf):
        r0 = pl.multiple_of(c * C, C)
        g = g_ref[0, pl.ds(r0, C), :]
        g_hi = g.astype(BF16)
        g_lo = (g - g_hi.astype(F32)).astype(BF16)
        g2 = jnp.concatenate([g_hi, g_lo], axis=0)
        d_all = jnp.dot(sumsel_ref[...], g2, preferred_element_type=F32)
        decay_scr[buf] = jnp.exp(d_all)

    def mix(c, buf, state_t):
        r0 = pl.multiple_of(c * C, C)
        q = q_ref[0, pl.ds(r0, C), :].astype(F32)
        k = k_ref[0, pl.ds(r0, C), :].astype(F32)
        v = v_ref[0, pl.ds(r0, C), :]
        v_st = jnp.concatenate([v[:, hh * DV:(hh + 1) * DV] for hh in range(H)], axis=0)
        v_st_t = v_st.astype(F32).T.astype(BF16)
        e_all = decay_scr.at[buf]

        k_out = stack_heads(k * e_all[C:2 * C])
        new_state_t = (e_all[C - 1:C] * state_t
                       + jnp.dot(v_st_t, k_out, preferred_element_type=F32))

        a_bd = None
        for lvl in range(n_levels + 1):
            if lvl < n_levels:
                e = e_all[(2 + lvl) * C:(3 + lvl) * C]
                xs = stack_heads(jnp.where(upper_ref[lvl] > 0.5, q, k) * e)
                prod = lax.dot_general(xs, xs, NT_DIMS, preferred_element_type=F32)
            else:
                prod = lax.dot_general(stack_heads(q), stack_heads(k), NT_DIMS, preferred_element_type=F32)
            term = prod * pair_ref[lvl]
            a_bd = term if a_bd is None else a_bd + term

        q_in = stack_heads(q * e_all[0:C])
        mixed_scr[buf] = (lax.dot_general(q_in, state_t.astype(BF16), NT_DIMS, preferred_element_type=F32)
                          + jnp.dot(a_bd.astype(BF16), v_st, preferred_element_type=F32))
        return new_state_t

    def finish(c, buf):
        r0 = pl.multiple_of(c * C, C)
        for hh in range(H):
            o_h = _rms(mixed_scr[buf, hh * C:(hh + 1) * C, :], gsub_ref[...])
            gate = z_ref[0, pl.ds(r0, C), hh * DV:(hh + 1) * DV].astype(F32)
            o_ref[0, pl.ds(r0, C), hh * DV:(hh + 1) * DV] = (o_h * gate).astype(o_ref.dtype)

    decays(0, 0)
    mixed_scr[1] = jnp.zeros(mixed_scr.shape[1:], F32)

    def trip(cc, carry):
        c0 = CHUNKS_PER_TRIP * cc
        state = state_scr[...]
        for u in range(CHUNKS_PER_TRIP):
            c = c0 + u
            decays(jnp.minimum(c + 1, n_chunks - 1), (u + 1) % 2)
            state = mix(c, u % 2, state)
            finish(jnp.maximum(c - 1, 0), (u + 1) % 2)
        state_scr[...] = state
        return carry

    lax.fori_loop(0, n_chunks // CHUNKS_PER_TRIP, trip, 0)
    finish(n_chunks - 1, 1)


def _gated_linear_attention(qb, kb, glog, vb, zb, g_sub_b, *, chunk, rows_per_step):
    B, S, b_qk = qb.shape
    b_v = vb.shape[-1]
    sum_sel, upper_rows, pair_bd, head_mask = _gla_tables(chunk, GLA_HEADS, GLA_DK)
    n_levels = upper_rows.shape[0]
    T = rows_per_step

    def row_spec(width):
        return pl.BlockSpec((1, T, width), lambda b, i: (b, i, 0))

    def const_spec(shape):
        return pl.BlockSpec(shape, lambda b, i: (0,) * len(shape))

    sum_sel2 = np.concatenate([sum_sel, sum_sel], axis=1)
    assert CHUNKS_PER_TRIP % 2 == 0 and (T // chunk) % CHUNKS_PER_TRIP == 0
    kern = functools.partial(_gla_kernel, chunk=chunk, n_chunks=T // chunk, n_levels=n_levels)
    return pl.pallas_call(
        kern,
        grid=(B, S // T),
        in_specs=[row_spec(b_qk), row_spec(b_qk), row_spec(b_qk), row_spec(b_v), row_spec(b_v),
                  const_spec((1, GLA_DV)), const_spec(sum_sel2.shape), const_spec(upper_rows.shape),
                  const_spec(pair_bd.shape), const_spec(head_mask.shape)],
        out_specs=row_spec(b_v),
        out_shape=jax.ShapeDtypeStruct((B, S, b_v), BF16),
        scratch_shapes=[pltpu.VMEM((GLA_DV, b_qk), F32),
                        pltpu.VMEM((2,) + (sum_sel.shape[0], b_qk), F32),
                        pltpu.VMEM((2, GLA_HEADS * chunk, GLA_DV), F32)],
        compiler_params=pltpu.CompilerParams(
            dimension_semantics=("parallel", "arbitrary"), vmem_limit_bytes=VMEM_LIMIT),
        name="gla",
    )(qb, kb, glog, vb, zb, g_sub_b.reshape(1, GLA_DV), jnp.asarray(sum_sel2, BF16),
      jnp.asarray(upper_rows), jnp.asarray(pair_bd), jnp.asarray(head_mask, BF16))


def _merge_kernel(x_ref, ua_ref, ub_ref, ga_ref, gb_ref, wua_ref, wub_ref, wout_ref, gpost_ref, o_ref, *,
                  sub_rows):
    for r0 in range(0, x_ref.shape[1], sub_rows):
        rows = slice(r0, r0 + sub_rows)
        ya = jnp.dot(ua_ref[0, rows, :], wua_ref[...], preferred_element_type=F32)
        yb = jnp.dot(ub_ref[0, rows, :], wub_ref[...], preferred_element_type=F32)
        y = ga_ref[0, rows, :].astype(F32) * ya + gb_ref[0, rows, :].astype(F32) * yb
        out = jnp.dot(y.astype(BF16), wout_ref[...], preferred_element_type=F32)
        o_ref[0, rows, :] = x_ref[0, rows, :] + _rms(out, gpost_ref[...])


def _merge(x, ua, ub, ga, gb, w_up_a, w_up_b, w_out, g_post, *, tm, sub_rows):
    B, S, D = x.shape
    assert tm % sub_rows == 0

    def row_spec(width):
        return pl.BlockSpec((1, tm, width), lambda b, i: (b, i, 0))

    def const_spec(shape):
        return pl.BlockSpec(shape, lambda b, i: (0,) * len(shape), pipeline_mode=pl.Buffered(1))

    return pl.pallas_call(
        functools.partial(_merge_kernel, sub_rows=sub_rows),
        grid=(B, S // tm),
        in_specs=[row_spec(D), row_spec(ua.shape[-1]), row_spec(ub.shape[-1]), row_spec(D), row_spec(D),
                  const_spec(w_up_a.shape), const_spec(w_up_b.shape), const_spec(w_out.shape),
                  const_spec((1, D))],
        out_specs=row_spec(D),
        out_shape=jax.ShapeDtypeStruct((B, S, D), x.dtype),
        compiler_params=pltpu.CompilerParams(
            dimension_semantics=("parallel", "parallel"), vmem_limit_bytes=VMEM_LIMIT),
        name="merge",
    )(x, ua, ub, ga, gb, w_up_a, w_up_b, w_out, g_post.reshape(1, D))


ROW_TILE = 512
ATTN_TILE = 256
GLA_CHUNK = 64
GLA_STEP_ROWS = 2048


def _layer(x, g_pre, w_in, lam_q1, lam_k1, lam_q2, lam_k2, g_sub_a, w_alpha, b_alpha, g_sub_b,
           w_up_a, w_up_b, w_out, g_post):
    n_mix = 2 * (DA_HEADS * 2 * DA_HEAD_DIM) + 2 * (DA_HEADS * DA_V_DIM) + 2 * (GLA_HEADS * GLA_DK) \
        + 2 * (GLA_HEADS * GLA_DV)
    gate0 = n_mix + GLA_GATE_RANK
    qa_t, ka, va_t, za, qb, kb, vb, zb, ga, gb, glog = _input_projection(
        x, g_pre, w_in.astype(BF16), w_in[:, gate0:].astype(BF16), w_alpha.astype(BF16), b_alpha,
        tm=ROW_TILE)
    ua = _diff_attention(qa_t, ka, va_t, za, lam_q1, lam_k1, lam_q2, lam_k2, g_sub_a, tq=ATTN_TILE)
    ub = _gated_linear_attention(qb, kb, glog, vb, zb, g_sub_b, chunk=GLA_CHUNK, rows_per_step=GLA_STEP_ROWS)
    return _merge(x, ua, ub, ga, gb, w_up_a.astype(BF16), w_up_b.astype(BF16), w_out.astype(BF16),
                  g_post, tm=2 * ROW_TILE, sub_rows=ROW_TILE)


def kernel(x, g_pre, w_in, lam_q1, lam_k1, lam_q2, lam_k2, g_sub_a, w_alpha, b_alpha, g_sub_b, w_up_a, w_up_b, w_out, g_post):
    depth = w_in.shape[0]
    assert depth == 1, "LAM_INIT is specialised to a single layer"
    first = lambda p: p.reshape(p.shape[1:])
    return _layer(x, *(first(p) for p in (g_pre, w_in, lam_q1, lam_k1, lam_q2, lam_k2, g_sub_a, w_alpha,
                                          b_alpha, g_sub_b, w_up_a, w_up_b, w_out, g_post)))
```
